```python
import jax, jax.numpy as jnp
from jax import lax
import numpy as np

D_MODEL = 1024
BATCH = 4
SEQ = 4096
DEPTH = 1

N_ATTN_HEADS = 8
HEAD_DIM = 64
ATTN_WIDTH = N_ATTN_HEADS * HEAD_DIM
POOL_WIDTH = D_MODEL - ATTN_WIDTH
POOL_WINDOWS = (2, 4, 8, 16)
N_POOL_GROUPS = len(POOL_WINDOWS)
POOL_GROUP = POOL_WIDTH // N_POOL_GROUPS
IN_WIDTH = 3 * ATTN_WIDTH + POOL_WIDTH
MOBA_BLOCK = 256
MOBA_TOPK = 3
QUERY_CHUNK = 32
D_FF = 4 * D_MODEL
EPS = 1e-6

kernel_name = "hymba_moba_pool_sqrelu_block"


def _alibi_slopes(n_heads):
    return jnp.asarray(2.0 ** (-8.0 * np.arange(1, n_heads + 1) / n_heads), dtype=jnp.float32)


def rms_norm(x, g):
    xf = x.astype(jnp.float32)
    y = xf * lax.rsqrt(jnp.mean(xf * xf, axis=-1, keepdims=True) + EPS)
    return (y * g.astype(jnp.float32)).astype(x.dtype)


def moba_attention(q, k, v):
    B, H, S, Dh = q.shape
    nb = -(-S // MOBA_BLOCK)
    pad = nb * MOBA_BLOCK - S
    kp = jnp.pad(k, ((0, 0), (0, 0), (0, pad), (0, 0)))
    vp = jnp.pad(v, ((0, 0), (0, 0), (0, pad), (0, 0)))
    kb = kp.reshape(B, H, nb, MOBA_BLOCK, Dh)
    vb = vp.reshape(B, H, nb, MOBA_BLOCK, Dh)
    slopes = _alibi_slopes(H)
    scale = HEAD_DIM ** -0.5

    counts = jnp.asarray(np.clip(S - np.arange(nb) * MOBA_BLOCK, 1, MOBA_BLOCK), dtype=jnp.float32)
    kmean = kb.astype(jnp.float32).sum(axis=3) / counts[:, None]
    pos = jnp.arange(S)
    qblk = pos // MOBA_BLOCK
    gate = jnp.einsum('bhtd,bhnd->bhtn', q.astype(jnp.float32), kmean)
    past = jnp.arange(nb)[None, :] < qblk[:, None]
    gate = jnp.where(past[None, None], gate, -jnp.inf)
    n_sel = max(1, min(MOBA_TOPK, nb - 1))
    _, sel = lax.top_k(gate, n_sel)
    sel_valid = jnp.arange(n_sel)[None, :] < qblk[:, None]

    nc = S // QUERY_CHUNK
    qc = q.reshape(B, H, nc, QUERY_CHUNK, Dh).transpose(2, 0, 1, 3, 4)
    selc = sel.reshape(B, H, nc, QUERY_CHUNK, n_sel).transpose(2, 0, 1, 3, 4)
    validc = sel_valid.reshape(nc, QUERY_CHUNK, n_sel)
    bi = jnp.arange(B)[:, None, None, None]
    hi = jnp.arange(H)[None, :, None, None]
    blk_off = jnp.arange(MOBA_BLOCK)

    def chunk(args):
        c, q_c, sel_c, valid_c = args
        t = c * QUERY_CHUNK + jnp.arange(QUERY_CHUNK)
        own = t[0] // MOBA_BLOCK
        k_own = lax.dynamic_index_in_dim(kb, own, axis=2, keepdims=False)
        v_own = lax.dynamic_index_in_dim(vb, own, axis=2, keepdims=False)
        s_own = own * MOBA_BLOCK + blk_off
        sc_own = jnp.einsum('bhtd,bhsd->bhts', q_c, k_own).astype(jnp.float32) * scale
        dist_own = (t[:, None] - s_own[None, :]).astype(jnp.float32)
        sc_own = sc_own - slopes[:, None, None] * dist_own[None]
        sc_own = jnp.where((s_own[None, :] <= t[:, None])[None, None], sc_own, -jnp.inf)

        k_sel = kb[bi, hi, sel_c]
        v_sel = vb[bi, hi, sel_c]
        s_sel = sel_c[..., None] * MOBA_BLOCK + blk_off
        sc_sel = jnp.einsum('bhtd,bhtrsd->bhtrs', q_c, k_sel).astype(jnp.float32) * scale
        dist_sel = (t[None, None, :, None, None] - s_sel).astype(jnp.float32)
        sc_sel = sc_sel - slopes[None, :, None, None, None] * dist_sel
        sc_sel = jnp.where(valid_c[None, None, :, :, None], sc_sel, -jnp.inf)

        scores = jnp.concatenate(
            [sc_own, sc_sel.reshape(B, H, QUERY_CHUNK, n_sel * MOBA_BLOCK)], axis=-1)
        p = jax.nn.softmax(scores, axis=-1)
        p_own = p[..., :MOBA_BLOCK].astype(v.dtype)
        p_sel = p[..., MOBA_BLOCK:].reshape(B, H, QUERY_CHUNK, n_sel, MOBA_BLOCK).astype(v.dtype)
        out = (jnp.einsum('bhts,bhsd->bhtd', p_own, v_own)
               + jnp.einsum('bhtrs,bhtrsd->bhtd', p_sel, v_sel))
        return out.astype(q.dtype)

    out = lax.map(chunk, (jnp.arange(nc), qc, selc, validc))
    return out.transpose(1, 2, 0, 3, 4).reshape(B, H, S, Dh)


def multiscale_pool(u, w_pool, pool_scale):
    B, S, _ = u.shape
    ug = u.astype(jnp.float32).reshape(B, S, N_POOL_GROUPS, POOL_GROUP)
    cs = jnp.cumsum(ug, axis=1)
    pos = jnp.arange(S)
    outs = []
    for g, w in enumerate(POOL_WINDOWS):
        c = cs[:, :, g]
        prev = jnp.pad(c, ((0, 0), (w, 0), (0, 0)))[:, :S]
        cnt = jnp.minimum(pos + 1, w).astype(jnp.float32)[None, :, None]
        outs.append((c - prev) / cnt - ug[:, :, g])
    mixed = jnp.stack(outs, axis=2)
    y = jnp.einsum('bsgc,gcd->bsgd', mixed, w_pool.astype(jnp.float32))
    y = y.reshape(B, S, POOL_WIDTH) * pool_scale.astype(jnp.float32)
    return y.astype(u.dtype)


def setup_inputs(seed: int = 0) -> dict:
    key = jax.random.key(seed)
    ks = jax.random.split(key, 12)
    f32 = jnp.float32
    x = jax.random.normal(ks[0], (BATCH, SEQ, D_MODEL), f32)
    norm_mix = 1.0 + 0.02 * jax.random.normal(ks[1], (DEPTH, D_MODEL), f32)
    w_in = jax.random.normal(ks[2], (DEPTH, D_MODEL, IN_WIDTH), f32) * D_MODEL ** -0.5
    w_pool = jax.random.normal(ks[3], (DEPTH, N_POOL_GROUPS, POOL_GROUP, POOL_GROUP), f32) * POOL_GROUP ** -0.5
    pool_scale = 1.0 + 0.02 * jax.random.normal(ks[4], (DEPTH, POOL_WIDTH), f32)
    w_out = jax.random.normal(ks[5], (DEPTH, D_MODEL, D_MODEL), f32) * D_MODEL ** -0.5
    norm_mlp = 1.0 + 0.02 * jax.random.normal(ks[6], (DEPTH, D_MODEL), f32)
    w_up = jax.random.normal(ks[7], (DEPTH, D_MODEL, D_FF), f32) * D_MODEL ** -0.5
    w_down = jax.random.normal(ks[8], (DEPTH, D_FF, D_MODEL), f32) * D_FF ** -0.5
    norm_final = 1.0 + 0.02 * jax.random.normal(ks[9], (D_MODEL,), f32)
    return {"x": x, "norm_mix": norm_mix, "w_in": w_in, "w_pool": w_pool,
            "pool_scale": pool_scale, "w_out": w_out, "norm_mlp": norm_mlp,
            "w_up": w_up, "w_down": w_down, "norm_final": norm_final}


def reference(x, norm_mix, w_in, w_pool, pool_scale, w_out, norm_mlp, w_up, w_down, norm_final):
    B, S, _ = x.shape
    for l in range(DEPTH):
        h = rms_norm(x, norm_mix[l])
        proj = h @ w_in[l]
        q, k, v, u = jnp.split(proj, [ATTN_WIDTH, 2 * ATTN_WIDTH, 3 * ATTN_WIDTH], axis=-1)
        to_heads = lambda t: t.reshape(B, S, N_ATTN_HEADS, HEAD_DIM).transpose(0, 2, 1, 3)
        a = moba_attention(to_heads(q), to_heads(k), to_heads(v))
        a = a.transpose(0, 2, 1, 3).reshape(B, S, ATTN_WIDTH)
        p = multiscale_pool(u, w_pool[l], pool_scale[l])
        x = x + jnp.concatenate([a, p], axis=-1) @ w_out[l]
        h = rms_norm(x, norm_mlp[l])
        x = x + jnp.square(jax.nn.relu(h @ w_up[l])) @ w_down[l]
    return rms_norm(x, norm_final)
```

```python
import functools

import jax
import jax.numpy as jnp
import numpy as np
from jax import lax
from jax.experimental import pallas as pl
from jax.experimental.pallas import tpu as pltpu

F32 = jnp.float32
BF16 = jnp.bfloat16

N_HEADS = 8
HEAD_DIM = 64
ATTN_WIDTH = N_HEADS * HEAD_DIM
POOL_WINDOWS = (2, 4, 8, 16)
POOL_GROUP = 128
POOL_WIDTH = POOL_GROUP * len(POOL_WINDOWS)
MOBA_BLOCK = 256
MOBA_TOPK = 3
EPS = 1e-6
QK_SCALE = HEAD_DIM ** -0.5
POOL_HALO = 16
HEADS_PER_STEP = 2
LANES = 128

ROW_TILE_IN = 512
ROW_TILE_FFN = 512
FF_CHUNK = 1024
VMEM_LIMIT = 56 * 1024 * 1024

NT_DIMS = (((1,), (1,)), ((), ()))


def _rms_norm(x, g):
    ms = jnp.mean(x * x, axis=-1, keepdims=True)
    return x * lax.rsqrt(ms + EPS) * g


def _inproj_kernel(x_ref, g_ref, wqku_ref, wvt_ref, wpool_ref, pscale_ref,
                   q_ref, k_ref, vt_ref, kmean_ref, p_ref, halo_ref):
    t = pl.program_id(1)
    tile = x_ref.shape[1]
    nblk = tile // MOBA_BLOCK

    h = _rms_norm(x_ref[0], g_ref[...]).astype(BF16)
    qku = jnp.dot(h, wqku_ref[...], preferred_element_type=F32)
    q_ref[0] = (qku[:, :ATTN_WIDTH] * QK_SCALE).astype(BF16)
    kf = qku[:, ATTN_WIDTH:2 * ATTN_WIDTH]
    k_ref[0] = kf.astype(BF16)
    for b in range(nblk):
        kmean_ref[0, 0, b:b + 1, :] = (
            jnp.sum(kf[b * MOBA_BLOCK:(b + 1) * MOBA_BLOCK], axis=0, keepdims=True)
            * (1.0 / MOBA_BLOCK))

    vt = lax.dot_general(wvt_ref[...], h, NT_DIMS, preferred_element_type=F32)
    for b in range(nblk):
        vt_ref[0, b] = vt[:, b * MOBA_BLOCK:(b + 1) * MOBA_BLOCK].astype(BF16)

    u = qku[:, 2 * ATTN_WIDTH:]

    @pl.when(t == 0)
    def _():
        halo_ref[...] = jnp.zeros_like(halo_ref)

    ext = jnp.concatenate([halo_ref[...], u], axis=0)
    halo_ref[...] = u[tile - POOL_HALO:, :]
    pos = t * tile + lax.broadcasted_iota(jnp.int32, (tile, POOL_GROUP), 0)
    for g, w in enumerate(POOL_WINDOWS):
        cols = slice(g * POOL_GROUP, (g + 1) * POOL_GROUP)
        s = ext[:, cols]
        shift = 1
        while shift < w:
            s = s + pltpu.roll(s, shift, axis=0)
            shift *= 2
        cnt = jnp.minimum(pos + 1, w).astype(F32)
        mixed = s[POOL_HALO:, :] / cnt - u[:, cols]
        y = jnp.dot(mixed.astype(BF16), wpool_ref[g], preferred_element_type=F32)
        p_ref[0, :, cols] = (y * pscale_ref[:, cols]).astype(BF16)


def _inproj_call(x, g, wqku, wvt, wpool, pscale):
    B, S, D = x.shape
    T = ROW_TILE_IN
    nblk = T // MOBA_BLOCK
    const = lambda shape: pl.BlockSpec(shape, lambda b, t: (0,) * len(shape),
                                       pipeline_mode=pl.Buffered(1))
    return pl.pallas_call(
        _inproj_kernel,
        grid=(B, S // T),
        in_specs=[
            pl.BlockSpec((1, T, D), lambda b, t: (b, t, 0)),
            const((1, D)),
            const(wqku.shape),
            const(wvt.shape),
            const(wpool.shape),
            const((1, POOL_WIDTH)),
        ],
        out_specs=[
            pl.BlockSpec((1, T, ATTN_WIDTH), lambda b, t: (b, t, 0)),
            pl.BlockSpec((1, T, ATTN_WIDTH), lambda b, t: (b, t, 0)),
            pl.BlockSpec((1, nblk, ATTN_WIDTH, MOBA_BLOCK), lambda b, t: (b, t, 0, 0)),
            pl.BlockSpec((1, 1, nblk, ATTN_WIDTH), lambda b, t: (b, t, 0, 0)),
            pl.BlockSpec((1, T, POOL_WIDTH), lambda b, t: (b, t, 0)),
        ],
        out_shape=[
            jax.ShapeDtypeStruct((B, S, ATTN_WIDTH), BF16),
            jax.ShapeDtypeStruct((B, S, ATTN_WIDTH), BF16),
            jax.ShapeDtypeStruct((B, S // MOBA_BLOCK, ATTN_WIDTH, MOBA_BLOCK), BF16),
            jax.ShapeDtypeStruct((B, S // T, nblk, ATTN_WIDTH), F32),
            jax.ShapeDtypeStruct((B, S, POOL_WIDTH), BF16),
        ],
        scratch_shapes=[pltpu.VMEM((POOL_HALO, POOL_WIDTH), F32)],
        compiler_params=pltpu.CompilerParams(
            dimension_semantics=("arbitrary", "arbitrary"), vmem_limit_bytes=VMEM_LIMIT),
        name="inproj_pool",
    )(x, g, wqku, wvt, wpool, pscale)


def _attn_kernel(slopes_ref, q_ref, k_ref, vt_ref, kmean_ref, o_ref, pen_ref):
    pair = pl.program_id(1)
    i = pl.program_id(2)
    nb = kmean_ref.shape[1]
    blk = MOBA_BLOCK
    neg_inf = -jnp.inf

    qf = q_ref[0].astype(F32)
    lane = lax.broadcasted_iota(jnp.int32, qf.shape, 1)
    kmean = kmean_ref[0].astype(BF16)
    key_idx = lax.broadcasted_iota(jnp.int32, (blk, blk), 0)
    qry_idx = lax.broadcasted_iota(jnp.int32, (blk, blk), 1)
    rel = (qry_idx - key_idx).astype(F32)
    causal = key_idx <= qry_idx
    blk_iota = lax.broadcasted_iota(jnp.int32, (nb, blk), 0)
    k_own = k_ref[0, pl.ds(pl.multiple_of(i * blk, blk), blk), :]

    qh, slope, state = [], [], []
    for hh in range(HEADS_PER_STEP):
        in_head = (lane >= hh * HEAD_DIM) & (lane < (hh + 1) * HEAD_DIM)
        q_h = jnp.where(in_head, qf, 0.0).astype(BF16)
        sl = slopes_ref[pair * HEADS_PER_STEP + hh]
        qh.append(q_h)
        slope.append(sl)

        gate = lax.dot_general(kmean, q_h, NT_DIMS, preferred_element_type=F32)
        gate = jnp.where(blk_iota < i, gate, neg_inf)
        sel = jnp.zeros(gate.shape, jnp.bool_)
        for _ in range(MOBA_TOPK):
            top = jnp.max(gate, axis=0, keepdims=True)
            first = jnp.min(jnp.where(gate == top, blk_iota, nb), axis=0, keepdims=True)
            pick = (blk_iota == first) & (top > neg_inf)
            sel = sel | pick
            gate = jnp.where(pick, neg_inf, gate)
        pen_ref[hh] = jnp.where(sel, 0.0, jnp.inf)

        st = lax.dot_general(k_own, q_h, NT_DIMS, preferred_element_type=F32)
        st = jnp.where(causal, st - sl * rel, neg_inf)
        m = jnp.max(st, axis=0, keepdims=True)
        p = jnp.exp(st - m)
        l = jnp.sum(p, axis=0, keepdims=True)
        vt_own = vt_ref[0, i, hh * HEAD_DIM:(hh + 1) * HEAD_DIM, :]
        acc = jnp.dot(vt_own, p.astype(BF16), preferred_element_type=F32)
        state.extend([m, l, acc])

    def body(j, carry):
        kj = k_ref[0, pl.ds(pl.multiple_of(j * blk, blk), blk), :]
        dist = ((i - j) * blk).astype(F32)
        out = []
        for hh in range(HEADS_PER_STEP):
            m, l, acc = carry[3 * hh:3 * hh + 3]
            st = lax.dot_general(kj, qh[hh], NT_DIMS, preferred_element_type=F32)
            st = st - slope[hh] * (rel + dist)
            pen = pen_ref[hh, pl.ds(j, 1), :]
            m_blk = jnp.max(st, axis=0, keepdims=True)
            m_new = jnp.maximum(m, m_blk - pen)
            alpha = jnp.exp(m - m_new)
            p = jnp.exp(st - (m_new + pen))
            l = alpha * l + jnp.sum(p, axis=0, keepdims=True)
            vtj = vt_ref[0, j, hh * HEAD_DIM:(hh + 1) * HEAD_DIM, :]
            acc = alpha * acc + jnp.dot(vtj, p.astype(BF16), preferred_element_type=F32)
            out.extend([m_new, l, acc])
        return tuple(out)

    state = lax.fori_loop(0, i, body, tuple(state))
    o_t = jnp.concatenate([state[3 * hh + 2] / state[3 * hh + 1]
                           for hh in range(HEADS_PER_STEP)], axis=0)
    o_ref[0] = o_t.T.astype(o_ref.dtype)


def _attn_call(slopes, q, k, vt, kmean):
    B, S, W = q.shape
    nb = S // MOBA_BLOCK
    n_pairs = W // LANES
    grid_spec = pltpu.PrefetchScalarGridSpec(
        num_scalar_prefetch=1,
        grid=(B, n_pairs, nb),
        in_specs=[
            pl.BlockSpec((1, MOBA_BLOCK, LANES), lambda b, p, i, s: (b, i, p)),
            pl.BlockSpec((1, S, LANES), lambda b, p, i, s: (b, 0, p)),
            pl.BlockSpec((1, nb, LANES, MOBA_BLOCK), lambda b, p, i, s: (b, 0, p, 0)),
            pl.BlockSpec((1, nb, LANES), lambda b, p, i, s: (b, 0, p)),
        ],
        out_specs=pl.BlockSpec((1, MOBA_BLOCK, LANES), lambda b, p, i, s: (b, i, p)),
        scratch_shapes=[pltpu.VMEM((HEADS_PER_STEP, nb, MOBA_BLOCK), F32)],
    )
    return pl.pallas_call(
        _attn_kernel,
        grid_spec=grid_spec,
        out_shape=jax.ShapeDtypeStruct((B, S, W), BF16),
        compiler_params=pltpu.CompilerParams(
            dimension_semantics=("arbitrary", "arbitrary", "arbitrary"),
            vmem_limit_bytes=VMEM_LIMIT),
        name="moba_attn",
    )(slopes, q, k, vt, kmean)


def _ffn_kernel(x_ref, a_ref, p_ref, woa_ref, wop_ref, g2_ref, wup_ref, wdn_ref, g3_ref, o_ref):
    x1 = (x_ref[...]
          + jnp.dot(a_ref[...], woa_ref[...], preferred_element_type=F32)
          + jnp.dot(p_ref[...], wop_ref[...], preferred_element_type=F32))
    h = _rms_norm(x1, g2_ref[...]).astype(BF16)
    o_ref[...] = x1
    for c in range(wup_ref.shape[1] // FF_CHUNK):
        cols = slice(c * FF_CHUNK, (c + 1) * FF_CHUNK)
        up = jnp.dot(h, wup_ref[:, cols], preferred_element_type=F32)
        act = jnp.square(jnp.maximum(up, 0.0)).astype(BF16)
        o_ref[...] += jnp.dot(act, wdn_ref[cols, :], preferred_element_type=F32)
    o_ref[...] = _rms_norm(o_ref[...], g3_ref[...])


def _ffn_call(x, a, p, woa, wop, g2, wup, wdn, g3):
    N, D = x.shape
    T = ROW_TILE_FFN
    const = lambda shape: pl.BlockSpec(shape, lambda t: (0,) * len(shape),
                                       pipeline_mode=pl.Buffered(1))
    return pl.pallas_call(
        _ffn_kernel,
        grid=(N // T,),
        in_specs=[
            pl.BlockSpec((T, D), lambda t: (t, 0)),
            pl.BlockSpec((T, ATTN_WIDTH), lambda t: (t, 0)),
            pl.BlockSpec((T, POOL_WIDTH), lambda t: (t, 0)),
            const(woa.shape), const(wop.shape), const((1, D)),
            const(wup.shape), const(wdn.shape), const((1, D)),
        ],
        out_specs=pl.BlockSpec((T, D), lambda t: (t, 0)),
        out_shape=jax.ShapeDtypeStruct((N, D), F32),
        compiler_params=pltpu.CompilerParams(
            dimension_semantics=("arbitrary",), vmem_limit_bytes=VMEM_LIMIT),
        name="outproj_ffn",
    )(x, a, p, woa, wop, g2, wup, wdn, g3)


def _alibi_slopes():
    return jnp.asarray(2.0 ** (-8.0 * np.arange(1, N_HEADS + 1) / N_HEADS), dtype=F32)


def kernel(x, norm_mix, w_in, w_pool, pool_scale, w_out, norm_mlp, w_up, w_down, norm_final):
    B, S, D = x.shape
    assert w_in.shape[0] == 1, "single trunk layer (the final norm is fused into the FFN call)"
    assert S % ROW_TILE_IN == 0 and (B * S) % ROW_TILE_FFN == 0 and S % MOBA_BLOCK == 0
    assert w_in.shape[2] == 3 * ATTN_WIDTH + POOL_WIDTH and D == ATTN_WIDTH + POOL_WIDTH
    wi = w_in[0].astype(BF16)
    wqku = jnp.concatenate([wi[:, :2 * ATTN_WIDTH], wi[:, 3 * ATTN_WIDTH:]], axis=1)
    wvt = wi[:, 2 * ATTN_WIDTH:3 * ATTN_WIDTH].T
    q, k, vt, kmean, p = _inproj_call(
        x, norm_mix[0][None, :], wqku, wvt, w_pool[0].astype(BF16), pool_scale[0][None, :])
    kmean = kmean.reshape(B, S // MOBA_BLOCK, ATTN_WIDTH)
    a = _attn_call(_alibi_slopes(), q, k, vt, kmean)
    wo = w_out[0].astype(BF16)
    y = _ffn_call(x.reshape(B * S, D), a.reshape(B * S, ATTN_WIDTH),
                  p.reshape(B * S, POOL_WIDTH), wo[:ATTN_WIDTH], wo[ATTN_WIDTH:],
                  norm_mlp[0][None, :], w_up[0].astype(BF16), w_down[0].astype(BF16),
                  norm_final[None, :])
    return y.reshape(B, S, D)
```

```python
import functools

import jax
import jax.numpy as jnp
import numpy as np
from jax import lax
from jax.experimental import pallas as pl
from jax.experimental.pallas import tpu as pltpu

F32 = jnp.float32
BF16 = jnp.bfloat16

N_HEADS = 8
HEAD_DIM = 64
ATTN_WIDTH = N_HEADS * HEAD_DIM
POOL_WINDOWS = (2, 4, 8, 16)
POOL_GROUP = 128
POOL_WIDTH = POOL_GROUP * len(POOL_WINDOWS)
MOBA_BLOCK = 256
MOBA_TOPK = 3
EPS = 1e-6
QK_SCALE = HEAD_DIM ** -0.5
POOL_HALO = 16
HEADS_PER_STEP = 2
LANES = 128

ROW_TILE_IN = 512
ROW_TILE_FFN = 512
FF_CHUNK = 1024
VMEM_LIMIT = 56 * 1024 * 1024

NT_DIMS = (((1,), (1,)), ((), ()))


def _rms_norm(x, g):
    ms = jnp.mean(x * x, axis=-1, keepdims=True)
    return x * lax.rsqrt(ms + EPS) * g


def _inproj_kernel(x_ref, g_ref, wqku_ref, wvt_ref, wpool_ref, pscale_ref,
                   q_ref, k_ref, vt_ref, kmean_ref, p_ref, halo_ref):
    t = pl.program_id(1)
    tile = x_ref.shape[1]
    nblk = tile // MOBA_BLOCK

    h = _rms_norm(x_ref[0], g_ref[...]).astype(BF16)
    qku = jnp.dot(h, wqku_ref[...], preferred_element_type=F32)
    q_ref[0] = (qku[:, :ATTN_WIDTH] * QK_SCALE).astype(BF16)
    kf = qku[:, ATTN_WIDTH:2 * ATTN_WIDTH]
    k_ref[0] = kf.astype(BF16)
    for b in range(nblk):
        kmean_ref[0, 0, b:b + 1, :] = (
            jnp.sum(kf[b * MOBA_BLOCK:(b + 1) * MOBA_BLOCK], axis=0, keepdims=True)
            * (1.0 / MOBA_BLOCK))

    vt = lax.dot_general(wvt_ref[...], h, NT_DIMS, preferred_element_type=F32)
    for b in range(nblk):
        vt_ref[0, b] = vt[:, b * MOBA_BLOCK:(b + 1) * MOBA_BLOCK].astype(BF16)

    u = qku[:, 2 * ATTN_WIDTH:]

    @pl.when(t == 0)
    def _():
        halo_ref[...] = jnp.zeros_like(halo_ref)

    ext = jnp.concatenate([halo_ref[...], u], axis=0)
    halo_ref[...] = u[tile - POOL_HALO:, :]
    pos = t * tile + lax.broadcasted_iota(jnp.int32, (tile, POOL_GROUP), 0)
    for g, w in enumerate(POOL_WINDOWS):
        cols = slice(g * POOL_GROUP, (g + 1) * POOL_GROUP)
        s = ext[:, cols]
        shift = 1
        while shift < w:
            s = s + pltpu.roll(s, shift, axis=0)
            shift *= 2
        cnt = jnp.minimum(pos + 1, w).astype(F32)
        mixed = s[POOL_HALO:, :] / cnt - u[:, cols]
        y = jnp.dot(mixed.astype(BF16), wpool_ref[g], preferred_element_type=F32)
        p_ref[0, :, cols] = (y * pscale_ref[:, cols]).astype(BF16)


def _inproj_call(x, g, wqku, wvt, wpool, pscale):
    B, S, D = x.shape
    T = ROW_TILE_IN
    nblk = T // MOBA_BLOCK
    const = lambda shape: pl.BlockSpec(shape, lambda b, t: (0,) * len(shape),
                                       pipeline_mode=pl.Buffered(1))
    return pl.pallas_call(
        _inproj_kernel,
        grid=(B, S // T),
        in_specs=[
            pl.BlockSpec((1, T, D), lambda b, t: (b, t, 0)),
            const((1, D)),
            const(wqku.shape),
            const(wvt.shape),
            const(wpool.shape),
            const((1, POOL_WIDTH)),
        ],
        out_specs=[
            pl.BlockSpec((1, T, ATTN_WIDTH), lambda b, t: (b, t, 0)),
            pl.BlockSpec((1, T, ATTN_WIDTH), lambda b, t: (b, t, 0)),
            pl.BlockSpec((1, nblk, ATTN_WIDTH, MOBA_BLOCK), lambda b, t: (b, t, 0, 0)),
            pl.BlockSpec((1, 1, nblk, ATTN_WIDTH), lambda b, t: (b, t, 0, 0)),
            pl.BlockSpec((1, T, POOL_WIDTH), lambda b, t: (b, t, 0)),
        ],
        out_shape=[
            jax.ShapeDtypeStruct((B, S, ATTN_WIDTH), BF16),
            jax.ShapeDtypeStruct((B, S, ATTN_WIDTH), BF16),
            jax.ShapeDtypeStruct((B, S // MOBA_BLOCK, ATTN_WIDTH, MOBA_BLOCK), BF16),
            jax.ShapeDtypeStruct((B, S // T, nblk, ATTN_WIDTH), F32),
            jax.ShapeDtypeStruct((B, S, POOL_WIDTH), BF16),
        ],
        scratch_shapes=[pltpu.VMEM((POOL_HALO, POOL_WIDTH), F32)],
        compiler_params=pltpu.CompilerParams(
            dimension_semantics=("arbitrary", "arbitrary"), vmem_limit_bytes=VMEM_LIMIT),
        name="inproj_pool",
    )(x, g, wqku, wvt, wpool, pscale)


def _attn_kernel(slopes_ref, q_ref, k_ref, vt_ref, kmean_ref, bias_ref, o_ref,
                 shift_ref, s_ref, mx_ref, m_ref, l_ref, acc_ref):
    pair = pl.program_id(1)
    i = pl.program_id(2)
    nb = kmean_ref.shape[1]
    blk = MOBA_BLOCK
    neg_inf = -jnp.inf
    heads = range(HEADS_PER_STEP)

    qf = q_ref[0].astype(F32)
    lane = lax.broadcasted_iota(jnp.int32, qf.shape, 1)
    kmean = kmean_ref[0].astype(BF16)
    blk_iota = lax.broadcasted_iota(jnp.int32, (nb, blk), 0)
    dist = ((i - blk_iota) * blk).astype(F32)

    qh = []
    for hh in heads:
        in_head = (lane >= hh * HEAD_DIM) & (lane < (hh + 1) * HEAD_DIM)
        q_h = jnp.where(in_head, qf, 0.0).astype(BF16)
        qh.append(q_h)

        gate = lax.dot_general(kmean, q_h, NT_DIMS, preferred_element_type=F32)
        gate = jnp.where(blk_iota < i, gate, neg_inf)
        sel = blk_iota == i
        for _ in range(MOBA_TOPK):
            top = jnp.max(gate, axis=0, keepdims=True)
            first = jnp.min(jnp.where(gate == top, blk_iota, nb), axis=0, keepdims=True)
            pick = (blk_iota == first) & (top > neg_inf)
            sel = sel | pick
            gate = jnp.where(pick, neg_inf, gate)
        slope = slopes_ref[pair * HEADS_PER_STEP + hh]
        shift_ref[hh] = jnp.where(sel, slope * dist, jnp.inf)
        m_ref[hh] = jnp.full(m_ref.shape[1:], neg_inf, F32)
        l_ref[hh] = jnp.zeros(l_ref.shape[1:], F32)
        acc_ref[hh] = jnp.zeros(acc_ref.shape[1:], F32)

    def unit_blocks(t):
        ja = jnp.maximum(i - 2 * t, 0)
        jb = i - 2 * t - 1
        return ja, jnp.maximum(jb, 0), jb >= 0

    def score_dots(t):
        ja, jb, _ = unit_blocks(t)
        kk = jnp.concatenate(
            [k_ref[0, pl.ds(pl.multiple_of(ja * blk, blk), blk), :],
             k_ref[0, pl.ds(pl.multiple_of(jb * blk, blk), blk), :]], axis=0)
        return [lax.dot_general(kk, qh[hh], NT_DIMS, preferred_element_type=F32) for hh in heads]

    def stage_scores(sts, slot, first_table):
        for hh in heads:
            for u, table in ((0, first_table), (1, 0)):
                s_u = sts[hh][u * blk:(u + 1) * blk] + bias_ref[hh, table]
                s_ref[slot, hh, u] = s_u
                mx_ref[slot, hh, u] = jnp.max(s_u, axis=0, keepdims=True)

    def softmax_pv(t, slot):
        ja, jb, b_valid = unit_blocks(t)
        ps, alphas = [], []
        for hh in heads:
            sh_a = shift_ref[hh, pl.ds(ja, 1), :]
            sh_b = jnp.where(b_valid, shift_ref[hh, pl.ds(jb, 1), :], jnp.inf)
            m_old = m_ref[hh]
            m_new = jnp.maximum(m_old, jnp.maximum(mx_ref[slot, hh, 0] - sh_a,
                                                   mx_ref[slot, hh, 1] - sh_b))
            alpha = jnp.exp(m_old - m_new)
            p_a = jnp.exp(s_ref[slot, hh, 0] - (m_new + sh_a))
            p_b = jnp.exp(s_ref[slot, hh, 1] - (m_new + sh_b))
            l_ref[hh] = (alpha * l_ref[hh] + jnp.sum(p_a, axis=0, keepdims=True)
                         + jnp.sum(p_b, axis=0, keepdims=True))
            m_ref[hh] = m_new
            ps.append(jnp.concatenate([p_a.astype(BF16), p_b.astype(BF16)], axis=0))
            alphas.append(alpha)
        for hh in heads:
            rows = slice(hh * HEAD_DIM, (hh + 1) * HEAD_DIM)
            vts = jnp.concatenate([vt_ref[0, ja, rows, :], vt_ref[0, jb, rows, :]], axis=1)
            pv = jnp.dot(vts, ps[hh], preferred_element_type=F32)
            acc_ref[hh] = alphas[hh] * acc_ref[hh] + pv

    stage_scores(score_dots(0), 0, 1)

    def body(t, carry):
        slot = t & 1
        sts = score_dots(t + 1)
        softmax_pv(t, slot)
        stage_scores(sts, 1 - slot, 0)
        return carry

    lax.fori_loop(0, (i + 2) // 2, body, 0)

    o_t = jnp.concatenate([acc_ref[hh] / l_ref[hh] for hh in heads], axis=0)
    o_ref[0] = o_t.T.astype(o_ref.dtype)


def _attn_call(slopes, bias, q, k, vt, kmean):
    B, S, W = q.shape
    nb = S // MOBA_BLOCK
    n_pairs = W // LANES
    hps = HEADS_PER_STEP
    grid_spec = pltpu.PrefetchScalarGridSpec(
        num_scalar_prefetch=1,
        grid=(B, n_pairs, nb),
        in_specs=[
            pl.BlockSpec((1, MOBA_BLOCK, LANES), lambda b, p, i, s: (b, i, p)),
            pl.BlockSpec((1, S, LANES), lambda b, p, i, s: (b, 0, p)),
            pl.BlockSpec((1, nb, LANES, MOBA_BLOCK), lambda b, p, i, s: (b, 0, p, 0)),
            pl.BlockSpec((1, nb, LANES), lambda b, p, i, s: (b, 0, p)),
            pl.BlockSpec((hps, 2, MOBA_BLOCK, MOBA_BLOCK), lambda b, p, i, s: (p, 0, 0, 0)),
        ],
        out_specs=pl.BlockSpec((1, MOBA_BLOCK, LANES), lambda b, p, i, s: (b, i, p)),
        scratch_shapes=[
            pltpu.VMEM((hps, nb, MOBA_BLOCK), F32),
            pltpu.VMEM((2, hps, 2, MOBA_BLOCK, MOBA_BLOCK), F32),
            pltpu.VMEM((2, hps, 2, 1, MOBA_BLOCK), F32),
            pltpu.VMEM((hps, 1, MOBA_BLOCK), F32),
            pltpu.VMEM((hps, 1, MOBA_BLOCK), F32),
            pltpu.VMEM((hps, HEAD_DIM, MOBA_BLOCK), F32),
        ],
    )
    return pl.pallas_call(
        _attn_kernel,
        grid_spec=grid_spec,
        out_shape=jax.ShapeDtypeStruct((B, S, W), BF16),
        compiler_params=pltpu.CompilerParams(
            dimension_semantics=("arbitrary", "arbitrary", "arbitrary"),
            vmem_limit_bytes=VMEM_LIMIT),
        name="moba_attn",
    )(slopes, q, k, vt, kmean, bias)


def _ffn_kernel(x_ref, a_ref, p_ref, woa_ref, wop_ref, g2_ref, wup_ref, wdn_ref, g3_ref, o_ref):
    x1 = (x_ref[...]
          + jnp.dot(a_ref[...], woa_ref[...], preferred_element_type=F32)
          + jnp.dot(p_ref[...], wop_ref[...], preferred_element_type=F32))
    h = _rms_norm(x1, g2_ref[...]).astype(BF16)
    o_ref[...] = x1
    for c in range(wup_ref.shape[1] // FF_CHUNK):
        cols = slice(c * FF_CHUNK, (c + 1) * FF_CHUNK)
        up = jnp.dot(h, wup_ref[:, cols], preferred_element_type=F32)
        act = jnp.square(jnp.maximum(up, 0.0)).astype(BF16)
        o_ref[...] += jnp.dot(act, wdn_ref[cols, :], preferred_element_type=F32)
    o_ref[...] = _rms_norm(o_ref[...], g3_ref[...])


def _ffn_call(x, a, p, woa, wop, g2, wup, wdn, g3):
    N, D = x.shape
    T = ROW_TILE_FFN
    const = lambda shape: pl.BlockSpec(shape, lambda t: (0,) * len(shape),
                                       pipeline_mode=pl.Buffered(1))
    return pl.pallas_call(
        _ffn_kernel,
        grid=(N // T,),
        in_specs=[
            pl.BlockSpec((T, D), lambda t: (t, 0)),
            pl.BlockSpec((T, ATTN_WIDTH), lambda t: (t, 0)),
            pl.BlockSpec((T, POOL_WIDTH), lambda t: (t, 0)),
            const(woa.shape), const(wop.shape), const((1, D)),
            const(wup.shape), const(wdn.shape), const((1, D)),
        ],
        out_specs=pl.BlockSpec((T, D), lambda t: (t, 0)),
        out_shape=jax.ShapeDtypeStruct((N, D), F32),
        compiler_params=pltpu.CompilerParams(
            dimension_semantics=("arbitrary",), vmem_limit_bytes=VMEM_LIMIT),
        name="outproj_ffn",
    )(x, a, p, woa, wop, g2, wup, wdn, g3)


def _alibi_constants():
    slopes = jnp.asarray(2.0 ** (-8.0 * np.arange(1, N_HEADS + 1) / N_HEADS), dtype=F32)
    key = lax.broadcasted_iota(jnp.int32, (MOBA_BLOCK, MOBA_BLOCK), 0)
    qry = lax.broadcasted_iota(jnp.int32, (MOBA_BLOCK, MOBA_BLOCK), 1)
    past = -slopes[:, None, None] * (qry - key).astype(F32)[None]
    own = jnp.where((key <= qry)[None], past, -jnp.inf)
    return slopes, jnp.stack([past, own], axis=1)


def kernel(x, norm_mix, w_in, w_pool, pool_scale, w_out, norm_mlp, w_up, w_down, norm_final):
    B, S, D = x.shape
    assert w_in.shape[0] == 1, "single trunk layer (the final norm is fused into the FFN call)"
    assert S % ROW_TILE_IN == 0 and (B * S) % ROW_TILE_FFN == 0 and S % MOBA_BLOCK == 0
    assert w_in.shape[2] == 3 * ATTN_WIDTH + POOL_WIDTH and D == ATTN_WIDTH + POOL_WIDTH
    wi = w_in[0].astype(BF16)
    wqku = jnp.concatenate([wi[:, :2 * ATTN_WIDTH], wi[:, 3 * ATTN_WIDTH:]], axis=1)
    wvt = wi[:, 2 * ATTN_WIDTH:3 * ATTN_WIDTH].T
    q, k, vt, kmean, p = _inproj_call(
        x, norm_mix[0][None, :], wqku, wvt, w_pool[0].astype(BF16), pool_scale[0][None, :])
    kmean = kmean.reshape(B, S // MOBA_BLOCK, ATTN_WIDTH)
    slopes, bias = _alibi_constants()
    a = _attn_call(slopes, bias, q, k, vt, kmean)
    wo = w_out[0].astype(BF16)
    y = _ffn_call(x.reshape(B * S, D), a.reshape(B * S, ATTN_WIDTH),
                  p.reshape(B * S, POOL_WIDTH), wo[:ATTN_WIDTH], wo[ATTN_WIDTH:],
                  norm_mlp[0][None, :], w_up[0].astype(BF16), w_down[0].astype(BF16),
                  norm_final[None, :])
    return y.reshape(B, S, D)
```

```python
import functools

import jax
import jax.numpy as jnp
import numpy as np
from jax import lax
from jax.experimental import pallas as pl
from jax.experimental.pallas import tpu as pltpu

F32 = jnp.float32
BF16 = jnp.bfloat16

N_HEADS = 8
HEAD_DIM = 64
ATTN_WIDTH = N_HEADS * HEAD_DIM
POOL_WINDOWS = (2, 4, 8, 16)
POOL_GROUP = 128
POOL_WIDTH = POOL_GROUP * len(POOL_WINDOWS)
MOBA_BLOCK = 256
MOBA_TOPK = 3
EPS = 1e-6
QK_SCALE = HEAD_DIM ** -0.5
POOL_HALO = 16
HEADS_PER_STEP = 2
LANES = 128

ROW_TILE_IN = 512
ROW_TILE_FFN = 512
FF_CHUNK = 1024
VMEM_LIMIT = 56 * 1024 * 1024

NT_DIMS = (((1,), (1,)), ((), ()))


def _rms_norm(x, g):
    ms = jnp.mean(x * x, axis=-1, keepdims=True)
    return x * lax.rsqrt(ms + EPS) * g


def _inproj_kernel(x_ref, g_ref, wqku_ref, wvt_ref, wpool_ref, pscale_ref,
                   q_ref, k_ref, vt_ref, kmean_ref, p_ref, halo_ref):
    t = pl.program_id(1)
    tile = x_ref.shape[1]
    nblk = tile // MOBA_BLOCK

    h = _rms_norm(x_ref[0], g_ref[...]).astype(BF16)
    qku = jnp.dot(h, wqku_ref[...], preferred_element_type=F32)
    q_ref[0] = (qku[:, :ATTN_WIDTH] * QK_SCALE).astype(BF16)
    kf = qku[:, ATTN_WIDTH:2 * ATTN_WIDTH]
    k_ref[0] = kf.astype(BF16)
    for b in range(nblk):
        kmean_ref[0, 0, b:b + 1, :] = (
            jnp.sum(kf[b * MOBA_BLOCK:(b + 1) * MOBA_BLOCK], axis=0, keepdims=True)
            * (1.0 / MOBA_BLOCK))

    vt = lax.dot_general(wvt_ref[...], h, NT_DIMS, preferred_element_type=F32)
    for b in range(nblk):
        vt_ref[0, b] = vt[:, b * MOBA_BLOCK:(b + 1) * MOBA_BLOCK].astype(BF16)

    u = qku[:, 2 * ATTN_WIDTH:]

    @pl.when(t == 0)
    def _():
        halo_ref[...] = jnp.zeros_like(halo_ref)

    ext = jnp.concatenate([halo_ref[...], u], axis=0)
    halo_ref[...] = u[tile - POOL_HALO:, :]
    pos = t * tile + lax.broadcasted_iota(jnp.int32, (tile, POOL_GROUP), 0)
    for g, w in enumerate(POOL_WINDOWS):
        cols = slice(g * POOL_GROUP, (g + 1) * POOL_GROUP)
        s = ext[:, cols]
        shift = 1
        while shift < w:
            s = s + pltpu.roll(s, shift, axis=0)
            shift *= 2
        cnt = jnp.minimum(pos + 1, w).astype(F32)
        mixed = s[POOL_HALO:, :] / cnt - u[:, cols]
        y = jnp.dot(mixed.astype(BF16), wpool_ref[g], preferred_element_type=F32)
        p_ref[0, :, cols] = (y * pscale_ref[:, cols]).astype(BF16)


def _inproj_call(x, g, wqku, wvt, wpool, pscale):
    B, S, D = x.shape
    T = ROW_TILE_IN
    nblk = T // MOBA_BLOCK
    const = lambda shape: pl.BlockSpec(shape, lambda b, t: (0,) * len(shape),
                                       pipeline_mode=pl.Buffered(1))
    return pl.pallas_call(
        _inproj_kernel,
        grid=(B, S // T),
        in_specs=[
            pl.BlockSpec((1, T, D), lambda b, t: (b, t, 0)),
            const((1, D)),
            const(wqku.shape),
            const(wvt.shape),
            const(wpool.shape),
            const((1, POOL_WIDTH)),
        ],
        out_specs=[
            pl.BlockSpec((1, T, ATTN_WIDTH), lambda b, t: (b, t, 0)),
            pl.BlockSpec((1, T, ATTN_WIDTH), lambda b, t: (b, t, 0)),
            pl.BlockSpec((1, nblk, ATTN_WIDTH, MOBA_BLOCK), lambda b, t: (b, t, 0, 0)),
            pl.BlockSpec((1, 1, nblk, ATTN_WIDTH), lambda b, t: (b, t, 0, 0)),
            pl.BlockSpec((1, T, POOL_WIDTH), lambda b, t: (b, t, 0)),
        ],
        out_shape=[
            jax.ShapeDtypeStruct((B, S, ATTN_WIDTH), BF16),
            jax.ShapeDtypeStruct((B, S, ATTN_WIDTH), BF16),
            jax.ShapeDtypeStruct((B, S // MOBA_BLOCK, ATTN_WIDTH, MOBA_BLOCK), BF16),
            jax.ShapeDtypeStruct((B, S // T, nblk, ATTN_WIDTH), F32),
            jax.ShapeDtypeStruct((B, S, POOL_WIDTH), BF16),
        ],
        scratch_shapes=[pltpu.VMEM((POOL_HALO, POOL_WIDTH), F32)],
        compiler_params=pltpu.CompilerParams(
            dimension_semantics=("arbitrary", "arbitrary"), vmem_limit_bytes=VMEM_LIMIT),
        name="inproj_pool",
    )(x, g, wqku, wvt, wpool, pscale)


_F_QBLK, _F_JA, _F_JB, _F_A_VALID, _F_B_VALID, _F_FIRST, _F_LAST = range(7)


def _attn_schedule(nb):
    items = []
    for i in range(nb):
        steps = (i + 2) // 2
        for t in range(steps):
            jb = i - 2 * t - 1
            items.append((i, i - 2 * t, max(jb, 0), 1, int(jb >= 0), int(t == 0), int(t == steps - 1)))
    n_items = len(items)
    items += [(nb - 1, 0, 0, 0, 0, 0, 0)] * 2
    return n_items, np.asarray(items, np.int32).T.reshape(-1)


def _attn_kernel(n_items, slopes_ref, sched_ref, q_ref, k_ref, vt_ref, kmean_ref, bias_ref, o_ref,
                 qh_ref, shift_ref, s_ref, mx_ref, p_ref, alpha_ref, m_ref, l_ref, lfin_ref, acc_ref):
    pair = pl.program_id(1)
    nb = kmean_ref.shape[1]
    blk = MOBA_BLOCK
    neg_inf = -jnp.inf
    heads = range(HEADS_PER_STEP)
    stride = n_items + 2

    def field(f, w):
        return sched_ref[f * stride + w]

    def setup_query_block(w):
        i = field(_F_QBLK, w)
        par = i & 1
        qf = q_ref[0, pl.ds(pl.multiple_of(i * blk, blk), blk), :].astype(F32)
        lane = lax.broadcasted_iota(jnp.int32, qf.shape, 1)
        kmean = kmean_ref[0].astype(BF16)
        blk_iota = lax.broadcasted_iota(jnp.int32, (nb, blk), 0)
        dist = ((i - blk_iota) * blk).astype(F32)
        for hh in heads:
            in_head = (lane >= hh * HEAD_DIM) & (lane < (hh + 1) * HEAD_DIM)
            q_h = jnp.where(in_head, qf, 0.0).astype(BF16)
            qh_ref[hh] = q_h
            gate = lax.dot_general(kmean, q_h, NT_DIMS, preferred_element_type=F32)
            gate = jnp.where(blk_iota < i, gate, neg_inf)
            sel = blk_iota == i
            for _ in range(MOBA_TOPK):
                top = jnp.max(gate, axis=0, keepdims=True)
                first = jnp.min(jnp.where(gate == top, blk_iota, nb), axis=0, keepdims=True)
                pick = (blk_iota == first) & (top > neg_inf)
                sel = sel | pick
                gate = jnp.where(pick, neg_inf, gate)
            slope = slopes_ref[pair * HEADS_PER_STEP + hh]
            shift_ref[par, hh] = jnp.where(sel, slope * dist, jnp.inf)

    def score_dots(w):
        ja, jb = field(_F_JA, w), field(_F_JB, w)
        kk = jnp.concatenate(
            [k_ref[0, pl.ds(pl.multiple_of(ja * blk, blk), blk), :],
             k_ref[0, pl.ds(pl.multiple_of(jb * blk, blk), blk), :]], axis=0)
        return [lax.dot_general(kk, qh_ref[hh], NT_DIMS, preferred_element_type=F32) for hh in heads]

    def stage_scores(sts, w, slot):
        first = field(_F_FIRST, w)
        for hh in heads:
            for u, table in ((0, first), (1, 0)):
                s_u = sts[hh][u * blk:(u + 1) * blk] + bias_ref[hh, table]
                s_ref[slot, hh, u] = s_u
                mx_ref[slot, hh, u] = jnp.max(s_u, axis=0, keepdims=True)

    def softmax(w, slot):
        par = field(_F_QBLK, w) & 1
        ja, jb = field(_F_JA, w), field(_F_JB, w)
        a_valid, b_valid = field(_F_A_VALID, w) == 1, field(_F_B_VALID, w) == 1
        first = field(_F_FIRST, w) == 1
        for hh in heads:
            sh_a = jnp.where(a_valid, shift_ref[par, hh, pl.ds(ja, 1), :], jnp.inf)
            sh_b = jnp.where(b_valid, shift_ref[par, hh, pl.ds(jb, 1), :], jnp.inf)
            m_old = jnp.where(first, neg_inf, m_ref[hh])
            m_new = jnp.maximum(m_old, jnp.maximum(mx_ref[slot, hh, 0] - sh_a,
                                                   mx_ref[slot, hh, 1] - sh_b))
            alpha = jnp.exp(m_old - m_new)
            p_a = jnp.exp(s_ref[slot, hh, 0] - (m_new + sh_a))
            p_b = jnp.exp(s_ref[slot, hh, 1] - (m_new + sh_b))
            l_new = (alpha * l_ref[hh] + jnp.sum(p_a, axis=0, keepdims=True)
                     + jnp.sum(p_b, axis=0, keepdims=True))
            l_ref[hh] = l_new
            lfin_ref[slot, hh] = l_new
            m_ref[hh] = m_new
            alpha_ref[slot, hh] = alpha
            p_ref[slot, hh, :blk] = p_a.astype(BF16)
            p_ref[slot, hh, blk:] = p_b.astype(BF16)

    def pv_dots(w, slot):
        ja, jb = field(_F_JA, w), field(_F_JB, w)
        pvs = []
        for hh in heads:
            rows = slice(hh * HEAD_DIM, (hh + 1) * HEAD_DIM)
            vts = jnp.concatenate([vt_ref[0, ja, rows, :], vt_ref[0, jb, rows, :]], axis=1)
            pvs.append(jnp.dot(vts, p_ref[slot, hh], preferred_element_type=F32))
        return pvs

    def accumulate(pvs, slot):
        for hh in heads:
            acc_ref[hh] = alpha_ref[slot, hh] * acc_ref[hh] + pvs[hh]

    def finalize(w, slot):
        i = field(_F_QBLK, w)
        o_t = jnp.concatenate([acc_ref[hh] / lfin_ref[slot, hh] for hh in heads], axis=0)
        o_ref[0, pl.ds(pl.multiple_of(i * blk, blk), blk), :] = o_t.T.astype(o_ref.dtype)

    l_ref[...] = jnp.zeros_like(l_ref)
    acc_ref[...] = jnp.zeros_like(acc_ref)
    m_ref[...] = jnp.zeros_like(m_ref)

    setup_query_block(0)
    stage_scores(score_dots(0), 0, 0)
    setup_query_block(1)
    stage_scores(score_dots(1), 1, 1)
    softmax(0, 0)

    def item(w, slot):
        pl.when(field(_F_FIRST, w + 2) == 1)(functools.partial(setup_query_block, w + 2))
        sts = score_dots(w + 2)
        pvs = pv_dots(w, slot)
        softmax(w + 1, 1 - slot)
        stage_scores(sts, w + 2, slot)
        accumulate(pvs, slot)
        pl.when(field(_F_LAST, w) == 1)(functools.partial(finalize, w, slot))

    def body(w2, carry):
        item(2 * w2, 0)
        item(2 * w2 + 1, 1)
        return carry

    assert n_items % 2 == 0
    lax.fori_loop(0, n_items // 2, body, 0)


def _attn_call(slopes, bias, q, k, vt, kmean):
    B, S, W = q.shape
    nb = S // MOBA_BLOCK
    n_pairs = W // LANES
    hps = HEADS_PER_STEP
    n_items, sched = _attn_schedule(nb)
    grid_spec = pltpu.PrefetchScalarGridSpec(
        num_scalar_prefetch=2,
        grid=(B, n_pairs),
        in_specs=[
            pl.BlockSpec((1, S, LANES), lambda b, p, *_: (b, 0, p)),
            pl.BlockSpec((1, S, LANES), lambda b, p, *_: (b, 0, p)),
            pl.BlockSpec((1, nb, LANES, MOBA_BLOCK), lambda b, p, *_: (b, 0, p, 0)),
            pl.BlockSpec((1, nb, LANES), lambda b, p, *_: (b, 0, p)),
            pl.BlockSpec((hps, 2, MOBA_BLOCK, MOBA_BLOCK), lambda b, p, *_: (p, 0, 0, 0)),
        ],
        out_specs=pl.BlockSpec((1, S, LANES), lambda b, p, *_: (b, 0, p)),
        scratch_shapes=[
            pltpu.VMEM((hps, MOBA_BLOCK, LANES), BF16),
            pltpu.VMEM((2, hps, nb, MOBA_BLOCK), F32),
            pltpu.VMEM((2, hps, 2, MOBA_BLOCK, MOBA_BLOCK), F32),
            pltpu.VMEM((2, hps, 2, 1, MOBA_BLOCK), F32),
            pltpu.VMEM((2, hps, 2 * MOBA_BLOCK, MOBA_BLOCK), BF16),
            pltpu.VMEM((2, hps, 1, MOBA_BLOCK), F32),
            pltpu.VMEM((hps, 1, MOBA_BLOCK), F32),
            pltpu.VMEM((hps, 1, MOBA_BLOCK), F32),
            pltpu.VMEM((2, hps, 1, MOBA_BLOCK), F32),
            pltpu.VMEM((hps, HEAD_DIM, MOBA_BLOCK), F32),
        ],
    )
    return pl.pallas_call(
        functools.partial(_attn_kernel, n_items),
        grid_spec=grid_spec,
        out_shape=jax.ShapeDtypeStruct((B, S, W), BF16),
        compiler_params=pltpu.CompilerParams(
            dimension_semantics=("arbitrary", "arbitrary"), vmem_limit_bytes=VMEM_LIMIT),
        name="moba_attn",
    )(slopes, jnp.asarray(sched), q, k, vt, kmean, bias)


def _ffn_kernel(x_ref, a_ref, p_ref, woa_ref, wop_ref, g2_ref, wup_ref, wdn_ref, g3_ref, o_ref):
    x1 = (x_ref[...]
          + jnp.dot(a_ref[...], woa_ref[...], preferred_element_type=F32)
          + jnp.dot(p_ref[...], wop_ref[...], preferred_element_type=F32))
    h = _rms_norm(x1, g2_ref[...]).astype(BF16)
    o_ref[...] = x1
    for c in range(wup_ref.shape[1] // FF_CHUNK):
        cols = slice(c * FF_CHUNK, (c + 1) * FF_CHUNK)
        up = jnp.dot(h, wup_ref[:, cols], preferred_element_type=F32)
        act = jnp.square(jnp.maximum(up, 0.0)).astype(BF16)
        o_ref[...] += jnp.dot(act, wdn_ref[cols, :], preferred_element_type=F32)
    o_ref[...] = _rms_norm(o_ref[...], g3_ref[...])


def _ffn_call(x, a, p, woa, wop, g2, wup, wdn, g3):
    N, D = x.shape
    T = ROW_TILE_FFN
    const = lambda shape: pl.BlockSpec(shape, lambda t: (0,) * len(shape),
                                       pipeline_mode=pl.Buffered(1))
    return pl.pallas_call(
        _ffn_kernel,
        grid=(N // T,),
        in_specs=[
            pl.BlockSpec((T, D), lambda t: (t, 0)),
            pl.BlockSpec((T, ATTN_WIDTH), lambda t: (t, 0)),
            pl.BlockSpec((T, POOL_WIDTH), lambda t: (t, 0)),
            const(woa.shape), const(wop.shape), const((1, D)),
            const(wup.shape), const(wdn.shape), const((1, D)),
        ],
        out_specs=pl.BlockSpec((T, D), lambda t: (t, 0)),
        out_shape=jax.ShapeDtypeStruct((N, D), F32),
        compiler_params=pltpu.CompilerParams(
            dimension_semantics=("arbitrary",), vmem_limit_bytes=VMEM_LIMIT),
        name="outproj_ffn",
    )(x, a, p, woa, wop, g2, wup, wdn, g3)


def _alibi_constants():
    slopes = jnp.asarray(2.0 ** (-8.0 * np.arange(1, N_HEADS + 1) / N_HEADS), dtype=F32)
    key = lax.broadcasted_iota(jnp.int32, (MOBA_BLOCK, MOBA_BLOCK), 0)
    qry = lax.broadcasted_iota(jnp.int32, (MOBA_BLOCK, MOBA_BLOCK), 1)
    past = -slopes[:, None, None] * (qry - key).astype(F32)[None]
    own = jnp.where((key <= qry)[None], past, -jnp.inf)
    return slopes, jnp.stack([past, own], axis=1)


def kernel(x, norm_mix, w_in, w_pool, pool_scale, w_out, norm_mlp, w_up, w_down, norm_final):
    B, S, D = x.shape
    assert w_in.shape[0] == 1, "single trunk layer (the final norm is fused into the FFN call)"
    assert S % ROW_TILE_IN == 0 and (B * S) % ROW_TILE_FFN == 0 and S % MOBA_BLOCK == 0
    assert w_in.shape[2] == 3 * ATTN_WIDTH + POOL_WIDTH and D == ATTN_WIDTH + POOL_WIDTH
    wi = w_in[0].astype(BF16)
    wqku = jnp.concatenate([wi[:, :2 * ATTN_WIDTH], wi[:, 3 * ATTN_WIDTH:]], axis=1)
    wvt = wi[:, 2 * ATTN_WIDTH:3 * ATTN_WIDTH].T
    q, k, vt, kmean, p = _inproj_call(
        x, norm_mix[0][None, :], wqku, wvt, w_pool[0].astype(BF16), pool_scale[0][None, :])
    kmean = kmean.reshape(B, S // MOBA_BLOCK, ATTN_WIDTH)
    slopes, bias = _alibi_constants()
    a = _attn_call(slopes, bias, q, k, vt, kmean)
    wo = w_out[0].astype(BF16)
    y = _ffn_call(x.reshape(B * S, D), a.reshape(B * S, ATTN_WIDTH),
                  p.reshape(B * S, POOL_WIDTH), wo[:ATTN_WIDTH], wo[ATTN_WIDTH:],
                  norm_mlp[0][None, :], w_up[0].astype(BF16), w_down[0].astype(BF16),
                  norm_final[None, :])
    return y.reshape(B, S, D)
```

```python
import functools

import jax
import jax.numpy as jnp
import numpy as np
from jax import lax
from jax.experimental import pallas as pl
from jax.experimental.pallas import tpu as pltpu

F32 = jnp.float32
BF16 = jnp.bfloat16

N_HEADS = 8
HEAD_DIM = 64
ATTN_WIDTH = N_HEADS * HEAD_DIM
POOL_WINDOWS = (2, 4, 8, 16)
POOL_GROUP = 128
POOL_WIDTH = POOL_GROUP * len(POOL_WINDOWS)
MOBA_BLOCK = 256
MOBA_TOPK = 3
EPS = 1e-6
LOG2E = 1.4426950408889634
QK_SCALE = HEAD_DIM ** -0.5 * LOG2E
SUM_ROWS = 16
POOL_HALO = 16
HEADS_PER_STEP = 2
LANES = 128

ROW_TILE_IN = 512
ROW_TILE_FFN = 512
FF_CHUNK = 1024
VMEM_LIMIT = 56 * 1024 * 1024

NT_DIMS = (((1,), (1,)), ((), ()))


def _rms_norm(x, g):
    ms = jnp.mean(x * x, axis=-1, keepdims=True)
    return x * lax.rsqrt(ms + EPS) * g


def _inproj_kernel(x_ref, g_ref, wqku_ref, wvt_ref, wpool_ref, pscale_ref,
                   q_ref, k_ref, vt_ref, kmean_ref, p_ref, halo_ref):
    t = pl.program_id(1)
    tile = x_ref.shape[1]
    nblk = tile // MOBA_BLOCK

    h = _rms_norm(x_ref[0], g_ref[...]).astype(BF16)
    qku = jnp.dot(h, wqku_ref[...], preferred_element_type=F32)
    q_ref[0] = (qku[:, :ATTN_WIDTH] * QK_SCALE).astype(BF16)
    kf = qku[:, ATTN_WIDTH:2 * ATTN_WIDTH]
    k_ref[0] = kf.astype(BF16)
    for b in range(nblk):
        kmean_ref[0, 0, b:b + 1, :] = (
            jnp.sum(kf[b * MOBA_BLOCK:(b + 1) * MOBA_BLOCK], axis=0, keepdims=True)
            * (1.0 / MOBA_BLOCK))

    vt = lax.dot_general(wvt_ref[...], h, NT_DIMS, preferred_element_type=F32)
    for b in range(nblk):
        vt_ref[0, b] = vt[:, b * MOBA_BLOCK:(b + 1) * MOBA_BLOCK].astype(BF16)

    u = qku[:, 2 * ATTN_WIDTH:]

    @pl.when(t == 0)
    def _():
        halo_ref[...] = jnp.zeros_like(halo_ref)

    ext = jnp.concatenate([halo_ref[...], u], axis=0)
    halo_ref[...] = u[tile - POOL_HALO:, :]
    pos = t * tile + lax.broadcasted_iota(jnp.int32, (tile, POOL_GROUP), 0)
    for g, w in enumerate(POOL_WINDOWS):
        cols = slice(g * POOL_GROUP, (g + 1) * POOL_GROUP)
        s = ext[:, cols]
        shift = 1
        while shift < w:
            s = s + pltpu.roll(s, shift, axis=0)
            shift *= 2
        cnt = jnp.minimum(pos + 1, w).astype(F32)
        mixed = s[POOL_HALO:, :] / cnt - u[:, cols]
        y = jnp.dot(mixed.astype(BF16), wpool_ref[g], preferred_element_type=F32)
        p_ref[0, :, cols] = (y * pscale_ref[:, cols]).astype(BF16)


def _inproj_call(x, g, wqku, wvt, wpool, pscale):
    B, S, D = x.shape
    T = ROW_TILE_IN
    nblk = T // MOBA_BLOCK
    const = lambda shape: pl.BlockSpec(shape, lambda b, t: (0,) * len(shape),
                                       pipeline_mode=pl.Buffered(1))
    return pl.pallas_call(
        _inproj_kernel,
        grid=(B, S // T),
        in_specs=[
            pl.BlockSpec((1, T, D), lambda b, t: (b, t, 0)),
            const((1, D)),
            const(wqku.shape),
            const(wvt.shape),
            const(wpool.shape),
            const((1, POOL_WIDTH)),
        ],
        out_specs=[
            pl.BlockSpec((1, T, ATTN_WIDTH), lambda b, t: (b, t, 0)),
            pl.BlockSpec((1, T, ATTN_WIDTH), lambda b, t: (b, t, 0)),
            pl.BlockSpec((1, nblk, ATTN_WIDTH, MOBA_BLOCK), lambda b, t: (b, t, 0, 0)),
            pl.BlockSpec((1, 1, nblk, ATTN_WIDTH), lambda b, t: (b, t, 0, 0)),
            pl.BlockSpec((1, T, POOL_WIDTH), lambda b, t: (b, t, 0)),
        ],
        out_shape=[
            jax.ShapeDtypeStruct((B, S, ATTN_WIDTH), BF16),
            jax.ShapeDtypeStruct((B, S, ATTN_WIDTH), BF16),
            jax.ShapeDtypeStruct((B, S // MOBA_BLOCK, ATTN_WIDTH, MOBA_BLOCK), BF16),
            jax.ShapeDtypeStruct((B, S // T, nblk, ATTN_WIDTH), F32),
            jax.ShapeDtypeStruct((B, S, POOL_WIDTH), BF16),
        ],
        scratch_shapes=[pltpu.VMEM((POOL_HALO, POOL_WIDTH), F32)],
        compiler_params=pltpu.CompilerParams(
            dimension_semantics=("arbitrary", "arbitrary"), vmem_limit_bytes=VMEM_LIMIT),
        name="inproj_pool",
    )(x, g, wqku, wvt, wpool, pscale)


_F_QBLK, _F_JA, _F_JB, _F_A_VALID, _F_B_VALID, _F_FIRST, _F_LAST = range(7)


def _attn_schedule(nb):
    items = []
    for i in range(nb):
        steps = (i + 2) // 2
        for t in range(steps):
            jb = i - 2 * t - 1
            items.append((i, i - 2 * t, max(jb, 0), 1, int(jb >= 0), int(t == 0), int(t == steps - 1)))
    n_items = len(items)
    items += [(nb - 1, 0, 0, 0, 0, 0, 0)] * 2
    return n_items, np.asarray(items, np.int32).T.reshape(-1)


def _attn_kernel(n_items, slopes_ref, sched_ref, q_ref, k_ref, vt_ref, kmean_ref, bias_ref, o_ref,
                 qh_ref, shift_ref, s_ref, mx_ref, p_ref, alpha_ref, m_ref, acc_ref):
    pair = pl.program_id(1)
    nb = kmean_ref.shape[1]
    seq = q_ref.shape[1]
    blk = MOBA_BLOCK
    neg_inf = -jnp.inf
    heads = range(HEADS_PER_STEP)
    stride = n_items + 2

    def field(f, w):
        return sched_ref[f * stride + w]

    lane = lax.broadcasted_iota(jnp.int32, (1, LANES), 1)
    kmean = kmean_ref[0].astype(BF16)
    key_blk = lax.broadcasted_iota(jnp.int32, (nb, seq), 0)
    qry_blk = lax.broadcasted_iota(jnp.int32, (nb, seq), 1) // blk
    dist = ((qry_blk - key_blk) * blk).astype(F32)
    for hh in heads:
        in_head = ((lane >= hh * HEAD_DIM) & (lane < (hh + 1) * HEAD_DIM)).astype(BF16)
        q_h = q_ref[0] * in_head
        qh_ref[hh] = q_h
        gate = lax.dot_general(kmean, q_h, NT_DIMS, preferred_element_type=F32)
        gate = jnp.where(key_blk < qry_blk, gate, neg_inf)
        sel = key_blk == qry_blk
        for _ in range(MOBA_TOPK):
            top = jnp.max(gate, axis=0, keepdims=True)
            first = jnp.min(jnp.where(gate == top, key_blk, nb), axis=0, keepdims=True)
            pick = (key_blk == first) & (top > neg_inf)
            sel = sel | pick
            gate = jnp.where(pick, neg_inf, gate)
        slope = slopes_ref[pair * HEADS_PER_STEP + hh]
        shift = jnp.where(sel, slope * dist, jnp.inf)
        for i in range(nb):
            shift_ref[hh, i] = shift[:, i * blk:(i + 1) * blk]

    def score_dot(w, hh):
        i, ja, jb = field(_F_QBLK, w), field(_F_JA, w), field(_F_JB, w)
        kk = jnp.concatenate(
            [k_ref[0, pl.ds(pl.multiple_of(ja * blk, blk), blk), :],
             k_ref[0, pl.ds(pl.multiple_of(jb * blk, blk), blk), :]], axis=0)
        q_h = qh_ref[hh, pl.ds(pl.multiple_of(i * blk, blk), blk), :]
        return lax.dot_general(kk, q_h, NT_DIMS, preferred_element_type=F32)

    def stage_scores(sts, w, slot):
        first = field(_F_FIRST, w)
        for hh in heads:
            for u, table in ((0, first), (1, 0)):
                s_u = sts[hh][u * blk:(u + 1) * blk] + bias_ref[hh, table]
                s_ref[slot, hh, u] = s_u
                mx_ref[slot, hh, u] = jnp.max(s_u, axis=0, keepdims=True)

    def softmax_prep(w, slot, hh):
        i, ja, jb = field(_F_QBLK, w), field(_F_JA, w), field(_F_JB, w)
        a_valid, b_valid = field(_F_A_VALID, w) == 1, field(_F_B_VALID, w) == 1
        first = field(_F_FIRST, w) == 1
        sh_a = jnp.where(a_valid, shift_ref[hh, i, pl.ds(ja, 1), :], jnp.inf)
        sh_b = jnp.where(b_valid, shift_ref[hh, i, pl.ds(jb, 1), :], jnp.inf)
        m_old = jnp.where(first, neg_inf, m_ref[hh])
        m_new = jnp.maximum(m_old, jnp.maximum(mx_ref[slot, hh, 0] - sh_a,
                                               mx_ref[slot, hh, 1] - sh_b))
        m_ref[hh] = m_new
        alpha_ref[slot, hh] = jnp.exp2(m_old - m_new)
        return m_new + sh_a, m_new + sh_b

    def softmax_block(slot, hh, u, offset):
        p_ref[slot, hh, u * blk:(u + 1) * blk] = jnp.exp2(s_ref[slot, hh, u] - offset).astype(BF16)

    def softmax(w, slot):
        for hh in heads:
            offsets = softmax_prep(w, slot, hh)
            for u in range(2):
                softmax_block(slot, hh, u, offsets[u])

    ones_rows = jnp.ones((SUM_ROWS, 2 * blk), BF16)

    def pv_dot(w, slot, hh):
        ja, jb = field(_F_JA, w), field(_F_JB, w)
        rows = slice(hh * HEAD_DIM, (hh + 1) * HEAD_DIM)
        vts = jnp.concatenate([vt_ref[0, ja, rows, :], vt_ref[0, jb, rows, :]], axis=1)
        lhs = jnp.concatenate([vts, ones_rows], axis=0)
        return jnp.dot(lhs, p_ref[slot, hh], preferred_element_type=F32)

    def accumulate(pvs, slot):
        for hh in heads:
            acc_ref[hh] = alpha_ref[slot, hh] * acc_ref[hh] + pvs[hh]

    def finalize(w):
        i = field(_F_QBLK, w)
        o_t = jnp.concatenate([acc_ref[hh, :HEAD_DIM] / acc_ref[hh, HEAD_DIM:HEAD_DIM + 1]
                               for hh in heads], axis=0)
        o_ref[0, pl.ds(pl.multiple_of(i * blk, blk), blk), :] = o_t.T.astype(o_ref.dtype)

    acc_ref[...] = jnp.zeros_like(acc_ref)
    m_ref[...] = jnp.zeros_like(m_ref)

    stage_scores([score_dot(0, hh) for hh in heads], 0, 0)
    stage_scores([score_dot(1, hh) for hh in heads], 1, 1)
    softmax(0, 0)

    def item(w, slot):
        sm = 1 - slot
        off_a, off_b = [softmax_prep(w + 1, sm, hh) for hh in heads]
        softmax_block(sm, 0, 0, off_a[0])
        sts_a = score_dot(w + 2, 0)
        softmax_block(sm, 0, 1, off_a[1])
        sts_b = score_dot(w + 2, 1)
        softmax_block(sm, 1, 0, off_b[0])
        softmax_block(sm, 1, 1, off_b[1])
        pv_a = pv_dot(w, slot, 0)
        pv_b = pv_dot(w, slot, 1)
        stage_scores([sts_a, sts_b], w + 2, slot)
        accumulate([pv_a, pv_b], slot)
        pl.when(field(_F_LAST, w) == 1)(functools.partial(finalize, w))

    def body(w2, carry):
        item(2 * w2, 0)
        item(2 * w2 + 1, 1)
        return carry

    assert n_items % 2 == 0
    lax.fori_loop(0, n_items // 2, body, 0)


def _attn_call(slopes, bias, q, k, vt, kmean):
    B, S, W = q.shape
    nb = S // MOBA_BLOCK
    n_pairs = W // LANES
    hps = HEADS_PER_STEP
    n_items, sched = _attn_schedule(nb)
    grid_spec = pltpu.PrefetchScalarGridSpec(
        num_scalar_prefetch=2,
        grid=(B, n_pairs),
        in_specs=[
            pl.BlockSpec((1, S, LANES), lambda b, p, *_: (b, 0, p)),
            pl.BlockSpec((1, S, LANES), lambda b, p, *_: (b, 0, p)),
            pl.BlockSpec((1, nb, LANES, MOBA_BLOCK), lambda b, p, *_: (b, 0, p, 0)),
            pl.BlockSpec((1, nb, LANES), lambda b, p, *_: (b, 0, p)),
            pl.BlockSpec((hps, 2, MOBA_BLOCK, MOBA_BLOCK), lambda b, p, *_: (p, 0, 0, 0)),
        ],
        out_specs=pl.BlockSpec((1, S, LANES), lambda b, p, *_: (b, 0, p)),
        scratch_shapes=[
            pltpu.VMEM((hps, S, LANES), BF16),
            pltpu.VMEM((hps, nb, nb, MOBA_BLOCK), F32),
            pltpu.VMEM((2, hps, 2, MOBA_BLOCK, MOBA_BLOCK), F32),
            pltpu.VMEM((2, hps, 2, 1, MOBA_BLOCK), F32),
            pltpu.VMEM((2, hps, 2 * MOBA_BLOCK, MOBA_BLOCK), BF16),
            pltpu.VMEM((2, hps, 1, MOBA_BLOCK), F32),
            pltpu.VMEM((hps, 1, MOBA_BLOCK), F32),
            pltpu.VMEM((hps, HEAD_DIM + SUM_ROWS, MOBA_BLOCK), F32),
        ],
    )
    return pl.pallas_call(
        functools.partial(_attn_kernel, n_items),
        grid_spec=grid_spec,
        out_shape=jax.ShapeDtypeStruct((B, S, W), BF16),
        compiler_params=pltpu.CompilerParams(
            dimension_semantics=("arbitrary", "arbitrary"), vmem_limit_bytes=VMEM_LIMIT),
        name="moba_attn",
    )(slopes, jnp.asarray(sched), q, k, vt, kmean, bias)


def _ffn_kernel(x_ref, a_ref, p_ref, woa_ref, wop_ref, g2_ref, wup_ref, wdn_ref, g3_ref, o_ref):
    x1 = (x_ref[...]
          + jnp.dot(a_ref[...], woa_ref[...], preferred_element_type=F32)
          + jnp.dot(p_ref[...], wop_ref[...], preferred_element_type=F32))
    h = _rms_norm(x1, g2_ref[...]).astype(BF16)
    o_ref[...] = x1
    for c in range(wup_ref.shape[1] // FF_CHUNK):
        cols = slice(c * FF_CHUNK, (c + 1) * FF_CHUNK)
        up = jnp.dot(h, wup_ref[:, cols], preferred_element_type=F32)
        act = jnp.square(jnp.maximum(up, 0.0)).astype(BF16)
        o_ref[...] += jnp.dot(act, wdn_ref[cols, :], preferred_element_type=F32)
    o_ref[...] = _rms_norm(o_ref[...], g3_ref[...])


def _ffn_call(x, a, p, woa, wop, g2, wup, wdn, g3):
    N, D = x.shape
    T = ROW_TILE_FFN
    const = lambda shape: pl.BlockSpec(shape, lambda t: (0,) * len(shape),
                                       pipeline_mode=pl.Buffered(1))
    return pl.pallas_call(
        _ffn_kernel,
        grid=(N // T,),
        in_specs=[
            pl.BlockSpec((T, D), lambda t: (t, 0)),
            pl.BlockSpec((T, ATTN_WIDTH), lambda t: (t, 0)),
            pl.BlockSpec((T, POOL_WIDTH), lambda t: (t, 0)),
            const(woa.shape), const(wop.shape), const((1, D)),
            const(wup.shape), const(wdn.shape), const((1, D)),
        ],
        out_specs=pl.BlockSpec((T, D), lambda t: (t, 0)),
        out_shape=jax.ShapeDtypeStruct((N, D), F32),
        compiler_params=pltpu.CompilerParams(
            dimension_semantics=("arbitrary",), vmem_limit_bytes=VMEM_LIMIT),
        name="outproj_ffn",
    )(x, a, p, woa, wop, g2, wup, wdn, g3)


def _alibi_constants():
    slopes = jnp.asarray(2.0 ** (-8.0 * np.arange(1, N_HEADS + 1) / N_HEADS) * LOG2E, dtype=F32)
    key = lax.broadcasted_iota(jnp.int32, (MOBA_BLOCK, MOBA_BLOCK), 0)
    qry = lax.broadcasted_iota(jnp.int32, (MOBA_BLOCK, MOBA_BLOCK), 1)
    past = -slopes[:, None, None] * (qry - key).astype(F32)[None]
    own = jnp.where((key <= qry)[None], past, -jnp.inf)
    return slopes, jnp.stack([past, own], axis=1)


def kernel(x, norm_mix, w_in, w_pool, pool_scale, w_out, norm_mlp, w_up, w_down, norm_final):
    B, S, D = x.shape
    assert w_in.shape[0] == 1, "single trunk layer (the final norm is fused into the FFN call)"
    assert S % ROW_TILE_IN == 0 and (B * S) % ROW_TILE_FFN == 0 and S % MOBA_BLOCK == 0
    assert w_in.shape[2] == 3 * ATTN_WIDTH + POOL_WIDTH and D == ATTN_WIDTH + POOL_WIDTH
    wi = w_in[0].astype(BF16)
    wqku = jnp.concatenate([wi[:, :2 * ATTN_WIDTH], wi[:, 3 * ATTN_WIDTH:]], axis=1)
    wvt = wi[:, 2 * ATTN_WIDTH:3 * ATTN_WIDTH].T
    q, k, vt, kmean, p = _inproj_call(
        x, norm_mix[0][None, :], wqku, wvt, w_pool[0].astype(BF16), pool_scale[0][None, :])
    kmean = kmean.reshape(B, S // MOBA_BLOCK, ATTN_WIDTH)
    slopes, bias = _alibi_constants()
    a = _attn_call(slopes, bias, q, k, vt, kmean)
    wo = w_out[0].astype(BF16)
    y = _ffn_call(x.reshape(B * S, D), a.reshape(B * S, ATTN_WIDTH),
                  p.reshape(B * S, POOL_WIDTH), wo[:ATTN_WIDTH], wo[ATTN_WIDTH:],
                  norm_mlp[0][None, :], w_up[0].astype(BF16), w_down[0].astype(BF16),
                  norm_final[None, :])
    return y.reshape(B, S, D)
```

```python
import functools

import jax
import jax.numpy as jnp
import numpy as np
from jax import lax
from jax.experimental import pallas as pl
from jax.experimental.pallas import tpu as pltpu

F32 = jnp.float32
BF16 = jnp.bfloat16

N_HEADS = 8
HEAD_DIM = 64
ATTN_WIDTH = N_HEADS * HEAD_DIM
POOL_WINDOWS = (2, 4, 8, 16)
POOL_GROUP = 128
POOL_WIDTH = POOL_GROUP * len(POOL_WINDOWS)
MOBA_BLOCK = 256
MOBA_TOPK = 3
EPS = 1e-6
LOG2E = 1.4426950408889634
QK_SCALE = HEAD_DIM ** -0.5 * LOG2E
SUM_ROWS = 16
POOL_HALO = 16
LANES = 128
HEADS_PER_GROUP = LANES // HEAD_DIM
GROUPS_PER_STEP = 2
HEADS_PER_STEP = GROUPS_PER_STEP * HEADS_PER_GROUP

ROW_TILE_IN = 512
ROW_TILE_FFN = 512
FF_CHUNK = 1024
VMEM_LIMIT = 56 * 1024 * 1024

NT_DIMS = (((1,), (1,)), ((), ()))


def _rms_norm(x, g):
    ms = jnp.mean(x * x, axis=-1, keepdims=True)
    return x * lax.rsqrt(ms + EPS) * g


def _inproj_kernel(x_ref, g_ref, wqku_ref, wvt_ref, wpool_ref, pscale_ref,
                   q_ref, k_ref, vt_ref, kmean_ref, p_ref, halo_ref):
    t = pl.program_id(1)
    tile = x_ref.shape[1]
    nblk = tile // MOBA_BLOCK

    h = _rms_norm(x_ref[0], g_ref[...]).astype(BF16)
    qku = jnp.dot(h, wqku_ref[...], preferred_element_type=F32)
    q_ref[0] = (qku[:, :ATTN_WIDTH] * QK_SCALE).astype(BF16)
    kf = qku[:, ATTN_WIDTH:2 * ATTN_WIDTH]
    k_ref[0] = kf.astype(BF16)
    for b in range(nblk):
        kmean_ref[0, 0, b:b + 1, :] = (
            jnp.sum(kf[b * MOBA_BLOCK:(b + 1) * MOBA_BLOCK], axis=0, keepdims=True)
            * (1.0 / MOBA_BLOCK))

    vt = lax.dot_general(wvt_ref[...], h, NT_DIMS, preferred_element_type=F32)
    for b in range(nblk):
        vt_ref[0, b] = vt[:, b * MOBA_BLOCK:(b + 1) * MOBA_BLOCK].astype(BF16)

    u = qku[:, 2 * ATTN_WIDTH:]

    @pl.when(t == 0)
    def _():
        halo_ref[...] = jnp.zeros_like(halo_ref)

    ext = jnp.concatenate([halo_ref[...], u], axis=0)
    halo_ref[...] = u[tile - POOL_HALO:, :]
    pos = t * tile + lax.broadcasted_iota(jnp.int32, (tile, POOL_GROUP), 0)
    for g, w in enumerate(POOL_WINDOWS):
        cols = slice(g * POOL_GROUP, (g + 1) * POOL_GROUP)
        s = ext[:, cols]
        shift = 1
        while shift < w:
            s = s + pltpu.roll(s, shift, axis=0)
            shift *= 2
        cnt = jnp.minimum(pos + 1, w).astype(F32)
        mixed = s[POOL_HALO:, :] / cnt - u[:, cols]
        y = jnp.dot(mixed.astype(BF16), wpool_ref[g], preferred_element_type=F32)
        p_ref[0, :, cols] = (y * pscale_ref[:, cols]).astype(BF16)


def _inproj_call(x, g, wqku, wvt, wpool, pscale):
    B, S, D = x.shape
    T = ROW_TILE_IN
    nblk = T // MOBA_BLOCK
    const = lambda shape: pl.BlockSpec(shape, lambda b, t: (0,) * len(shape),
                                       pipeline_mode=pl.Buffered(1))
    return pl.pallas_call(
        _inproj_kernel,
        grid=(B, S // T),
        in_specs=[
            pl.BlockSpec((1, T, D), lambda b, t: (b, t, 0)),
            const((1, D)),
            const(wqku.shape),
            const(wvt.shape),
            const(wpool.shape),
            const((1, POOL_WIDTH)),
        ],
        out_specs=[
            pl.BlockSpec((1, T, ATTN_WIDTH), lambda b, t: (b, t, 0)),
            pl.BlockSpec((1, T, ATTN_WIDTH), lambda b, t: (b, t, 0)),
            pl.BlockSpec((1, nblk, ATTN_WIDTH, MOBA_BLOCK), lambda b, t: (b, t, 0, 0)),
            pl.BlockSpec((1, 1, nblk, ATTN_WIDTH), lambda b, t: (b, t, 0, 0)),
            pl.BlockSpec((1, T, POOL_WIDTH), lambda b, t: (b, t, 0)),
        ],
        out_shape=[
            jax.ShapeDtypeStruct((B, S, ATTN_WIDTH), BF16),
            jax.ShapeDtypeStruct((B, S, ATTN_WIDTH), BF16),
            jax.ShapeDtypeStruct((B, S // MOBA_BLOCK, ATTN_WIDTH, MOBA_BLOCK), BF16),
            jax.ShapeDtypeStruct((B, S // T, nblk, ATTN_WIDTH), F32),
            jax.ShapeDtypeStruct((B, S, POOL_WIDTH), BF16),
        ],
        scratch_shapes=[pltpu.VMEM((POOL_HALO, POOL_WIDTH), F32)],
        compiler_params=pltpu.CompilerParams(
            dimension_semantics=("arbitrary", "arbitrary"), vmem_limit_bytes=VMEM_LIMIT),
        name="inproj_pool",
    )(x, g, wqku, wvt, wpool, pscale)


_F_QBLK, _F_JA, _F_JB, _F_A_VALID, _F_B_VALID, _F_FIRST, _F_LAST = range(7)


def _attn_schedule(nb):
    items = []
    for i in range(nb):
        steps = (i + 2) // 2
        for t in range(steps):
            jb = i - 2 * t - 1
            items.append((i, i - 2 * t, max(jb, 0), 1, int(jb >= 0), int(t == 0), int(t == steps - 1)))
    n_items = len(items)
    items += [(nb - 1, 0, 0, 0, 0, 0, 0)] * 2
    return n_items, np.asarray(items, np.int32).T.reshape(-1)


def _attn_kernel(n_items, slopes_ref, sched_ref, q_ref, k_ref, vt_ref, kmean_ref, bias_ref, o_ref,
                 qh_ref, shift_ref, s_ref, mx_ref, p_ref, alpha_ref, m_ref, acc_ref):
    step = pl.program_id(1)
    nb = kmean_ref.shape[1]
    seq = q_ref.shape[1]
    blk = MOBA_BLOCK
    neg_inf = -jnp.inf
    heads = range(HEADS_PER_STEP)
    groups = [tuple(range(g * HEADS_PER_GROUP, (g + 1) * HEADS_PER_GROUP)) for g in range(GROUPS_PER_STEP)]
    stride = n_items + 2

    def group_lanes(hh):
        g = hh // HEADS_PER_GROUP
        return slice(g * LANES, (g + 1) * LANES)

    def field(f, w):
        return sched_ref[f * stride + w]

    lane = lax.broadcasted_iota(jnp.int32, (1, LANES), 1)
    kmean = kmean_ref[0].astype(BF16)
    key_blk = lax.broadcasted_iota(jnp.int32, (nb, seq), 0)
    qry_blk = lax.broadcasted_iota(jnp.int32, (nb, seq), 1) // blk
    dist = ((qry_blk - key_blk) * blk).astype(F32)
    for hh in heads:
        lo = (hh % HEADS_PER_GROUP) * HEAD_DIM
        in_head = ((lane >= lo) & (lane < lo + HEAD_DIM)).astype(BF16)
        q_h = q_ref[0, :, group_lanes(hh)] * in_head
        qh_ref[hh] = q_h
        gate = lax.dot_general(kmean[:, group_lanes(hh)], q_h, NT_DIMS,
                               preferred_element_type=F32)
        gate = jnp.where(key_blk < qry_blk, gate, neg_inf)
        sel = key_blk == qry_blk
        for _ in range(MOBA_TOPK):
            top = jnp.max(gate, axis=0, keepdims=True)
            first = jnp.min(jnp.where(gate == top, key_blk, nb), axis=0, keepdims=True)
            pick = (key_blk == first) & (top > neg_inf)
            sel = sel | pick
            gate = jnp.where(pick, neg_inf, gate)
        slope = slopes_ref[step * HEADS_PER_STEP + hh]
        shift = jnp.where(sel, slope * dist, jnp.inf)
        for i in range(nb):
            shift_ref[hh, i] = shift[:, i * blk:(i + 1) * blk]

    def score_dot(w, hh):
        i, ja, jb = field(_F_QBLK, w), field(_F_JA, w), field(_F_JB, w)
        kk = jnp.concatenate(
            [k_ref[0, pl.ds(pl.multiple_of(ja * blk, blk), blk), group_lanes(hh)],
             k_ref[0, pl.ds(pl.multiple_of(jb * blk, blk), blk), group_lanes(hh)]], axis=0)
        q_h = qh_ref[hh, pl.ds(pl.multiple_of(i * blk, blk), blk), :]
        return lax.dot_general(kk, q_h, NT_DIMS, preferred_element_type=F32)

    def stage_scores(sts, w, slot, hhs):
        first = field(_F_FIRST, w)
        for hh in hhs:
            for u, table in ((0, first), (1, 0)):
                s_u = sts[hh][u * blk:(u + 1) * blk] + bias_ref[hh, table]
                s_ref[slot, hh, u] = s_u
                mx_ref[slot, hh, u] = jnp.max(s_u, axis=0, keepdims=True)

    def softmax_prep(w, slot, hh):
        i, ja, jb = field(_F_QBLK, w), field(_F_JA, w), field(_F_JB, w)
        a_valid, b_valid = field(_F_A_VALID, w) == 1, field(_F_B_VALID, w) == 1
        first = field(_F_FIRST, w) == 1
        sh_a = jnp.where(a_valid, shift_ref[hh, i, pl.ds(ja, 1), :], jnp.inf)
        sh_b = jnp.where(b_valid, shift_ref[hh, i, pl.ds(jb, 1), :], jnp.inf)
        m_old = jnp.where(first, neg_inf, m_ref[hh])
        m_new = jnp.maximum(m_old, jnp.maximum(mx_ref[slot, hh, 0] - sh_a,
                                               mx_ref[slot, hh, 1] - sh_b))
        m_ref[hh] = m_new
        alpha_ref[slot, hh] = jnp.exp2(m_old - m_new)
        return m_new + sh_a, m_new + sh_b

    def softmax_block(slot, hh, u, offset):
        p_ref[slot, hh, u * blk:(u + 1) * blk] = jnp.exp2(s_ref[slot, hh, u] - offset).astype(BF16)

    def softmax(w, slot):
        for hh in heads:
            offsets = softmax_prep(w, slot, hh)
            for u in range(2):
                softmax_block(slot, hh, u, offsets[u])

    ones_rows = jnp.ones((SUM_ROWS, 2 * blk), BF16)

    def pv_dot(w, slot, hh):
        ja, jb = field(_F_JA, w), field(_F_JB, w)
        rows = slice(hh * HEAD_DIM, (hh + 1) * HEAD_DIM)
        vts = jnp.concatenate([vt_ref[0, ja, rows, :], vt_ref[0, jb, rows, :]], axis=1)
        lhs = jnp.concatenate([vts, ones_rows], axis=0)
        return jnp.dot(lhs, p_ref[slot, hh], preferred_element_type=F32)

    def accumulate(pvs, slot):
        for hh in heads:
            acc_ref[hh] = alpha_ref[slot, hh] * acc_ref[hh] + pvs[hh]

    def finalize(w):
        i = field(_F_QBLK, w)
        o_t = jnp.concatenate([acc_ref[hh, :HEAD_DIM] / acc_ref[hh, HEAD_DIM:HEAD_DIM + 1]
                               for hh in heads], axis=0)
        o_ref[0, pl.ds(pl.multiple_of(i * blk, blk), blk), :] = o_t.T.astype(o_ref.dtype)

    acc_ref[...] = jnp.zeros_like(acc_ref)
    m_ref[...] = jnp.zeros_like(m_ref)

    stage_scores([score_dot(0, hh) for hh in heads], 0, 0, heads)
    stage_scores([score_dot(1, hh) for hh in heads], 1, 1, heads)
    softmax(0, 0)

    def item(w, slot):
        sm = 1 - slot
        sts, pvs = {}, {}
        for g, hhs in enumerate(groups):
            for hh in hhs:
                sts[hh] = score_dot(w + 2, hh)
            offs = {hh: softmax_prep(w + 1, sm, hh) for hh in hhs}
            if g > 0:
                stage_scores(sts, w + 2, slot, groups[g - 1])
            for hh in hhs:
                for u in range(2):
                    softmax_block(sm, hh, u, offs[hh][u])
        for hh in heads:
            pvs[hh] = pv_dot(w, slot, hh)
        stage_scores(sts, w + 2, slot, groups[-1])
        accumulate(pvs, slot)
        pl.when(field(_F_LAST, w) == 1)(functools.partial(finalize, w))

    def body(w2, carry):
        item(2 * w2, 0)
        item(2 * w2 + 1, 1)
        return carry

    assert n_items % 2 == 0
    lax.fori_loop(0, n_items // 2, body, 0)


def _attn_call(slopes, bias, q, k, vt, kmean):
    B, S, W = q.shape
    nb = S // MOBA_BLOCK
    step_lanes = GROUPS_PER_STEP * LANES
    n_steps = W // step_lanes
    hps = HEADS_PER_STEP
    n_items, sched = _attn_schedule(nb)
    grid_spec = pltpu.PrefetchScalarGridSpec(
        num_scalar_prefetch=2,
        grid=(B, n_steps),
        in_specs=[
            pl.BlockSpec((1, S, step_lanes), lambda b, p, *_: (b, 0, p)),
            pl.BlockSpec((1, S, step_lanes), lambda b, p, *_: (b, 0, p)),
            pl.BlockSpec((1, nb, step_lanes, MOBA_BLOCK), lambda b, p, *_: (b, 0, p, 0)),
            pl.BlockSpec((1, nb, step_lanes), lambda b, p, *_: (b, 0, p)),
            pl.BlockSpec((hps, 2, MOBA_BLOCK, MOBA_BLOCK), lambda b, p, *_: (p, 0, 0, 0)),
        ],
        out_specs=pl.BlockSpec((1, S, step_lanes), lambda b, p, *_: (b, 0, p)),
        scratch_shapes=[
            pltpu.VMEM((hps, S, LANES), BF16),
            pltpu.VMEM((hps, nb, nb, MOBA_BLOCK), F32),
            pltpu.VMEM((2, hps, 2, MOBA_BLOCK, MOBA_BLOCK), F32),
            pltpu.VMEM((2, hps, 2, 1, MOBA_BLOCK), F32),
            pltpu.VMEM((2, hps, 2 * MOBA_BLOCK, MOBA_BLOCK), BF16),
            pltpu.VMEM((2, hps, 1, MOBA_BLOCK), F32),
            pltpu.VMEM((hps, 1, MOBA_BLOCK), F32),
            pltpu.VMEM((hps, HEAD_DIM + SUM_ROWS, MOBA_BLOCK), F32),
        ],
    )
    return pl.pallas_call(
        functools.partial(_attn_kernel, n_items),
        grid_spec=grid_spec,
        out_shape=jax.ShapeDtypeStruct((B, S, W), BF16),
        compiler_params=pltpu.CompilerParams(
            dimension_semantics=("arbitrary", "arbitrary"), vmem_limit_bytes=VMEM_LIMIT),
        name="moba_attn",
    )(slopes, jnp.asarray(sched), q, k, vt, kmean, bias)


def _ffn_kernel(x_ref, a_ref, p_ref, woa_ref, wop_ref, g2_ref, wup_ref, wdn_ref, g3_ref, o_ref):
    x1 = (x_ref[...]
          + jnp.dot(a_ref[...], woa_ref[...], preferred_element_type=F32)
          + jnp.dot(p_ref[...], wop_ref[...], preferred_element_type=F32))
    h = _rms_norm(x1, g2_ref[...]).astype(BF16)
    o_ref[...] = x1
    for c in range(wup_ref.shape[1] // FF_CHUNK):
        cols = slice(c * FF_CHUNK, (c + 1) * FF_CHUNK)
        up = jnp.dot(h, wup_ref[:, cols], preferred_element_type=F32)
        act = jnp.square(jnp.maximum(up, 0.0)).astype(BF16)
        o_ref[...] += jnp.dot(act, wdn_ref[cols, :], preferred_element_type=F32)
    o_ref[...] = _rms_norm(o_ref[...], g3_ref[...])


def _ffn_call(x, a, p, woa, wop, g2, wup, wdn, g3):
    N, D = x.shape
    T = ROW_TILE_FFN
    const = lambda shape: pl.BlockSpec(shape, lambda t: (0,) * len(shape),
                                       pipeline_mode=pl.Buffered(1))
    return pl.pallas_call(
        _ffn_kernel,
        grid=(N // T,),
        in_specs=[
            pl.BlockSpec((T, D), lambda t: (t, 0)),
            pl.BlockSpec((T, ATTN_WIDTH), lambda t: (t, 0)),
            pl.BlockSpec((T, POOL_WIDTH), lambda t: (t, 0)),
            const(woa.shape), const(wop.shape), const((1, D)),
            const(wup.shape), const(wdn.shape), const((1, D)),
        ],
        out_specs=pl.BlockSpec((T, D), lambda t: (t, 0)),
        out_shape=jax.ShapeDtypeStruct((N, D), F32),
        compiler_params=pltpu.CompilerParams(
            dimension_semantics=("arbitrary",), vmem_limit_bytes=VMEM_LIMIT),
        name="outproj_ffn",
    )(x, a, p, woa, wop, g2, wup, wdn, g3)


def _alibi_constants():
    slopes = jnp.asarray(2.0 ** (-8.0 * np.arange(1, N_HEADS + 1) / N_HEADS) * LOG2E, dtype=F32)
    key = lax.broadcasted_iota(jnp.int32, (MOBA_BLOCK, MOBA_BLOCK), 0)
    qry = lax.broadcasted_iota(jnp.int32, (MOBA_BLOCK, MOBA_BLOCK), 1)
    past = -slopes[:, None, None] * (qry - key).astype(F32)[None]
    own = jnp.where((key <= qry)[None], past, -jnp.inf)
    return slopes, jnp.stack([past, own], axis=1)


def kernel(x, norm_mix, w_in, w_pool, pool_scale, w_out, norm_mlp, w_up, w_down, norm_final):
    B, S, D = x.shape
    assert w_in.shape[0] == 1, "single trunk layer (the final norm is fused into the FFN call)"
    assert S % ROW_TILE_IN == 0 and (B * S) % ROW_TILE_FFN == 0 and S % MOBA_BLOCK == 0
    assert w_in.shape[2] == 3 * ATTN_WIDTH + POOL_WIDTH and D == ATTN_WIDTH + POOL_WIDTH
    wi = w_in[0].astype(BF16)
    wqku = jnp.concatenate([wi[:, :2 * ATTN_WIDTH], wi[:, 3 * ATTN_WIDTH:]], axis=1)
    wvt = wi[:, 2 * ATTN_WIDTH:3 * ATTN_WIDTH].T
    q, k, vt, kmean, p = _inproj_call(
        x, norm_mix[0][None, :], wqku, wvt, w_pool[0].astype(BF16), pool_scale[0][None, :])
    kmean = kmean.reshape(B, S // MOBA_BLOCK, ATTN_WIDTH)
    slopes, bias = _alibi_constants()
    a = _attn_call(slopes, bias, q, k, vt, kmean)
    wo = w_out[0].astype(BF16)
    y = _ffn_call(x.reshape(B * S, D), a.reshape(B * S, ATTN_WIDTH),
                  p.reshape(B * S, POOL_WIDTH), wo[:ATTN_WIDTH], wo[ATTN_WIDTH:],
                  norm_mlp[0][None, :], w_up[0].astype(BF16), w_down[0].astype(BF16),
                  norm_final[None, :])
    return y.reshape(B, S, D)
```

```python
import functools

import jax
import jax.numpy as jnp
import numpy as np
from jax import lax
from jax.experimental import pallas as pl
from jax.experimental.pallas import tpu as pltpu

F32 = jnp.float32
BF16 = jnp.bfloat16

N_HEADS = 8
HEAD_DIM = 64
ATTN_WIDTH = N_HEADS * HEAD_DIM
POOL_WINDOWS = (2, 4, 8, 16)
POOL_GROUP = 128
POOL_WIDTH = POOL_GROUP * len(POOL_WINDOWS)
MOBA_BLOCK = 256
MOBA_TOPK = 3
EPS = 1e-6
LOG2E = 1.4426950408889634
QK_SCALE = HEAD_DIM ** -0.5 * LOG2E
SUM_ROWS = 16
POOL_HALO = 16
LANES = 128
HEADS_PER_GROUP = LANES // HEAD_DIM
GROUPS_PER_STEP = 2
HEADS_PER_STEP = GROUPS_PER_STEP * HEADS_PER_GROUP

ROW_TILE_IN = 512
QK_CHUNK = 256
ROW_TILE_FFN = 512
FF_CHUNK = 1024
VMEM_LIMIT = 56 * 1024 * 1024

NT_DIMS = (((1,), (1,)), ((), ()))


def _rms_norm(x, g):
    ms = jnp.mean(x * x, axis=-1, keepdims=True)
    return x * lax.rsqrt(ms + EPS) * g


def _inproj_kernel(x_ref, g_ref, wi_ref, wvt_ref, wpool_ref, pscale_ref,
                   q_ref, k_ref, vt_ref, kmean_ref, p_ref, halo_ref):
    t = pl.program_id(1)
    tile = x_ref.shape[1]
    nblk = tile // MOBA_BLOCK
    u_cols = slice(3 * ATTN_WIDTH, 3 * ATTN_WIDTH + POOL_WIDTH)

    h = _rms_norm(x_ref[0], g_ref[...]).astype(BF16)
    u = jnp.dot(h, wi_ref[:, u_cols], preferred_element_type=F32)

    @pl.when(t == 0)
    def _():
        halo_ref[...] = jnp.zeros_like(halo_ref)

    ext = jnp.concatenate([halo_ref[...], u], axis=0)
    halo_ref[...] = u[tile - POOL_HALO:, :]
    pos = t * tile + lax.broadcasted_iota(jnp.int32, (tile, POOL_GROUP), 0)

    def pool_group(g, w):
        cols = slice(g * POOL_GROUP, (g + 1) * POOL_GROUP)
        s = ext[:, cols]
        shift = 1
        while shift < w:
            s = s + pltpu.roll(s, shift, axis=0)
            shift *= 2
        cnt = jnp.minimum(pos + 1, w).astype(F32)
        mixed = s[POOL_HALO:, :] / cnt - u[:, cols]
        y = jnp.dot(mixed.astype(BF16), wpool_ref[g], preferred_element_type=F32)
        p_ref[0, :, cols] = (y * pscale_ref[:, cols]).astype(BF16)

    def qk_chunk(c):
        cols = slice(c * QK_CHUNK, (c + 1) * QK_CHUNK)
        out = jnp.dot(h, wi_ref[:, cols], preferred_element_type=F32)
        if c < ATTN_WIDTH // QK_CHUNK:
            q_ref[0, :, cols] = (out * QK_SCALE).astype(BF16)
        else:
            kcols = slice(c * QK_CHUNK - ATTN_WIDTH, (c + 1) * QK_CHUNK - ATTN_WIDTH)
            k_ref[0, :, kcols] = out.astype(BF16)
            for b in range(nblk):
                kmean_ref[0, pl.ds(t * nblk + b, 1), kcols] = (
                    jnp.sum(out[b * MOBA_BLOCK:(b + 1) * MOBA_BLOCK], axis=0, keepdims=True)
                    * (1.0 / MOBA_BLOCK))

    for g, w in enumerate(POOL_WINDOWS):
        pool_group(g, w)
        qk_chunk(g)

    vt = lax.dot_general(wvt_ref[...], h, NT_DIMS, preferred_element_type=F32)
    for b in range(nblk):
        vt_ref[0, b] = vt[:, b * MOBA_BLOCK:(b + 1) * MOBA_BLOCK].astype(BF16)


def _inproj_call(x, g, wi, wvt, wpool, pscale):
    B, S, D = x.shape
    T = ROW_TILE_IN
    nblk = T // MOBA_BLOCK
    nb = S // MOBA_BLOCK
    const = lambda shape: pl.BlockSpec(shape, lambda b, t: (0,) * len(shape),
                                       pipeline_mode=pl.Buffered(1))
    return pl.pallas_call(
        _inproj_kernel,
        grid=(B, S // T),
        in_specs=[
            pl.BlockSpec((1, T, D), lambda b, t: (b, t, 0)),
            const((1, D)),
            const(wi.shape),
            const(wvt.shape),
            const(wpool.shape),
            const((1, POOL_WIDTH)),
        ],
        out_specs=[
            pl.BlockSpec((1, T, ATTN_WIDTH), lambda b, t: (b, t, 0)),
            pl.BlockSpec((1, T, ATTN_WIDTH), lambda b, t: (b, t, 0)),
            pl.BlockSpec((1, nblk, ATTN_WIDTH, MOBA_BLOCK), lambda b, t: (b, t, 0, 0)),
            pl.BlockSpec((1, nb, ATTN_WIDTH), lambda b, t: (b, 0, 0)),
            pl.BlockSpec((1, T, POOL_WIDTH), lambda b, t: (b, t, 0)),
        ],
        out_shape=[
            jax.ShapeDtypeStruct((B, S, ATTN_WIDTH), BF16),
            jax.ShapeDtypeStruct((B, S, ATTN_WIDTH), BF16),
            jax.ShapeDtypeStruct((B, nb, ATTN_WIDTH, MOBA_BLOCK), BF16),
            jax.ShapeDtypeStruct((B, nb, ATTN_WIDTH), F32),
            jax.ShapeDtypeStruct((B, S, POOL_WIDTH), BF16),
        ],
        scratch_shapes=[pltpu.VMEM((POOL_HALO, POOL_WIDTH), F32)],
        compiler_params=pltpu.CompilerParams(
            dimension_semantics=("arbitrary", "arbitrary"), vmem_limit_bytes=VMEM_LIMIT),
        name="inproj_pool",
    )(x, g, wi, wvt, wpool, pscale)


_F_QBLK, _F_JA, _F_JB, _F_A_VALID, _F_B_VALID, _F_FIRST, _F_LAST = range(7)


def _attn_schedule(nb):
    items = []
    for i in range(nb):
        steps = (i + 2) // 2
        for t in range(steps):
            jb = i - 2 * t - 1
            items.append((i, i - 2 * t, max(jb, 0), 1, int(jb >= 0), int(t == 0), int(t == steps - 1)))
    n_items = len(items)
    items += [(nb - 1, 0, 0, 0, 0, 0, 0)] * 2
    return n_items, np.asarray(items, np.int32).T.reshape(-1)


def _attn_kernel(n_items, slopes_ref, sched_ref, q_ref, k_ref, vt_ref, kmean_ref, bias_ref, o_ref,
                 qh_ref, shift_ref, s_ref, mx_ref, p_ref, alpha_ref, m_ref, acc_ref):
    step = pl.program_id(1)
    nb = kmean_ref.shape[1]
    seq = q_ref.shape[1]
    blk = MOBA_BLOCK
    neg_inf = -jnp.inf
    heads = range(HEADS_PER_STEP)
    groups = [tuple(range(g * HEADS_PER_GROUP, (g + 1) * HEADS_PER_GROUP)) for g in range(GROUPS_PER_STEP)]
    stride = n_items + 2

    def group_lanes(hh):
        g = hh // HEADS_PER_GROUP
        return slice(g * LANES, (g + 1) * LANES)

    def in_head(hh):
        lane = lax.broadcasted_iota(jnp.int32, (1, LANES), 1)
        lo = (hh % HEADS_PER_GROUP) * HEAD_DIM
        return ((lane >= lo) & (lane < lo + HEAD_DIM)).astype(BF16)

    def field(f, w):
        return sched_ref[f * stride + w]

    kmean = kmean_ref[0].astype(BF16)
    key_blk = lax.broadcasted_iota(jnp.int32, (nb, seq), 0)
    qry_blk = lax.broadcasted_iota(jnp.int32, (nb, seq), 1) // blk
    dist = ((qry_blk - key_blk) * blk).astype(F32)
    for hh in heads:
        q_h = q_ref[0, :, group_lanes(hh)] * in_head(hh)
        qh_ref[hh] = q_h
        gate = lax.dot_general(kmean[:, group_lanes(hh)], q_h, NT_DIMS,
                               preferred_element_type=F32)
        gate = jnp.where(key_blk < qry_blk, gate, neg_inf)
        sel = key_blk == qry_blk
        for _ in range(MOBA_TOPK):
            top = jnp.max(gate, axis=0, keepdims=True)
            first = jnp.min(jnp.where(gate == top, key_blk, nb), axis=0, keepdims=True)
            pick = (key_blk == first) & (top > neg_inf)
            sel = sel | pick
            gate = jnp.where(pick, neg_inf, gate)
        slope = slopes_ref[step * HEADS_PER_STEP + hh]
        shift = jnp.where(sel, slope * dist, jnp.inf)
        for i in range(nb):
            shift_ref[hh, i] = shift[:, i * blk:(i + 1) * blk]

    def score_dot(w, hh):
        i, ja, jb = field(_F_QBLK, w), field(_F_JA, w), field(_F_JB, w)
        kk = jnp.concatenate(
            [k_ref[0, pl.ds(pl.multiple_of(ja * blk, blk), blk), group_lanes(hh)],
             k_ref[0, pl.ds(pl.multiple_of(jb * blk, blk), blk), group_lanes(hh)]], axis=0)
        q_h = qh_ref[hh, pl.ds(pl.multiple_of(i * blk, blk), blk), :]
        return lax.dot_general(kk, q_h, NT_DIMS, preferred_element_type=F32)

    def stage_unit(st, w, slot, hh, u):
        table = field(_F_FIRST, w) if u == 0 else 0
        s_u = st[u * blk:(u + 1) * blk] + bias_ref[hh, table]
        s_ref[slot, hh, u] = s_u
        mx_ref[slot, hh, u] = jnp.max(s_u, axis=0, keepdims=True)

    def stage_scores(sts, w, slot, hhs):
        for hh in hhs:
            for u in range(2):
                stage_unit(sts[hh], w, slot, hh, u)

    def softmax_prep(w, slot, hh):
        i, ja, jb = field(_F_QBLK, w), field(_F_JA, w), field(_F_JB, w)
        a_valid, b_valid = field(_F_A_VALID, w) == 1, field(_F_B_VALID, w) == 1
        first = field(_F_FIRST, w) == 1
        sh_a = jnp.where(a_valid, shift_ref[hh, i, pl.ds(ja, 1), :], jnp.inf)
        sh_b = jnp.where(b_valid, shift_ref[hh, i, pl.ds(jb, 1), :], jnp.inf)
        m_old = jnp.where(first, neg_inf, m_ref[hh])
        m_new = jnp.maximum(m_old, jnp.maximum(mx_ref[slot, hh, 0] - sh_a,
                                               mx_ref[slot, hh, 1] - sh_b))
        m_ref[hh] = m_new
        alpha_ref[slot, hh] = jnp.exp2(m_old - m_new)
        return m_new + sh_a, m_new + sh_b

    def softmax_block(slot, hh, u, offset):
        p_ref[slot, hh, u * blk:(u + 1) * blk] = jnp.exp2(s_ref[slot, hh, u] - offset).astype(BF16)

    def softmax(w, slot):
        for hh in heads:
            offsets = softmax_prep(w, slot, hh)
            for u in range(2):
                softmax_block(slot, hh, u, offsets[u])

    ones_rows = jnp.ones((SUM_ROWS, 2 * blk), BF16)

    def pv_dot(w, slot, hh):
        ja, jb = field(_F_JA, w), field(_F_JB, w)
        rows = slice(hh * HEAD_DIM, (hh + 1) * HEAD_DIM)
        vts = jnp.concatenate([vt_ref[0, ja, rows, :], vt_ref[0, jb, rows, :]], axis=1)
        lhs = jnp.concatenate([vts, ones_rows], axis=0)
        return jnp.dot(lhs, p_ref[slot, hh], preferred_element_type=F32)

    def accumulate(pvs, slot):
        for hh in heads:
            acc_ref[hh] = alpha_ref[slot, hh] * acc_ref[hh] + pvs[hh]

    def finalize(w):
        i = field(_F_QBLK, w)
        o_t = jnp.concatenate([acc_ref[hh, :HEAD_DIM] / acc_ref[hh, HEAD_DIM:HEAD_DIM + 1]
                               for hh in heads], axis=0)
        o_ref[0, pl.ds(pl.multiple_of(i * blk, blk), blk), :] = o_t.T.astype(o_ref.dtype)

    acc_ref[...] = jnp.zeros_like(acc_ref)
    m_ref[...] = jnp.zeros_like(m_ref)

    stage_scores([score_dot(0, hh) for hh in heads], 0, 0, heads)
    stage_scores([score_dot(1, hh) for hh in heads], 1, 1, heads)
    softmax(0, 0)

    def item(w, slot):
        sm = 1 - slot
        sts, pvs = {}, {}
        for g, hhs in enumerate(groups):
            for hh in hhs:
                sts[hh] = score_dot(w + 2, hh)
            offs = {hh: softmax_prep(w + 1, sm, hh) for hh in hhs}
            if g > 0:
                stage_scores(sts, w + 2, slot, groups[g - 1])
            for hh in hhs:
                for u in range(2):
                    softmax_block(sm, hh, u, offs[hh][u])
        for hh in heads:
            pvs[hh] = pv_dot(w, slot, hh)
        stage_scores(sts, w + 2, slot, groups[-1])
        accumulate(pvs, slot)
        pl.when(field(_F_LAST, w) == 1)(functools.partial(finalize, w))

    def body(w2, carry):
        item(2 * w2, 0)
        item(2 * w2 + 1, 1)
        return carry

    assert n_items % 2 == 0
    lax.fori_loop(0, n_items // 2, body, 0)


def _attn_call(slopes, bias, q, k, vt, kmean):
    B, S, W = q.shape
    nb = S // MOBA_BLOCK
    step_lanes = GROUPS_PER_STEP * LANES
    n_steps = W // step_lanes
    hps = HEADS_PER_STEP
    n_items, sched = _attn_schedule(nb)
    grid_spec = pltpu.PrefetchScalarGridSpec(
        num_scalar_prefetch=2,
        grid=(B, n_steps),
        in_specs=[
            pl.BlockSpec((1, S, step_lanes), lambda b, p, *_: (b, 0, p)),
            pl.BlockSpec((1, S, step_lanes), lambda b, p, *_: (b, 0, p)),
            pl.BlockSpec((1, nb, step_lanes, MOBA_BLOCK), lambda b, p, *_: (b, 0, p, 0)),
            pl.BlockSpec((1, nb, step_lanes), lambda b, p, *_: (b, 0, p)),
            pl.BlockSpec((hps, 2, MOBA_BLOCK, MOBA_BLOCK), lambda b, p, *_: (p, 0, 0, 0)),
        ],
        out_specs=pl.BlockSpec((1, S, step_lanes), lambda b, p, *_: (b, 0, p)),
        scratch_shapes=[
            pltpu.VMEM((hps, S, LANES), BF16),
            pltpu.VMEM((hps, nb, nb, MOBA_BLOCK), F32),
            pltpu.VMEM((2, hps, 2, MOBA_BLOCK, MOBA_BLOCK), F32),
            pltpu.VMEM((2, hps, 2, 1, MOBA_BLOCK), F32),
            pltpu.VMEM((2, hps, 2 * MOBA_BLOCK, MOBA_BLOCK), BF16),
            pltpu.VMEM((2, hps, 1, MOBA_BLOCK), F32),
            pltpu.VMEM((hps, 1, MOBA_BLOCK), F32),
            pltpu.VMEM((hps, HEAD_DIM + SUM_ROWS, MOBA_BLOCK), F32),
        ],
    )
    return pl.pallas_call(
        functools.partial(_attn_kernel, n_items),
        grid_spec=grid_spec,
        out_shape=jax.ShapeDtypeStruct((B, S, W), BF16),
        compiler_params=pltpu.CompilerParams(
            dimension_semantics=("arbitrary", "arbitrary"), vmem_limit_bytes=VMEM_LIMIT),
        name="moba_attn",
    )(slopes, jnp.asarray(sched), q, k, vt, kmean, bias)


def _ffn_kernel(x_ref, a_ref, p_ref, wo_ref, g2_ref, wup_ref, wdn_ref, g3_ref, o_ref):
    x1 = (x_ref[...]
          + jnp.dot(a_ref[...], wo_ref[:ATTN_WIDTH, :], preferred_element_type=F32)
          + jnp.dot(p_ref[...], wo_ref[ATTN_WIDTH:, :], preferred_element_type=F32))
    h = _rms_norm(x1, g2_ref[...]).astype(BF16)
    o_ref[...] = x1
    for c in range(wup_ref.shape[1] // FF_CHUNK):
        cols = slice(c * FF_CHUNK, (c + 1) * FF_CHUNK)
        up = jnp.dot(h, wup_ref[:, cols], preferred_element_type=F32)
        act = jnp.square(jnp.maximum(up, 0.0)).astype(BF16)
        o_ref[...] += jnp.dot(act, wdn_ref[cols, :], preferred_element_type=F32)
    o_ref[...] = _rms_norm(o_ref[...], g3_ref[...])


def _ffn_call(x, a, p, wo, g2, wup, wdn, g3):
    N, D = x.shape
    T = ROW_TILE_FFN
    const = lambda shape: pl.BlockSpec(shape, lambda t: (0,) * len(shape),
                                       pipeline_mode=pl.Buffered(1))
    return pl.pallas_call(
        _ffn_kernel,
        grid=(N // T,),
        in_specs=[
            pl.BlockSpec((T, D), lambda t: (t, 0)),
            pl.BlockSpec((T, ATTN_WIDTH), lambda t: (t, 0)),
            pl.BlockSpec((T, POOL_WIDTH), lambda t: (t, 0)),
            const(wo.shape), const((1, D)),
            const(wup.shape), const(wdn.shape), const((1, D)),
        ],
        out_specs=pl.BlockSpec((T, D), lambda t: (t, 0)),
        out_shape=jax.ShapeDtypeStruct((N, D), F32),
        compiler_params=pltpu.CompilerParams(
            dimension_semantics=("arbitrary",), vmem_limit_bytes=VMEM_LIMIT),
        name="outproj_ffn",
    )(x, a, p, wo, g2, wup, wdn, g3)


def _alibi_constants():
    slopes = (2.0 ** (-8.0 * np.arange(1, N_HEADS + 1) / N_HEADS) * LOG2E).astype(np.float32)
    key = np.arange(MOBA_BLOCK, dtype=np.float32)[:, None]
    qry = np.arange(MOBA_BLOCK, dtype=np.float32)[None, :]
    past = -slopes[:, None, None] * (qry - key)[None]
    own = np.where((key <= qry)[None], past, -np.inf).astype(np.float32)
    return jnp.asarray(slopes), jnp.asarray(np.stack([past, own], axis=1))


def kernel(x, norm_mix, w_in, w_pool, pool_scale, w_out, norm_mlp, w_up, w_down, norm_final):
    B, S, D = x.shape
    assert w_in.shape[0] == 1, "single trunk layer (the final norm is fused into the FFN call)"
    assert S % ROW_TILE_IN == 0 and (B * S) % ROW_TILE_FFN == 0 and S % MOBA_BLOCK == 0
    assert w_in.shape[2] == 3 * ATTN_WIDTH + POOL_WIDTH and D == ATTN_WIDTH + POOL_WIDTH
    wi = w_in[0].astype(BF16)
    wvt = wi[:, 2 * ATTN_WIDTH:3 * ATTN_WIDTH].T
    q, k, vt, kmean, p = _inproj_call(
        x, norm_mix[0][None, :], wi, wvt, w_pool[0].astype(BF16), pool_scale[0][None, :])
    slopes, bias = _alibi_constants()
    a = _attn_call(slopes, bias, q, k, vt, kmean)
    wo = w_out[0].astype(BF16)
    y = _ffn_call(x.reshape(B * S, D), a.reshape(B * S, ATTN_WIDTH),
                  p.reshape(B * S, POOL_WIDTH), wo,
                  norm_mlp[0][None, :], w_up[0].astype(BF16), w_down[0].astype(BF16),
                  norm_final[None, :])
    return y.reshape(B, S, D)
```

```python
import functools

import jax
import jax.numpy as jnp
import numpy as np
from jax import lax
from jax.experimental import pallas as pl
from jax.experimental.pallas import tpu as pltpu

F32 = jnp.float32
BF16 = jnp.bfloat16

N_HEADS = 8
HEAD_DIM = 64
ATTN_WIDTH = N_HEADS * HEAD_DIM
POOL_WINDOWS = (2, 4, 8, 16)
POOL_GROUP = 128
POOL_WIDTH = POOL_GROUP * len(POOL_WINDOWS)
MOBA_BLOCK = 256
MOBA_TOPK = 3
EPS = 1e-6
LOG2E = 1.4426950408889634
QK_SCALE = HEAD_DIM ** -0.5 * LOG2E
SUM_ROWS = 16
POOL_HALO = 16
LANES = 128
HEADS_PER_GROUP = LANES // HEAD_DIM
GROUPS_PER_STEP = 2
HEADS_PER_STEP = GROUPS_PER_STEP * HEADS_PER_GROUP

ROW_TILE_IN = 512
ROW_TILE_FFN = 1024
FF_CHUNK = 1024
VMEM_LIMIT = 56 * 1024 * 1024

NT_DIMS = (((1,), (1,)), ((), ()))


def _rms_norm(x, g):
    ms = jnp.mean(x * x, axis=-1, keepdims=True)
    return x * lax.rsqrt(ms + EPS) * g


def _inproj_kernel(x_ref, g_ref, wi_ref, wvt_ref, wpool_ref, pscale_ref,
                   q_ref, k_ref, vt_ref, kmean_ref, p_ref, halo_ref):
    t = pl.program_id(1)
    tile = x_ref.shape[1]
    nblk = tile // MOBA_BLOCK
    u_cols = slice(3 * ATTN_WIDTH, 3 * ATTN_WIDTH + POOL_WIDTH)

    h = _rms_norm(x_ref[0], g_ref[...]).astype(BF16)
    qk = jnp.dot(h, wi_ref[:, :2 * ATTN_WIDTH], preferred_element_type=F32)
    q_ref[0] = (qk[:, :ATTN_WIDTH] * QK_SCALE).astype(BF16)
    kf = qk[:, ATTN_WIDTH:]
    k_ref[0] = kf.astype(BF16)
    for b in range(nblk):
        kmean_ref[0, pl.ds(t * nblk + b, 1), :] = (
            jnp.sum(kf[b * MOBA_BLOCK:(b + 1) * MOBA_BLOCK], axis=0, keepdims=True)
            * (1.0 / MOBA_BLOCK))

    vt = lax.dot_general(wvt_ref[...], h, NT_DIMS, preferred_element_type=F32)
    for b in range(nblk):
        vt_ref[0, b] = vt[:, b * MOBA_BLOCK:(b + 1) * MOBA_BLOCK].astype(BF16)

    u = jnp.dot(h, wi_ref[:, u_cols], preferred_element_type=F32)

    @pl.when(t == 0)
    def _():
        halo_ref[...] = jnp.zeros_like(halo_ref)

    ext = jnp.concatenate([halo_ref[...], u], axis=0)
    halo_ref[...] = u[tile - POOL_HALO:, :]
    pos = t * tile + lax.broadcasted_iota(jnp.int32, (tile, POOL_GROUP), 0)
    for g, w in enumerate(POOL_WINDOWS):
        cols = slice(g * POOL_GROUP, (g + 1) * POOL_GROUP)
        s = ext[:, cols]
        shift = 1
        while shift < w:
            s = s + pltpu.roll(s, shift, axis=0)
            shift *= 2
        cnt = jnp.minimum(pos + 1, w).astype(F32)
        mixed = s[POOL_HALO:, :] / cnt - u[:, cols]
        y = jnp.dot(mixed.astype(BF16), wpool_ref[g], preferred_element_type=F32)
        p_ref[0, :, cols] = (y * pscale_ref[:, cols]).astype(BF16)


def _inproj_call(x, g, wi, wvt, wpool, pscale):
    B, S, D = x.shape
    T = ROW_TILE_IN
    nblk = T // MOBA_BLOCK
    nb = S // MOBA_BLOCK
    const = lambda shape: pl.BlockSpec(shape, lambda b, t: (0,) * len(shape),
                                       pipeline_mode=pl.Buffered(1))
    return pl.pallas_call(
        _inproj_kernel,
        grid=(B, S // T),
        in_specs=[
            pl.BlockSpec((1, T, D), lambda b, t: (b, t, 0)),
            const((1, D)),
            const(wi.shape),
            const(wvt.shape),
            const(wpool.shape),
            const((1, POOL_WIDTH)),
        ],
        out_specs=[
            pl.BlockSpec((1, T, ATTN_WIDTH), lambda b, t: (b, t, 0)),
            pl.BlockSpec((1, T, ATTN_WIDTH), lambda b, t: (b, t, 0)),
            pl.BlockSpec((1, nblk, ATTN_WIDTH, MOBA_BLOCK), lambda b, t: (b, t, 0, 0)),
            pl.BlockSpec((1, nb, ATTN_WIDTH), lambda b, t: (b, 0, 0)),
            pl.BlockSpec((1, T, POOL_WIDTH), lambda b, t: (b, t, 0)),
        ],
        out_shape=[
            jax.ShapeDtypeStruct((B, S, ATTN_WIDTH), BF16),
            jax.ShapeDtypeStruct((B, S, ATTN_WIDTH), BF16),
            jax.ShapeDtypeStruct((B, nb, ATTN_WIDTH, MOBA_BLOCK), BF16),
            jax.ShapeDtypeStruct((B, nb, ATTN_WIDTH), F32),
            jax.ShapeDtypeStruct((B, S, POOL_WIDTH), BF16),
        ],
        scratch_shapes=[pltpu.VMEM((POOL_HALO, POOL_WIDTH), F32)],
        compiler_params=pltpu.CompilerParams(
            dimension_semantics=("arbitrary", "arbitrary"), vmem_limit_bytes=VMEM_LIMIT),
        name="inproj_pool",
    )(x, g, wi, wvt, wpool, pscale)


_F_QBLK, _F_JA, _F_JB, _F_A_VALID, _F_B_VALID, _F_FIRST, _F_LAST = range(7)


def _attn_schedule(nb):
    items = []
    for i in range(nb):
        steps = (i + 2) // 2
        for t in range(steps):
            jb = i - 2 * t - 1
            items.append((i, i - 2 * t, max(jb, 0), 1, int(jb >= 0), int(t == 0), int(t == steps - 1)))
    n_items = len(items)
    items += [(nb - 1, 0, 0, 0, 0, 0, 0)] * 2
    return n_items, np.asarray(items, np.int32).T.reshape(-1)


def _attn_kernel(n_items, slopes_ref, sched_ref, q_ref, k_ref, vt_ref, kmean_ref, bias_ref, o_ref,
                 qh_ref, shift_ref, s_ref, mx_ref, p_ref, alpha_ref, m_ref, acc_ref):
    step = pl.program_id(1)
    nb = kmean_ref.shape[1]
    seq = q_ref.shape[1]
    blk = MOBA_BLOCK
    neg_inf = -jnp.inf
    heads = range(HEADS_PER_STEP)
    groups = [tuple(range(g * HEADS_PER_GROUP, (g + 1) * HEADS_PER_GROUP)) for g in range(GROUPS_PER_STEP)]
    stride = n_items + 2

    def group_lanes(hh):
        g = hh // HEADS_PER_GROUP
        return slice(g * LANES, (g + 1) * LANES)

    def in_head(hh):
        lane = lax.broadcasted_iota(jnp.int32, (1, LANES), 1)
        lo = (hh % HEADS_PER_GROUP) * HEAD_DIM
        return ((lane >= lo) & (lane < lo + HEAD_DIM)).astype(BF16)

    def field(f, w):
        return sched_ref[f * stride + w]

    kmean = kmean_ref[0].astype(BF16)
    key_blk = lax.broadcasted_iota(jnp.int32, (nb, seq), 0)
    qry_blk = lax.broadcasted_iota(jnp.int32, (nb, seq), 1) // blk
    dist = ((qry_blk - key_blk) * blk).astype(F32)
    for hh in heads:
        q_h = q_ref[0, :, group_lanes(hh)] * in_head(hh)
        qh_ref[hh] = q_h
        gate = lax.dot_general(kmean[:, group_lanes(hh)], q_h, NT_DIMS,
                               preferred_element_type=F32)
        gate = jnp.where(key_blk < qry_blk, gate, neg_inf)
        sel = key_blk == qry_blk
        for _ in range(MOBA_TOPK):
            top = jnp.max(gate, axis=0, keepdims=True)
            first = jnp.min(jnp.where(gate == top, key_blk, nb), axis=0, keepdims=True)
            pick = (key_blk == first) & (top > neg_inf)
            sel = sel | pick
            gate = jnp.where(pick, neg_inf, gate)
        slope = slopes_ref[step * HEADS_PER_STEP + hh]
        shift = jnp.where(sel, slope * dist, jnp.inf)
        for i in range(nb):
            shift_ref[hh, i] = shift[:, i * blk:(i + 1) * blk]

    def score_dot(w, hh):
        i, ja, jb = field(_F_QBLK, w), field(_F_JA, w), field(_F_JB, w)
        kk = jnp.concatenate(
            [k_ref[0, pl.ds(pl.multiple_of(ja * blk, blk), blk), group_lanes(hh)],
             k_ref[0, pl.ds(pl.multiple_of(jb * blk, blk), blk), group_lanes(hh)]], axis=0)
        q_h = qh_ref[hh, pl.ds(pl.multiple_of(i * blk, blk), blk), :]
        return lax.dot_general(kk, q_h, NT_DIMS, preferred_element_type=F32)

    def stage_unit(st, w, slot, hh, u):
        table = field(_F_FIRST, w) if u == 0 else 0
        s_u = st[u * blk:(u + 1) * blk] + bias_ref[hh, table]
        s_ref[slot, hh, u] = s_u
        mx_ref[slot, hh, u] = jnp.max(s_u, axis=0, keepdims=True)

    def stage_scores(sts, w, slot, hhs):
        for hh in hhs:
            for u in range(2):
                stage_unit(sts[hh], w, slot, hh, u)

    def softmax_prep(w, slot, hh):
        i, ja, jb = field(_F_QBLK, w), field(_F_JA, w), field(_F_JB, w)
        a_valid, b_valid = field(_F_A_VALID, w) == 1, field(_F_B_VALID, w) == 1
        first = field(_F_FIRST, w) == 1
        sh_a = jnp.where(a_valid, shift_ref[hh, i, pl.ds(ja, 1), :], jnp.inf)
        sh_b = jnp.where(b_valid, shift_ref[hh, i, pl.ds(jb, 1), :], jnp.inf)
        m_old = jnp.where(first, neg_inf, m_ref[hh])
        m_new = jnp.maximum(m_old, jnp.maximum(mx_ref[slot, hh, 0] - sh_a,
                                               mx_ref[slot, hh, 1] - sh_b))
        m_ref[hh] = m_new
        alpha_ref[slot, hh] = jnp.exp2(m_old - m_new)
        return m_new + sh_a, m_new + sh_b

    def softmax_block(slot, hh, u, offset):
        p_ref[slot, hh, u * blk:(u + 1) * blk] = jnp.exp2(s_ref[slot, hh, u] - offset).astype(BF16)

    def softmax(w, slot):
        for hh in heads:
            offsets = softmax_prep(w, slot, hh)
            for u in range(2):
                softmax_block(slot, hh, u, offsets[u])

    ones_rows = jnp.ones((SUM_ROWS, 2 * blk), BF16)

    def pv_dot(w, slot, hh):
        ja, jb = field(_F_JA, w), field(_F_JB, w)
        rows = slice(hh * HEAD_DIM, (hh + 1) * HEAD_DIM)
        vts = jnp.concatenate([vt_ref[0, ja, rows, :], vt_ref[0, jb, rows, :]], axis=1)
        lhs = jnp.concatenate([vts, ones_rows], axis=0)
        return jnp.dot(lhs, p_ref[slot, hh], preferred_element_type=F32)

    def accumulate(pvs, slot):
        for hh in heads:
            acc_ref[hh] = alpha_ref[slot, hh] * acc_ref[hh] + pvs[hh]

    def finalize(w):
        i = field(_F_QBLK, w)
        o_t = jnp.concatenate([acc_ref[hh, :HEAD_DIM] / acc_ref[hh, HEAD_DIM:HEAD_DIM + 1]
                               for hh in heads], axis=0)
        o_ref[0, pl.ds(pl.multiple_of(i * blk, blk), blk), :] = o_t.T.astype(o_ref.dtype)

    acc_ref[...] = jnp.zeros_like(acc_ref)
    m_ref[...] = jnp.zeros_like(m_ref)

    stage_scores([score_dot(0, hh) for hh in heads], 0, 0, heads)
    stage_scores([score_dot(1, hh) for hh in heads], 1, 1, heads)
    softmax(0, 0)

    def item(w, slot):
        sm = 1 - slot
        sts, pvs = {}, {}
        for g, hhs in enumerate(groups):
            for hh in hhs:
                sts[hh] = score_dot(w + 2, hh)
            offs = {hh: softmax_prep(w + 1, sm, hh) for hh in hhs}
            if g > 0:
                stage_scores(sts, w + 2, slot, groups[g - 1])
            for hh in hhs:
                for u in range(2):
                    softmax_block(sm, hh, u, offs[hh][u])
        for hh in heads:
            pvs[hh] = pv_dot(w, slot, hh)
        stage_scores(sts, w + 2, slot, groups[-1])
        accumulate(pvs, slot)
        pl.when(field(_F_LAST, w) == 1)(functools.partial(finalize, w))

    def body(w2, carry):
        item(2 * w2, 0)
        item(2 * w2 + 1, 1)
        return carry

    assert n_items % 2 == 0
    lax.fori_loop(0, n_items // 2, body, 0)


def _attn_call(slopes, bias, q, k, vt, kmean):
    B, S, W = q.shape
    nb = S // MOBA_BLOCK
    step_lanes = GROUPS_PER_STEP * LANES
    n_steps = W // step_lanes
    hps = HEADS_PER_STEP
    n_items, sched = _attn_schedule(nb)
    grid_spec = pltpu.PrefetchScalarGridSpec(
        num_scalar_prefetch=2,
        grid=(B, n_steps),
        in_specs=[
            pl.BlockSpec((1, S, step_lanes), lambda b, p, *_: (b, 0, p)),
            pl.BlockSpec((1, S, step_lanes), lambda b, p, *_: (b, 0, p)),
            pl.BlockSpec((1, nb, step_lanes, MOBA_BLOCK), lambda b, p, *_: (b, 0, p, 0)),
            pl.BlockSpec((1, nb, step_lanes), lambda b, p, *_: (b, 0, p)),
            pl.BlockSpec((hps, 2, MOBA_BLOCK, MOBA_BLOCK), lambda b, p, *_: (p, 0, 0, 0)),
        ],
        out_specs=pl.BlockSpec((1, S, step_lanes), lambda b, p, *_: (b, 0, p)),
        scratch_shapes=[
            pltpu.VMEM((hps, S, LANES), BF16),
            pltpu.VMEM((hps, nb, nb, MOBA_BLOCK), F32),
            pltpu.VMEM((2, hps, 2, MOBA_BLOCK, MOBA_BLOCK), F32),
            pltpu.VMEM((2, hps, 2, 1, MOBA_BLOCK), F32),
            pltpu.VMEM((2, hps, 2 * MOBA_BLOCK, MOBA_BLOCK), BF16),
            pltpu.VMEM((2, hps, 1, MOBA_BLOCK), F32),
            pltpu.VMEM((hps, 1, MOBA_BLOCK), F32),
            pltpu.VMEM((hps, HEAD_DIM + SUM_ROWS, MOBA_BLOCK), F32),
        ],
    )
    return pl.pallas_call(
        functools.partial(_attn_kernel, n_items),
        grid_spec=grid_spec,
        out_shape=jax.ShapeDtypeStruct((B, S, W), BF16),
        compiler_params=pltpu.CompilerParams(
            dimension_semantics=("arbitrary", "arbitrary"), vmem_limit_bytes=VMEM_LIMIT),
        name="moba_attn",
    )(slopes, jnp.asarray(sched), q, k, vt, kmean, bias)


def _ffn_kernel(x_ref, a_ref, p_ref, wo_ref, g2_ref, wup_ref, wdn_ref, g3_ref, o_ref):
    x1 = (x_ref[...]
          + jnp.dot(a_ref[...], wo_ref[:ATTN_WIDTH, :], preferred_element_type=F32)
          + jnp.dot(p_ref[...], wo_ref[ATTN_WIDTH:, :], preferred_element_type=F32))
    h = _rms_norm(x1, g2_ref[...]).astype(BF16)
    o_ref[...] = x1
    for c in range(wup_ref.shape[1] // FF_CHUNK):
        cols = slice(c * FF_CHUNK, (c + 1) * FF_CHUNK)
        up = jnp.dot(h, wup_ref[:, cols], preferred_element_type=F32)
        act = jnp.square(jnp.maximum(up, 0.0)).astype(BF16)
        o_ref[...] += jnp.dot(act, wdn_ref[cols, :], preferred_element_type=F32)
    o_ref[...] = _rms_norm(o_ref[...], g3_ref[...])


def _ffn_call(x, a, p, wo, g2, wup, wdn, g3):
    N, D = x.shape
    T = ROW_TILE_FFN
    const = lambda shape: pl.BlockSpec(shape, lambda t: (0,) * len(shape),
                                       pipeline_mode=pl.Buffered(1))
    return pl.pallas_call(
        _ffn_kernel,
        grid=(N // T,),
        in_specs=[
            pl.BlockSpec((T, D), lambda t: (t, 0)),
            pl.BlockSpec((T, ATTN_WIDTH), lambda t: (t, 0)),
            pl.BlockSpec((T, POOL_WIDTH), lambda t: (t, 0)),
            const(wo.shape), const((1, D)),
            const(wup.shape), const(wdn.shape), const((1, D)),
        ],
        out_specs=pl.BlockSpec((T, D), lambda t: (t, 0)),
        out_shape=jax.ShapeDtypeStruct((N, D), F32),
        compiler_params=pltpu.CompilerParams(
            dimension_semantics=("arbitrary",), vmem_limit_bytes=VMEM_LIMIT),
        name="outproj_ffn",
    )(x, a, p, wo, g2, wup, wdn, g3)


def _alibi_constants():
    slopes = (2.0 ** (-8.0 * np.arange(1, N_HEADS + 1) / N_HEADS) * LOG2E).astype(np.float32)
    key = np.arange(MOBA_BLOCK, dtype=np.float32)[:, None]
    qry = np.arange(MOBA_BLOCK, dtype=np.float32)[None, :]
    past = -slopes[:, None, None] * (qry - key)[None]
    own = np.where((key <= qry)[None], past, -np.inf).astype(np.float32)
    return jnp.asarray(slopes), jnp.asarray(np.stack([past, own], axis=1))


def kernel(x, norm_mix, w_in, w_pool, pool_scale, w_out, norm_mlp, w_up, w_down, norm_final):
    B, S, D = x.shape
    assert w_in.shape[0] == 1, "single trunk layer (the final norm is fused into the FFN call)"
    assert S % ROW_TILE_IN == 0 and (B * S) % ROW_TILE_FFN == 0 and S % MOBA_BLOCK == 0
    assert w_in.shape[2] == 3 * ATTN_WIDTH + POOL_WIDTH and D == ATTN_WIDTH + POOL_WIDTH
    wi = w_in[0].astype(BF16)
    wvt = wi[:, 2 * ATTN_WIDTH:3 * ATTN_WIDTH].T
    q, k, vt, kmean, p = _inproj_call(
        x, norm_mix[0][None, :], wi, wvt, w_pool[0].astype(BF16), pool_scale[0][None, :])
    slopes, bias = _alibi_constants()
    a = _attn_call(slopes, bias, q, k, vt, kmean)
    wo = w_out[0].astype(BF16)
    y = _ffn_call(x.reshape(B * S, D), a.reshape(B * S, ATTN_WIDTH),
                  p.reshape(B * S, POOL_WIDTH), wo,
                  norm_mlp[0][None, :], w_up[0].astype(BF16), w_down[0].astype(BF16),
                  norm_final[None, :])
    return y.reshape(B, S, D)
```

```python
import functools

import jax
import jax.numpy as jnp
import numpy as np
from jax import lax
from jax.experimental import pallas as pl
from jax.experimental.pallas import tpu as pltpu

F32 = jnp.float32
BF16 = jnp.bfloat16

N_HEADS = 8
HEAD_DIM = 64
ATTN_WIDTH = N_HEADS * HEAD_DIM
POOL_WINDOWS = (2, 4, 8, 16)
POOL_GROUP = 128
POOL_WIDTH = POOL_GROUP * len(POOL_WINDOWS)
MOBA_BLOCK = 256
MOBA_TOPK = 3
EPS = 1e-6
LOG2E = 1.4426950408889634
QK_SCALE = HEAD_DIM ** -0.5 * LOG2E
SUM_ROWS = 16
POOL_HALO = 16
LANES = 128
HEADS_PER_GROUP = LANES // HEAD_DIM
GROUPS_PER_STEP = 2
HEADS_PER_STEP = GROUPS_PER_STEP * HEADS_PER_GROUP
ITEMS_PER_TRIP = 2
ACC_BUFFERS = 2

ROW_TILE_IN = 512
ROW_TILE_FFN = 1024
FF_CHUNK = 1024
VMEM_LIMIT = 56 * 1024 * 1024

NT_DIMS = (((1,), (1,)), ((), ()))


def _rms_norm(x, g):
    ms = jnp.mean(x * x, axis=-1, keepdims=True)
    return x * lax.rsqrt(ms + EPS) * g


def _inproj_kernel(x_ref, g_ref, wi_ref, wvt_ref, wpool_ref, pscale_ref,
                   q_ref, k_ref, vt_ref, kmean_ref, p_ref, halo_ref):
    t = pl.program_id(1)
    tile = x_ref.shape[1]
    nblk = tile // MOBA_BLOCK
    u_cols = slice(3 * ATTN_WIDTH, 3 * ATTN_WIDTH + POOL_WIDTH)

    h = _rms_norm(x_ref[0], g_ref[...]).astype(BF16)
    qk = jnp.dot(h, wi_ref[:, :2 * ATTN_WIDTH], preferred_element_type=F32)
    q_ref[0] = (qk[:, :ATTN_WIDTH] * QK_SCALE).astype(BF16)
    kf = qk[:, ATTN_WIDTH:]
    k_ref[0] = kf.astype(BF16)
    for b in range(nblk):
        kmean_ref[0, pl.ds(t * nblk + b, 1), :] = (
            jnp.sum(kf[b * MOBA_BLOCK:(b + 1) * MOBA_BLOCK], axis=0, keepdims=True)
            * (1.0 / MOBA_BLOCK))

    vt = lax.dot_general(wvt_ref[...], h, NT_DIMS, preferred_element_type=F32)
    for b in range(nblk):
        vt_ref[0, b] = vt[:, b * MOBA_BLOCK:(b + 1) * MOBA_BLOCK].astype(BF16)

    u = jnp.dot(h, wi_ref[:, u_cols], preferred_element_type=F32)

    @pl.when(t == 0)
    def _():
        halo_ref[...] = jnp.zeros_like(halo_ref)

    ext = jnp.concatenate([halo_ref[...], u], axis=0)
    halo_ref[...] = u[tile - POOL_HALO:, :]
    pos = t * tile + lax.broadcasted_iota(jnp.int32, (tile, POOL_GROUP), 0)
    for g, w in enumerate(POOL_WINDOWS):
        cols = slice(g * POOL_GROUP, (g + 1) * POOL_GROUP)
        s = ext[:, cols]
        shift = 1
        while shift < w:
            s = s + pltpu.roll(s, shift, axis=0)
            shift *= 2
        cnt = jnp.minimum(pos + 1, w).astype(F32)
        mixed = s[POOL_HALO:, :] / cnt - u[:, cols]
        y = jnp.dot(mixed.astype(BF16), wpool_ref[g], preferred_element_type=F32)
        p_ref[0, :, cols] = (y * pscale_ref[:, cols]).astype(BF16)


def _inproj_call(x, g, wi, wvt, wpool, pscale):
    B, S, D = x.shape
    T = ROW_TILE_IN
    nblk = T // MOBA_BLOCK
    nb = S // MOBA_BLOCK
    const = lambda shape: pl.BlockSpec(shape, lambda b, t: (0,) * len(shape),
                                       pipeline_mode=pl.Buffered(1))
    return pl.pallas_call(
        _inproj_kernel,
        grid=(B, S // T),
        in_specs=[
            pl.BlockSpec((1, T, D), lambda b, t: (b, t, 0)),
            const((1, D)),
            const(wi.shape),
            const(wvt.shape),
            const(wpool.shape),
            const((1, POOL_WIDTH)),
        ],
        out_specs=[
            pl.BlockSpec((1, T, ATTN_WIDTH), lambda b, t: (b, t, 0)),
            pl.BlockSpec((1, T, ATTN_WIDTH), lambda b, t: (b, t, 0)),
            pl.BlockSpec((1, nblk, ATTN_WIDTH, MOBA_BLOCK), lambda b, t: (b, t, 0, 0)),
            pl.BlockSpec((1, nb, ATTN_WIDTH), lambda b, t: (b, 0, 0)),
            pl.BlockSpec((1, T, POOL_WIDTH), lambda b, t: (b, t, 0)),
        ],
        out_shape=[
            jax.ShapeDtypeStruct((B, S, ATTN_WIDTH), BF16),
            jax.ShapeDtypeStruct((B, S, ATTN_WIDTH), BF16),
            jax.ShapeDtypeStruct((B, nb, ATTN_WIDTH, MOBA_BLOCK), BF16),
            jax.ShapeDtypeStruct((B, nb, ATTN_WIDTH), F32),
            jax.ShapeDtypeStruct((B, S, POOL_WIDTH), BF16),
        ],
        scratch_shapes=[pltpu.VMEM((POOL_HALO, POOL_WIDTH), F32)],
        compiler_params=pltpu.CompilerParams(
            dimension_semantics=("arbitrary", "arbitrary"), vmem_limit_bytes=VMEM_LIMIT),
        name="inproj_pool",
    )(x, g, wi, wvt, wpool, pscale)


_F_QBLK, _F_JA, _F_JB, _F_A_VALID, _F_B_VALID, _F_FIRST, _F_LAST = range(7)


def _attn_schedule(nb):
    items = []
    for i in range(nb):
        steps = (i + 2) // 2
        for t in range(steps):
            jb = i - 2 * t - 1
            items.append((i, i - 2 * t, max(jb, 0), 1, int(jb >= 0), int(t == 0), int(t == steps - 1)))
    n_items = len(items)
    items += [(nb - 1, 0, 0, 0, 0, 0, 0)] * 2
    return n_items, np.asarray(items, np.int32).T.reshape(-1)


def _attn_kernel(n_items, slopes_ref, sched_ref, q_ref, k_ref, vt_ref, kmean_ref, bias_ref, o_ref,
                 qh_ref, shift_ref, s_ref, mx_ref, p_ref, alpha_ref, m_ref, acc_ref):
    step = pl.program_id(1)
    nb = kmean_ref.shape[1]
    seq = q_ref.shape[1]
    blk = MOBA_BLOCK
    neg_inf = -jnp.inf
    heads = range(HEADS_PER_STEP)
    groups = [tuple(range(g * HEADS_PER_GROUP, (g + 1) * HEADS_PER_GROUP)) for g in range(GROUPS_PER_STEP)]
    stride = n_items + 2

    def group_lanes(hh):
        g = hh // HEADS_PER_GROUP
        return slice(g * LANES, (g + 1) * LANES)

    def in_head(hh):
        lane = lax.broadcasted_iota(jnp.int32, (1, LANES), 1)
        lo = (hh % HEADS_PER_GROUP) * HEAD_DIM
        return ((lane >= lo) & (lane < lo + HEAD_DIM)).astype(BF16)

    def field(f, w):
        return sched_ref[f * stride + w]

    kmean = kmean_ref[0].astype(BF16)
    key_blk = lax.broadcasted_iota(jnp.int32, (nb, seq), 0)
    qry_blk = lax.broadcasted_iota(jnp.int32, (nb, seq), 1) // blk
    dist = ((qry_blk - key_blk) * blk).astype(F32)
    for hh in heads:
        q_h = q_ref[0, :, group_lanes(hh)] * in_head(hh)
        qh_ref[hh] = q_h
        gate = lax.dot_general(kmean[:, group_lanes(hh)], q_h, NT_DIMS,
                               preferred_element_type=F32)
        gate = jnp.where(key_blk < qry_blk, gate, neg_inf)
        sel = key_blk == qry_blk
        for _ in range(MOBA_TOPK):
            top = jnp.max(gate, axis=0, keepdims=True)
            first = jnp.min(jnp.where(gate == top, key_blk, nb), axis=0, keepdims=True)
            pick = (key_blk == first) & (top > neg_inf)
            sel = sel | pick
            gate = jnp.where(pick, neg_inf, gate)
        slope = slopes_ref[step * HEADS_PER_STEP + hh]
        shift = jnp.where(sel, slope * dist, jnp.inf)
        for i in range(nb):
            shift_ref[hh, i] = shift[:, i * blk:(i + 1) * blk]

    def score_dot(w, hh):
        i, ja, jb = field(_F_QBLK, w), field(_F_JA, w), field(_F_JB, w)
        kk = jnp.concatenate(
            [k_ref[0, pl.ds(pl.multiple_of(ja * blk, blk), blk), group_lanes(hh)],
             k_ref[0, pl.ds(pl.multiple_of(jb * blk, blk), blk), group_lanes(hh)]], axis=0)
        q_h = qh_ref[hh, pl.ds(pl.multiple_of(i * blk, blk), blk), :]
        return lax.dot_general(kk, q_h, NT_DIMS, preferred_element_type=F32)

    def stage_unit(st, w, slot, hh, u):
        table = field(_F_FIRST, w) if u == 0 else 0
        s_u = st[u * blk:(u + 1) * blk] + bias_ref[hh, table]
        s_ref[slot, hh, u] = s_u
        mx_ref[slot, hh, u] = jnp.max(s_u, axis=0, keepdims=True)

    def stage_scores(sts, w, slot, hhs):
        for hh in hhs:
            for u in range(2):
                stage_unit(sts[hh], w, slot, hh, u)

    def softmax_prep(w, slot, hh):
        i, ja, jb = field(_F_QBLK, w), field(_F_JA, w), field(_F_JB, w)
        a_valid, b_valid = field(_F_A_VALID, w) == 1, field(_F_B_VALID, w) == 1
        first = field(_F_FIRST, w) == 1
        sh_a = jnp.where(a_valid, shift_ref[hh, i, pl.ds(ja, 1), :], jnp.inf)
        sh_b = jnp.where(b_valid, shift_ref[hh, i, pl.ds(jb, 1), :], jnp.inf)
        m_old = jnp.where(first, neg_inf, m_ref[hh])
        m_new = jnp.maximum(m_old, jnp.maximum(mx_ref[slot, hh, 0] - sh_a,
                                               mx_ref[slot, hh, 1] - sh_b))
        m_ref[hh] = m_new
        alpha_ref[slot, hh] = jnp.exp2(m_old - m_new)
        return m_new + sh_a, m_new + sh_b

    def softmax_block(slot, hh, u, offset):
        p_ref[slot, hh, u * blk:(u + 1) * blk] = jnp.exp2(s_ref[slot, hh, u] - offset).astype(BF16)

    def softmax(w, slot):
        for hh in heads:
            offsets = softmax_prep(w, slot, hh)
            for u in range(2):
                softmax_block(slot, hh, u, offsets[u])

    ones_rows = jnp.ones((SUM_ROWS, 2 * blk), BF16)

    def pv_dot(w, slot, hh):
        ja, jb = field(_F_JA, w), field(_F_JB, w)
        rows = slice(hh * HEAD_DIM, (hh + 1) * HEAD_DIM)
        vts = jnp.concatenate([vt_ref[0, ja, rows, :], vt_ref[0, jb, rows, :]], axis=1)
        lhs = jnp.concatenate([vts, ones_rows], axis=0)
        return jnp.dot(lhs, p_ref[slot, hh], preferred_element_type=F32)

    def accumulate(pvs, w, slot):
        par = field(_F_QBLK, w) % ACC_BUFFERS
        for hh in heads:
            acc_ref[par, hh] = alpha_ref[slot, hh] * acc_ref[par, hh] + pvs[hh]

    def finalize(w):
        i = field(_F_QBLK, w)
        par = i % ACC_BUFFERS
        o_t = jnp.concatenate([acc_ref[par, hh, :HEAD_DIM] / acc_ref[par, hh, HEAD_DIM:HEAD_DIM + 1]
                               for hh in heads], axis=0)
        o_ref[0, pl.ds(pl.multiple_of(i * blk, blk), blk), :] = o_t.T.astype(o_ref.dtype)

    acc_ref[...] = jnp.zeros_like(acc_ref)
    m_ref[...] = jnp.zeros_like(m_ref)

    stage_scores([score_dot(0, hh) for hh in heads], 0, 0, heads)
    stage_scores([score_dot(1, hh) for hh in heads], 1, 1, heads)
    softmax(0, 0)

    def item(w, slot):
        sm = 1 - slot
        sts, pvs = {}, {}
        for g, hhs in enumerate(groups):
            for hh in hhs:
                sts[hh] = score_dot(w + 2, hh)
            offs = {hh: softmax_prep(w + 1, sm, hh) for hh in hhs}
            if g > 0:
                stage_scores(sts, w + 2, slot, groups[g - 1])
            for hh in hhs:
                for u in range(2):
                    softmax_block(sm, hh, u, offs[hh][u])
        for hh in heads:
            pvs[hh] = pv_dot(w, slot, hh)
        stage_scores(sts, w + 2, slot, groups[-1])
        accumulate(pvs, w, slot)

    def body(trip, carry):
        for j in range(ITEMS_PER_TRIP):
            item(ITEMS_PER_TRIP * trip + j, j % 2)
        for j in range(ITEMS_PER_TRIP):
            w = ITEMS_PER_TRIP * trip + j
            pl.when(field(_F_LAST, w) == 1)(functools.partial(finalize, w))
        return carry

    assert n_items % ITEMS_PER_TRIP == 0 and ITEMS_PER_TRIP % 2 == 0
    lax.fori_loop(0, n_items // ITEMS_PER_TRIP, body, 0)


def _attn_call(slopes, bias, q, k, vt, kmean):
    B, S, W = q.shape
    nb = S // MOBA_BLOCK
    step_lanes = GROUPS_PER_STEP * LANES
    n_steps = W // step_lanes
    hps = HEADS_PER_STEP
    n_items, sched = _attn_schedule(nb)
    grid_spec = pltpu.PrefetchScalarGridSpec(
        num_scalar_prefetch=2,
        grid=(B, n_steps),
        in_specs=[
            pl.BlockSpec((1, S, step_lanes), lambda b, p, *_: (b, 0, p)),
            pl.BlockSpec((1, S, step_lanes), lambda b, p, *_: (b, 0, p)),
            pl.BlockSpec((1, nb, step_lanes, MOBA_BLOCK), lambda b, p, *_: (b, 0, p, 0)),
            pl.BlockSpec((1, nb, step_lanes), lambda b, p, *_: (b, 0, p)),
            pl.BlockSpec((hps, 2, MOBA_BLOCK, MOBA_BLOCK), lambda b, p, *_: (p, 0, 0, 0)),
        ],
        out_specs=pl.BlockSpec((1, S, step_lanes), lambda b, p, *_: (b, 0, p)),
        scratch_shapes=[
            pltpu.VMEM((hps, S, LANES), BF16),
            pltpu.VMEM((hps, nb, nb, MOBA_BLOCK), F32),
            pltpu.VMEM((2, hps, 2, MOBA_BLOCK, MOBA_BLOCK), F32),
            pltpu.VMEM((2, hps, 2, 1, MOBA_BLOCK), F32),
            pltpu.VMEM((2, hps, 2 * MOBA_BLOCK, MOBA_BLOCK), BF16),
            pltpu.VMEM((2, hps, 1, MOBA_BLOCK), F32),
            pltpu.VMEM((hps, 1, MOBA_BLOCK), F32),
            pltpu.VMEM((ACC_BUFFERS, hps, HEAD_DIM + SUM_ROWS, MOBA_BLOCK), F32),
        ],
    )
    return pl.pallas_call(
        functools.partial(_attn_kernel, n_items),
        grid_spec=grid_spec,
        out_shape=jax.ShapeDtypeStruct((B, S, W), BF16),
        compiler_params=pltpu.CompilerParams(
            dimension_semantics=("arbitrary", "arbitrary"), vmem_limit_bytes=VMEM_LIMIT),
        name="moba_attn",
    )(slopes, jnp.asarray(sched), q, k, vt, kmean, bias)


def _ffn_kernel(x_ref, a_ref, p_ref, wo_ref, g2_ref, wup_ref, wdn_ref, g3_ref, o_ref):
    x1 = (x_ref[...]
          + jnp.dot(a_ref[...], wo_ref[:ATTN_WIDTH, :], preferred_element_type=F32)
          + jnp.dot(p_ref[...], wo_ref[ATTN_WIDTH:, :], preferred_element_type=F32))
    h = _rms_norm(x1, g2_ref[...]).astype(BF16)
    o_ref[...] = x1
    for c in range(wup_ref.shape[1] // FF_CHUNK):
        cols = slice(c * FF_CHUNK, (c + 1) * FF_CHUNK)
        up = jnp.dot(h, wup_ref[:, cols], preferred_element_type=F32)
        act = jnp.square(jnp.maximum(up, 0.0)).astype(BF16)
        o_ref[...] += jnp.dot(act, wdn_ref[cols, :], preferred_element_type=F32)
    o_ref[...] = _rms_norm(o_ref[...], g3_ref[...])


def _ffn_call(x, a, p, wo, g2, wup, wdn, g3):
    N, D = x.shape
    T = ROW_TILE_FFN
    const = lambda shape: pl.BlockSpec(shape, lambda t: (0,) * len(shape),
                                       pipeline_mode=pl.Buffered(1))
    return pl.pallas_call(
        _ffn_kernel,
        grid=(N // T,),
        in_specs=[
            pl.BlockSpec((T, D), lambda t: (t, 0)),
            pl.BlockSpec((T, ATTN_WIDTH), lambda t: (t, 0)),
            pl.BlockSpec((T, POOL_WIDTH), lambda t: (t, 0)),
            const(wo.shape), const((1, D)),
            const(wup.shape), const(wdn.shape), const((1, D)),
        ],
        out_specs=pl.BlockSpec((T, D), lambda t: (t, 0)),
        out_shape=jax.ShapeDtypeStruct((N, D), F32),
        compiler_params=pltpu.CompilerParams(
            dimension_semantics=("arbitrary",), vmem_limit_bytes=VMEM_LIMIT),
        name="outproj_ffn",
    )(x, a, p, wo, g2, wup, wdn, g3)


def _alibi_constants():
    slopes = (2.0 ** (-8.0 * np.arange(1, N_HEADS + 1) / N_HEADS) * LOG2E).astype(np.float32)
    key = np.arange(MOBA_BLOCK, dtype=np.float32)[:, None]
    qry = np.arange(MOBA_BLOCK, dtype=np.float32)[None, :]
    past = -slopes[:, None, None] * (qry - key)[None]
    own = np.where((key <= qry)[None], past, -np.inf).astype(np.float32)
    return jnp.asarray(slopes), jnp.asarray(np.stack([past, own], axis=1))


def kernel(x, norm_mix, w_in, w_pool, pool_scale, w_out, norm_mlp, w_up, w_down, norm_final):
    B, S, D = x.shape
    assert w_in.shape[0] == 1, "single trunk layer (the final norm is fused into the FFN call)"
    assert S % ROW_TILE_IN == 0 and (B * S) % ROW_TILE_FFN == 0 and S % MOBA_BLOCK == 0
    assert w_in.shape[2] == 3 * ATTN_WIDTH + POOL_WIDTH and D == ATTN_WIDTH + POOL_WIDTH
    wi = w_in[0].astype(BF16)
    wvt = wi[:, 2 * ATTN_WIDTH:3 * ATTN_WIDTH].T
    q, k, vt, kmean, p = _inproj_call(
        x, norm_mix[0][None, :], wi, wvt, w_pool[0].astype(BF16), pool_scale[0][None, :])
    slopes, bias = _alibi_constants()
    a = _attn_call(slopes, bias, q, k, vt, kmean)
    wo = w_out[0].astype(BF16)
    y = _ffn_call(x.reshape(B * S, D), a.reshape(B * S, ATTN_WIDTH),
                  p.reshape(B * S, POOL_WIDTH), wo,
                  norm_mlp[0][None, :], w_up[0].astype(BF16), w_down[0].astype(BF16),
                  norm_final[None, :])
    return y.reshape(B, S, D)
```

```python
import functools

import jax
import jax.numpy as jnp
import numpy as np
from jax import lax
from jax.experimental import pallas as pl
from jax.experimental.pallas import tpu as pltpu

F32 = jnp.float32
BF16 = jnp.bfloat16

N_HEADS = 8
HEAD_DIM = 64
ATTN_WIDTH = N_HEADS * HEAD_DIM
POOL_WINDOWS = (2, 4, 8, 16)
POOL_GROUP = 128
POOL_WIDTH = POOL_GROUP * len(POOL_WINDOWS)
MOBA_BLOCK = 256
MOBA_TOPK = 3
EPS = 1e-6
LOG2E = 1.4426950408889634
QK_SCALE = HEAD_DIM ** -0.5 * LOG2E
SUM_ROWS = 16
POOL_HALO = 16
LANES = 128
HEADS_PER_GROUP = LANES // HEAD_DIM
GROUPS_PER_STEP = 2
HEADS_PER_STEP = GROUPS_PER_STEP * HEADS_PER_GROUP
ITEMS_PER_TRIP = 2
ACC_BUFFERS = 2

ROW_TILE_IN = 1024
ROW_SUBTILE_IN = 512
ROW_TILE_FFN = 1024
FF_CHUNK = 1024
VMEM_LIMIT = 56 * 1024 * 1024

NT_DIMS = (((1,), (1,)), ((), ()))


def _rms_norm(x, g):
    ms = jnp.mean(x * x, axis=-1, keepdims=True)
    return x * lax.rsqrt(ms + EPS) * g


def _inproj_kernel(x_ref, g_ref, wi_ref, wvt_ref, wpool_ref, pscale_ref,
                   q_ref, k_ref, vt_ref, kmean_ref, p_ref, halo_ref):
    t = pl.program_id(1)
    rows = x_ref.shape[1]
    sub = ROW_SUBTILE_IN
    nblk = sub // MOBA_BLOCK
    u_cols = slice(3 * ATTN_WIDTH, 3 * ATTN_WIDTH + POOL_WIDTH)

    @pl.when(t == 0)
    def _():
        halo_ref[...] = jnp.zeros_like(halo_ref)

    def project(i):
        r0 = i * sub
        h = _rms_norm(x_ref[0, r0:r0 + sub, :], g_ref[...]).astype(BF16)
        qk = jnp.dot(h, wi_ref[:, :2 * ATTN_WIDTH], preferred_element_type=F32)
        q_ref[0, r0:r0 + sub, :] = (qk[:, :ATTN_WIDTH] * QK_SCALE).astype(BF16)
        kf = qk[:, ATTN_WIDTH:]
        k_ref[0, r0:r0 + sub, :] = kf.astype(BF16)
        for b in range(nblk):
            blk_row = (t * (rows // sub) + i) * nblk + b
            kmean_ref[0, pl.ds(blk_row, 1), :] = (
                jnp.sum(kf[b * MOBA_BLOCK:(b + 1) * MOBA_BLOCK], axis=0, keepdims=True)
                * (1.0 / MOBA_BLOCK))
        vt = lax.dot_general(wvt_ref[...], h, NT_DIMS, preferred_element_type=F32)
        for b in range(nblk):
            vt_ref[0, i * nblk + b] = vt[:, b * MOBA_BLOCK:(b + 1) * MOBA_BLOCK].astype(BF16)
        return jnp.dot(h, wi_ref[:, u_cols], preferred_element_type=F32)

    def pool(i, u, halo):
        r0 = i * sub
        ext = jnp.concatenate([halo, u], axis=0)
        pos = t * rows + r0 + lax.broadcasted_iota(jnp.int32, (sub, POOL_GROUP), 0)
        for g, w in enumerate(POOL_WINDOWS):
            cols = slice(g * POOL_GROUP, (g + 1) * POOL_GROUP)
            s = ext[:, cols]
            shift = 1
            while shift < w:
                s = s + pltpu.roll(s, shift, axis=0)
                shift *= 2
            cnt = jnp.minimum(pos + 1, w).astype(F32)
            mixed = s[POOL_HALO:, :] / cnt - u[:, cols]
            y = jnp.dot(mixed.astype(BF16), wpool_ref[g], preferred_element_type=F32)
            p_ref[0, r0:r0 + sub, cols] = (y * pscale_ref[:, cols]).astype(BF16)
        return u[sub - POOL_HALO:, :]

    n_sub = rows // sub
    halo = halo_ref[...]
    u_prev = project(0)
    for i in range(1, n_sub):
        u_next = project(i)
        halo = pool(i - 1, u_prev, halo)
        u_prev = u_next
    halo_ref[...] = pool(n_sub - 1, u_prev, halo)


def _inproj_call(x, g, wi, wvt, wpool, pscale):
    B, S, D = x.shape
    T = ROW_TILE_IN
    nblk = T // MOBA_BLOCK
    nb = S // MOBA_BLOCK
    const = lambda shape: pl.BlockSpec(shape, lambda b, t: (0,) * len(shape),
                                       pipeline_mode=pl.Buffered(1))
    return pl.pallas_call(
        _inproj_kernel,
        grid=(B, S // T),
        in_specs=[
            pl.BlockSpec((1, T, D), lambda b, t: (b, t, 0)),
            const((1, D)),
            const(wi.shape),
            const(wvt.shape),
            const(wpool.shape),
            const((1, POOL_WIDTH)),
        ],
        out_specs=[
            pl.BlockSpec((1, T, ATTN_WIDTH), lambda b, t: (b, t, 0)),
            pl.BlockSpec((1, T, ATTN_WIDTH), lambda b, t: (b, t, 0)),
            pl.BlockSpec((1, nblk, ATTN_WIDTH, MOBA_BLOCK), lambda b, t: (b, t, 0, 0)),
            pl.BlockSpec((1, nb, ATTN_WIDTH), lambda b, t: (b, 0, 0)),
            pl.BlockSpec((1, T, POOL_WIDTH), lambda b, t: (b, t, 0)),
        ],
        out_shape=[
            jax.ShapeDtypeStruct((B, S, ATTN_WIDTH), BF16),
            jax.ShapeDtypeStruct((B, S, ATTN_WIDTH), BF16),
            jax.ShapeDtypeStruct((B, nb, ATTN_WIDTH, MOBA_BLOCK), BF16),
            jax.ShapeDtypeStruct((B, nb, ATTN_WIDTH), F32),
            jax.ShapeDtypeStruct((B, S, POOL_WIDTH), BF16),
        ],
        scratch_shapes=[pltpu.VMEM((POOL_HALO, POOL_WIDTH), F32)],
        compiler_params=pltpu.CompilerParams(
            dimension_semantics=("arbitrary", "arbitrary"), vmem_limit_bytes=VMEM_LIMIT),
        name="inproj_pool",
    )(x, g, wi, wvt, wpool, pscale)


_F_QBLK, _F_JA, _F_JB, _F_A_VALID, _F_B_VALID, _F_FIRST, _F_LAST = range(7)


def _attn_schedule(nb):
    items = []
    for i in range(nb):
        steps = (i + 2) // 2
        for t in range(steps):
            jb = i - 2 * t - 1
            items.append((i, i - 2 * t, max(jb, 0), 1, int(jb >= 0), int(t == 0), int(t == steps - 1)))
    n_items = len(items)
    items += [(nb - 1, 0, 0, 0, 0, 0, 0)] * 2
    return n_items, np.asarray(items, np.int32).T.reshape(-1)


def _attn_kernel(n_items, slopes_ref, sched_ref, q_ref, k_ref, vt_ref, kmean_ref, bias_ref, o_ref,
                 qh_ref, shift_ref, s_ref, mx_ref, p_ref, alpha_ref, m_ref, acc_ref):
    step = pl.program_id(1)
    nb = kmean_ref.shape[1]
    seq = q_ref.shape[1]
    blk = MOBA_BLOCK
    neg_inf = -jnp.inf
    heads = range(HEADS_PER_STEP)
    groups = [tuple(range(g * HEADS_PER_GROUP, (g + 1) * HEADS_PER_GROUP)) for g in range(GROUPS_PER_STEP)]
    stride = n_items + 2

    def group_lanes(hh):
        g = hh // HEADS_PER_GROUP
        return slice(g * LANES, (g + 1) * LANES)

    def in_head(hh):
        lane = lax.broadcasted_iota(jnp.int32, (1, LANES), 1)
        lo = (hh % HEADS_PER_GROUP) * HEAD_DIM
        return ((lane >= lo) & (lane < lo + HEAD_DIM)).astype(BF16)

    def field(f, w):
        return sched_ref[f * stride + w]

    kmean = kmean_ref[0].astype(BF16)
    key_blk = lax.broadcasted_iota(jnp.int32, (nb, seq), 0)
    qry_blk = lax.broadcasted_iota(jnp.int32, (nb, seq), 1) // blk
    dist = ((qry_blk - key_blk) * blk).astype(F32)
    for hh in heads:
        q_h = q_ref[0, :, group_lanes(hh)] * in_head(hh)
        qh_ref[hh] = q_h
        gate = lax.dot_general(kmean[:, group_lanes(hh)], q_h, NT_DIMS,
                               preferred_element_type=F32)
        gate = jnp.where(key_blk < qry_blk, gate, neg_inf)
        sel = key_blk == qry_blk
        for _ in range(MOBA_TOPK):
            top = jnp.max(gate, axis=0, keepdims=True)
            first = jnp.min(jnp.where(gate == top, key_blk, nb), axis=0, keepdims=True)
            pick = (key_blk == first) & (top > neg_inf)
            sel = sel | pick
            gate = jnp.where(pick, neg_inf, gate)
        slope = slopes_ref[step * HEADS_PER_STEP + hh]
        shift = jnp.where(sel, slope * dist, jnp.inf)
        for i in range(nb):
            shift_ref[hh, i] = shift[:, i * blk:(i + 1) * blk]

    def score_dot(w, hh):
        i, ja, jb = field(_F_QBLK, w), field(_F_JA, w), field(_F_JB, w)
        kk = jnp.concatenate(
            [k_ref[0, pl.ds(pl.multiple_of(ja * blk, blk), blk), group_lanes(hh)],
             k_ref[0, pl.ds(pl.multiple_of(jb * blk, blk), blk), group_lanes(hh)]], axis=0)
        q_h = qh_ref[hh, pl.ds(pl.multiple_of(i * blk, blk), blk), :]
        return lax.dot_general(kk, q_h, NT_DIMS, preferred_element_type=F32)

    def stage_unit(st, w, slot, hh, u):
        table = field(_F_FIRST, w) if u == 0 else 0
        s_u = st[u * blk:(u + 1) * blk] + bias_ref[hh, table]
        s_ref[slot, hh, u] = s_u
        mx_ref[slot, hh, u] = jnp.max(s_u, axis=0, keepdims=True)

    def stage_scores(sts, w, slot, hhs):
        for hh in hhs:
            for u in range(2):
                stage_unit(sts[hh], w, slot, hh, u)

    def softmax_prep(w, slot, hh):
        i, ja, jb = field(_F_QBLK, w), field(_F_JA, w), field(_F_JB, w)
        a_valid, b_valid = field(_F_A_VALID, w) == 1, field(_F_B_VALID, w) == 1
        first = field(_F_FIRST, w) == 1
        sh_a = jnp.where(a_valid, shift_ref[hh, i, pl.ds(ja, 1), :], jnp.inf)
        sh_b = jnp.where(b_valid, shift_ref[hh, i, pl.ds(jb, 1), :], jnp.inf)
        m_old = jnp.where(first, neg_inf, m_ref[hh])
        m_new = jnp.maximum(m_old, jnp.maximum(mx_ref[slot, hh, 0] - sh_a,
                                               mx_ref[slot, hh, 1] - sh_b))
        m_ref[hh] = m_new
        alpha_ref[slot, hh] = jnp.exp2(m_old - m_new)
        return m_new + sh_a, m_new + sh_b

    def softmax_block(slot, hh, u, offset):
        p_ref[slot, hh, u * blk:(u + 1) * blk] = jnp.exp2(s_ref[slot, hh, u] - offset).astype(BF16)

    def softmax(w, slot):
        for hh in heads:
            offsets = softmax_prep(w, slot, hh)
            for u in range(2):
                softmax_block(slot, hh, u, offsets[u])

    ones_rows = jnp.ones((SUM_ROWS, 2 * blk), BF16)

    def pv_dot(w, slot, hh):
        ja, jb = field(_F_JA, w), field(_F_JB, w)
        rows = slice(hh * HEAD_DIM, (hh + 1) * HEAD_DIM)
        vts = jnp.concatenate([vt_ref[0, ja, rows, :], vt_ref[0, jb, rows, :]], axis=1)
        lhs = jnp.concatenate([vts, ones_rows], axis=0)
        return jnp.dot(lhs, p_ref[slot, hh], preferred_element_type=F32)

    def accumulate(pvs, w, slot):
        par = field(_F_QBLK, w) % ACC_BUFFERS
        for hh in heads:
            acc_ref[par, hh] = alpha_ref[slot, hh] * acc_ref[par, hh] + pvs[hh]

    def finalize(w):
        i = field(_F_QBLK, w)
        par = i % ACC_BUFFERS
        o_t = jnp.concatenate([acc_ref[par, hh, :HEAD_DIM] / acc_ref[par, hh, HEAD_DIM:HEAD_DIM + 1]
                               for hh in heads], axis=0)
        o_ref[0, pl.ds(pl.multiple_of(i * blk, blk), blk), :] = o_t.T.astype(o_ref.dtype)

    acc_ref[...] = jnp.zeros_like(acc_ref)
    m_ref[...] = jnp.zeros_like(m_ref)

    stage_scores([score_dot(0, hh) for hh in heads], 0, 0, heads)
    stage_scores([score_dot(1, hh) for hh in heads], 1, 1, heads)
    softmax(0, 0)

    def item(w, slot):
        sm = 1 - slot
        sts, pvs = {}, {}
        for g, hhs in enumerate(groups):
            for hh in hhs:
                sts[hh] = score_dot(w + 2, hh)
            offs = {hh: softmax_prep(w + 1, sm, hh) for hh in hhs}
            if g > 0:
                stage_scores(sts, w + 2, slot, groups[g - 1])
            for hh in hhs:
                for u in range(2):
                    softmax_block(sm, hh, u, offs[hh][u])
        for hh in heads:
            pvs[hh] = pv_dot(w, slot, hh)
        stage_scores(sts, w + 2, slot, groups[-1])
        accumulate(pvs, w, slot)

    def body(trip, carry):
        for j in range(ITEMS_PER_TRIP):
            item(ITEMS_PER_TRIP * trip + j, j % 2)
        for j in range(ITEMS_PER_TRIP):
            w = ITEMS_PER_TRIP * trip + j
            pl.when(field(_F_LAST, w) == 1)(functools.partial(finalize, w))
        return carry

    assert n_items % ITEMS_PER_TRIP == 0 and ITEMS_PER_TRIP % 2 == 0
    lax.fori_loop(0, n_items // ITEMS_PER_TRIP, body, 0)


def _attn_call(slopes, bias, q, k, vt, kmean):
    B, S, W = q.shape
    nb = S // MOBA_BLOCK
    step_lanes = GROUPS_PER_STEP * LANES
    n_steps = W // step_lanes
    hps = HEADS_PER_STEP
    n_items, sched = _attn_schedule(nb)
    grid_spec = pltpu.PrefetchScalarGridSpec(
        num_scalar_prefetch=2,
        grid=(B, n_steps),
        in_specs=[
            pl.BlockSpec((1, S, step_lanes), lambda b, p, *_: (b, 0, p)),
            pl.BlockSpec((1, S, step_lanes), lambda b, p, *_: (b, 0, p)),
            pl.BlockSpec((1, nb, step_lanes, MOBA_BLOCK), lambda b, p, *_: (b, 0, p, 0)),
            pl.BlockSpec((1, nb, step_lanes), lambda b, p, *_: (b, 0, p)),
            pl.BlockSpec((hps, 2, MOBA_BLOCK, MOBA_BLOCK), lambda b, p, *_: (p, 0, 0, 0)),
        ],
        out_specs=pl.BlockSpec((1, S, step_lanes), lambda b, p, *_: (b, 0, p)),
        scratch_shapes=[
            pltpu.VMEM((hps, S, LANES), BF16),
            pltpu.VMEM((hps, nb, nb, MOBA_BLOCK), F32),
            pltpu.VMEM((2, hps, 2, MOBA_BLOCK, MOBA_BLOCK), F32),
            pltpu.VMEM((2, hps, 2, 1, MOBA_BLOCK), F32),
            pltpu.VMEM((2, hps, 2 * MOBA_BLOCK, MOBA_BLOCK), BF16),
            pltpu.VMEM((2, hps, 1, MOBA_BLOCK), F32),
            pltpu.VMEM((hps, 1, MOBA_BLOCK), F32),
            pltpu.VMEM((ACC_BUFFERS, hps, HEAD_DIM + SUM_ROWS, MOBA_BLOCK), F32),
        ],
    )
    return pl.pallas_call(
        functools.partial(_attn_kernel, n_items),
        grid_spec=grid_spec,
        out_shape=jax.ShapeDtypeStruct((B, S, W), BF16),
        compiler_params=pltpu.CompilerParams(
            dimension_semantics=("arbitrary", "arbitrary"), vmem_limit_bytes=VMEM_LIMIT),
        name="moba_attn",
    )(slopes, jnp.asarray(sched), q, k, vt, kmean, bias)


def _ffn_kernel(x_ref, a_ref, p_ref, wo_ref, g2_ref, wup_ref, wdn_ref, g3_ref, o_ref):
    x1 = (x_ref[...]
          + jnp.dot(a_ref[...], wo_ref[:ATTN_WIDTH, :], preferred_element_type=F32)
          + jnp.dot(p_ref[...], wo_ref[ATTN_WIDTH:, :], preferred_element_type=F32))
    h = _rms_norm(x1, g2_ref[...]).astype(BF16)
    o_ref[...] = x1
    for c in range(wup_ref.shape[1] // FF_CHUNK):
        cols = slice(c * FF_CHUNK, (c + 1) * FF_CHUNK)
        up = jnp.dot(h, wup_ref[:, cols], preferred_element_type=F32)
        act = jnp.square(jnp.maximum(up, 0.0)).astype(BF16)
        o_ref[...] += jnp.dot(act, wdn_ref[cols, :], preferred_element_type=F32)
    o_ref[...] = _rms_norm(o_ref[...], g3_ref[...])


def _ffn_call(x, a, p, wo, g2, wup, wdn, g3):
    N, D = x.shape
    T = ROW_TILE_FFN
    const = lambda shape: pl.BlockSpec(shape, lambda t: (0,) * len(shape),
                                       pipeline_mode=pl.Buffered(1))
    return pl.pallas_call(
        _ffn_kernel,
        grid=(N // T,),
        in_specs=[
            pl.BlockSpec((T, D), lambda t: (t, 0)),
            pl.BlockSpec((T, ATTN_WIDTH), lambda t: (t, 0)),
            pl.BlockSpec((T, POOL_WIDTH), lambda t: (t, 0)),
            const(wo.shape), const((1, D)),
            const(wup.shape), const(wdn.shape), const((1, D)),
        ],
        out_specs=pl.BlockSpec((T, D), lambda t: (t, 0)),
        out_shape=jax.ShapeDtypeStruct((N, D), F32),
        compiler_params=pltpu.CompilerParams(
            dimension_semantics=("arbitrary",), vmem_limit_bytes=VMEM_LIMIT),
        name="outproj_ffn",
    )(x, a, p, wo, g2, wup, wdn, g3)


def _alibi_constants():
    slopes = (2.0 ** (-8.0 * np.arange(1, N_HEADS + 1) / N_HEADS) * LOG2E).astype(np.float32)
    key = np.arange(MOBA_BLOCK, dtype=np.float32)[:, None]
    qry = np.arange(MOBA_BLOCK, dtype=np.float32)[None, :]
    past = -slopes[:, None, None] * (qry - key)[None]
    own = np.where((key <= qry)[None], past, -np.inf).astype(np.float32)
    return jnp.asarray(slopes), jnp.asarray(np.stack([past, own], axis=1))


def kernel(x, norm_mix, w_in, w_pool, pool_scale, w_out, norm_mlp, w_up, w_down, norm_final):
    B, S, D = x.shape
    assert w_in.shape[0] == 1, "single trunk layer (the final norm is fused into the FFN call)"
    assert S % ROW_TILE_IN == 0 and (B * S) % ROW_TILE_FFN == 0 and S % MOBA_BLOCK == 0
    assert w_in.shape[2] == 3 * ATTN_WIDTH + POOL_WIDTH and D == ATTN_WIDTH + POOL_WIDTH
    wi = w_in[0].astype(BF16)
    wvt = wi[:, 2 * ATTN_WIDTH:3 * ATTN_WIDTH].T
    q, k, vt, kmean, p = _inproj_call(
        x, norm_mix[0][None, :], wi, wvt, w_pool[0].astype(BF16), pool_scale[0][None, :])
    slopes, bias = _alibi_constants()
    a = _attn_call(slopes, bias, q, k, vt, kmean)
    wo = w_out[0].astype(BF16)
    y = _ffn_call(x.reshape(B * S, D), a.reshape(B * S, ATTN_WIDTH),
                  p.reshape(B * S, POOL_WIDTH), wo,
                  norm_mlp[0][None, :], w_up[0].astype(BF16), w_down[0].astype(BF16),
                  norm_final[None, :])
    return y.reshape(B, S, D)
```

```python
import functools

import jax
import jax.numpy as jnp
import numpy as np
from jax import lax
from jax.experimental import pallas as pl
from jax.experimental.pallas import tpu as pltpu

F32 = jnp.float32
BF16 = jnp.bfloat16

N_HEADS = 8
HEAD_DIM = 64
ATTN_WIDTH = N_HEADS * HEAD_DIM
POOL_WINDOWS = (2, 4, 8, 16)
POOL_GROUP = 128
POOL_WIDTH = POOL_GROUP * len(POOL_WINDOWS)
MOBA_BLOCK = 256
MOBA_TOPK = 3
EPS = 1e-6
LOG2E = 1.4426950408889634
QK_SCALE = HEAD_DIM ** -0.5 * LOG2E
SUM_ROWS = 16
POOL_HALO = 16
LANES = 128
HEADS_PER_GROUP = LANES // HEAD_DIM
GROUPS_PER_STEP = 2
HEADS_PER_STEP = GROUPS_PER_STEP * HEADS_PER_GROUP
ITEMS_PER_TRIP = 2
ACC_BUFFERS = 2

ROW_TILE_IN = 2048
ROW_SUBTILE_IN = 512
ROW_TILE_FFN = 1024
ROW_SUBTILE_FFN = 512
FF_CHUNK = 1024
VMEM_LIMIT = 56 * 1024 * 1024

NT_DIMS = (((1,), (1,)), ((), ()))


def _rms_norm(x, g):
    ms = jnp.mean(x * x, axis=-1, keepdims=True)
    return x * lax.rsqrt(ms + EPS) * g


def _inproj_kernel(x_ref, g_ref, wi_ref, wqvt_ref, wpool_ref, pscale_ref,
                   qt_ref, k_ref, vt_ref, kmean_ref, p_ref, halo_ref):
    t = pl.program_id(1)
    rows = x_ref.shape[1]
    sub = ROW_SUBTILE_IN
    nblk = sub // MOBA_BLOCK
    u_cols = slice(3 * ATTN_WIDTH, 3 * ATTN_WIDTH + POOL_WIDTH)

    @pl.when(t == 0)
    def _():
        halo_ref[...] = jnp.zeros_like(halo_ref)

    def project(i):
        r0 = i * sub
        h = _rms_norm(x_ref[0, r0:r0 + sub, :], g_ref[...]).astype(BF16)
        kf = jnp.dot(h, wi_ref[:, ATTN_WIDTH:2 * ATTN_WIDTH], preferred_element_type=F32)
        k_ref[0, r0:r0 + sub, :] = kf.astype(BF16)
        for b in range(nblk):
            blk_row = (t * (rows // sub) + i) * nblk + b
            kmean_ref[0, pl.ds(blk_row, 1), :] = (
                jnp.sum(kf[b * MOBA_BLOCK:(b + 1) * MOBA_BLOCK], axis=0, keepdims=True)
                * (1.0 / MOBA_BLOCK))
        qvt = lax.dot_general(wqvt_ref[...], h, NT_DIMS, preferred_element_type=F32)
        for b in range(nblk):
            cols = slice(b * MOBA_BLOCK, (b + 1) * MOBA_BLOCK)
            qt_ref[0, i * nblk + b] = (qvt[:ATTN_WIDTH, cols] * QK_SCALE).astype(BF16)
            vt_ref[0, i * nblk + b] = qvt[ATTN_WIDTH:, cols].astype(BF16)
        return jnp.dot(h, wi_ref[:, u_cols], preferred_element_type=F32)

    def pool(i, u, halo):
        r0 = i * sub
        ext = jnp.concatenate([halo, u], axis=0)
        pos = t * rows + r0 + lax.broadcasted_iota(jnp.int32, (sub, POOL_GROUP), 0)
        for g, w in enumerate(POOL_WINDOWS):
            cols = slice(g * POOL_GROUP, (g + 1) * POOL_GROUP)
            s = ext[:, cols]
            shift = 1
            while shift < w:
                s = s + pltpu.roll(s, shift, axis=0)
                shift *= 2
            cnt = jnp.minimum(pos + 1, w).astype(F32)
            mixed = s[POOL_HALO:, :] / cnt - u[:, cols]
            y = jnp.dot(mixed.astype(BF16), wpool_ref[g], preferred_element_type=F32)
            p_ref[0, r0:r0 + sub, cols] = (y * pscale_ref[:, cols]).astype(BF16)
        return u[sub - POOL_HALO:, :]

    n_sub = rows // sub
    halo = halo_ref[...]
    u_prev = project(0)
    for i in range(1, n_sub):
        u_next = project(i)
        halo = pool(i - 1, u_prev, halo)
        u_prev = u_next
    halo_ref[...] = pool(n_sub - 1, u_prev, halo)


def _inproj_call(x, g, wi, wqvt, wpool, pscale):
    B, S, D = x.shape
    T = ROW_TILE_IN
    nblk = T // MOBA_BLOCK
    nb = S // MOBA_BLOCK
    const = lambda shape: pl.BlockSpec(shape, lambda b, t: (0,) * len(shape),
                                       pipeline_mode=pl.Buffered(1))
    return pl.pallas_call(
        _inproj_kernel,
        grid=(B, S // T),
        in_specs=[
            pl.BlockSpec((1, T, D), lambda b, t: (b, t, 0)),
            const((1, D)),
            const(wi.shape),
            const(wqvt.shape),
            const(wpool.shape),
            const((1, POOL_WIDTH)),
        ],
        out_specs=[
            pl.BlockSpec((1, nblk, ATTN_WIDTH, MOBA_BLOCK), lambda b, t: (b, t, 0, 0)),
            pl.BlockSpec((1, T, ATTN_WIDTH), lambda b, t: (b, t, 0)),
            pl.BlockSpec((1, nblk, ATTN_WIDTH, MOBA_BLOCK), lambda b, t: (b, t, 0, 0)),
            pl.BlockSpec((1, nb, ATTN_WIDTH), lambda b, t: (b, 0, 0)),
            pl.BlockSpec((1, T, POOL_WIDTH), lambda b, t: (b, t, 0)),
        ],
        out_shape=[
            jax.ShapeDtypeStruct((B, nb, ATTN_WIDTH, MOBA_BLOCK), BF16),
            jax.ShapeDtypeStruct((B, S, ATTN_WIDTH), BF16),
            jax.ShapeDtypeStruct((B, nb, ATTN_WIDTH, MOBA_BLOCK), BF16),
            jax.ShapeDtypeStruct((B, nb, ATTN_WIDTH), F32),
            jax.ShapeDtypeStruct((B, S, POOL_WIDTH), BF16),
        ],
        scratch_shapes=[pltpu.VMEM((POOL_HALO, POOL_WIDTH), F32)],
        compiler_params=pltpu.CompilerParams(
            dimension_semantics=("arbitrary", "arbitrary"), vmem_limit_bytes=VMEM_LIMIT),
        name="inproj_pool",
    )(x, g, wi, wqvt, wpool, pscale)


_F_QBLK, _F_JA, _F_JB, _F_A_VALID, _F_B_VALID, _F_FIRST, _F_LAST = range(7)


def _attn_schedule(nb):
    items = []
    for i in range(nb):
        steps = (i + 2) // 2
        for t in range(steps):
            jb = i - 2 * t - 1
            items.append((i, i - 2 * t, max(jb, 0), 1, int(jb >= 0), int(t == 0), int(t == steps - 1)))
    n_items = len(items)
    items += [(nb - 1, 0, 0, 0, 0, 0, 0)] * 2
    return n_items, np.asarray(items, np.int32).T.reshape(-1)


def _attn_kernel(n_items, slopes_ref, sched_ref, qt_ref, k_ref, vt_ref, kmean_ref, bias_ref, o_ref,
                 shift_ref, s_ref, mx_ref, p_ref, alpha_ref, m_ref, acc_ref):
    step = pl.program_id(1)
    nb = kmean_ref.shape[1]
    seq = k_ref.shape[1]
    blk = MOBA_BLOCK
    neg_inf = -jnp.inf
    heads = range(HEADS_PER_STEP)
    groups = [tuple(range(g * HEADS_PER_GROUP, (g + 1) * HEADS_PER_GROUP)) for g in range(GROUPS_PER_STEP)]
    stride = n_items + 2

    def group_lanes(hh):
        g = hh // HEADS_PER_GROUP
        return slice(g * LANES, (g + 1) * LANES)

    def head_weights(hh, rows):
        pad = jnp.zeros_like(rows)
        return jnp.concatenate([rows, pad] if hh % HEADS_PER_GROUP == 0 else [pad, rows], axis=0)

    def head_rows(hh):
        return slice(hh * HEAD_DIM, (hh + 1) * HEAD_DIM)

    def field(f, w):
        return sched_ref[f * stride + w]

    kmean = kmean_ref[0].astype(BF16)
    key_blk = lax.broadcasted_iota(jnp.int32, (nb, seq), 0)
    qry_blk = lax.broadcasted_iota(jnp.int32, (nb, seq), 1) // blk
    dist = ((qry_blk - key_blk) * blk).astype(F32)
    for hh in heads:
        qt_all = jnp.concatenate([qt_ref[0, i, head_rows(hh), :] for i in range(nb)], axis=1)
        gate = jnp.dot(kmean[:, group_lanes(hh)], head_weights(hh, qt_all),
                       preferred_element_type=F32)
        gate = jnp.where(key_blk < qry_blk, gate, neg_inf)
        sel = key_blk == qry_blk
        for _ in range(MOBA_TOPK):
            top = jnp.max(gate, axis=0, keepdims=True)
            first = jnp.min(jnp.where(gate == top, key_blk, nb), axis=0, keepdims=True)
            pick = (key_blk == first) & (top > neg_inf)
            sel = sel | pick
            gate = jnp.where(pick, neg_inf, gate)
        slope = slopes_ref[step * HEADS_PER_STEP + hh]
        shift = jnp.where(sel, slope * dist, jnp.inf)
        for i in range(nb):
            shift_ref[hh, i] = shift[:, i * blk:(i + 1) * blk]

    def score_dot(w, hh):
        i, ja, jb = field(_F_QBLK, w), field(_F_JA, w), field(_F_JB, w)
        kk = jnp.concatenate(
            [k_ref[0, pl.ds(pl.multiple_of(ja * blk, blk), blk), group_lanes(hh)],
             k_ref[0, pl.ds(pl.multiple_of(jb * blk, blk), blk), group_lanes(hh)]], axis=0)
        return jnp.dot(kk, head_weights(hh, qt_ref[0, i, head_rows(hh), :]),
                       preferred_element_type=F32)

    def stage_unit(st, w, slot, hh, u):
        table = field(_F_FIRST, w) if u == 0 else 0
        s_u = st[u * blk:(u + 1) * blk] + bias_ref[hh, table]
        s_ref[slot, hh, u] = s_u
        mx_ref[slot, hh, u] = jnp.max(s_u, axis=0, keepdims=True)

    def stage_scores(sts, w, slot, hhs):
        for hh in hhs:
            for u in range(2):
                stage_unit(sts[hh], w, slot, hh, u)

    def softmax_prep(w, slot, hh):
        i, ja, jb = field(_F_QBLK, w), field(_F_JA, w), field(_F_JB, w)
        a_valid, b_valid = field(_F_A_VALID, w) == 1, field(_F_B_VALID, w) == 1
        first = field(_F_FIRST, w) == 1
        sh_a = jnp.where(a_valid, shift_ref[hh, i, pl.ds(ja, 1), :], jnp.inf)
        sh_b = jnp.where(b_valid, shift_ref[hh, i, pl.ds(jb, 1), :], jnp.inf)
        m_old = jnp.where(first, neg_inf, m_ref[hh])
        m_new = jnp.maximum(m_old, jnp.maximum(mx_ref[slot, hh, 0] - sh_a,
                                               mx_ref[slot, hh, 1] - sh_b))
        m_ref[hh] = m_new
        alpha_ref[slot, hh] = jnp.exp2(m_old - m_new)
        return m_new + sh_a, m_new + sh_b

    def softmax_block(slot, hh, u, offset):
        p_ref[slot, hh, u * blk:(u + 1) * blk] = jnp.exp2(s_ref[slot, hh, u] - offset).astype(BF16)

    def softmax(w, slot):
        for hh in heads:
            offsets = softmax_prep(w, slot, hh)
            for u in range(2):
                softmax_block(slot, hh, u, offsets[u])

    ones_rows = jnp.ones((SUM_ROWS, 2 * blk), BF16)

    def pv_dot(w, slot, hh):
        ja, jb = field(_F_JA, w), field(_F_JB, w)
        rows = slice(hh * HEAD_DIM, (hh + 1) * HEAD_DIM)
        vts = jnp.concatenate([vt_ref[0, ja, rows, :], vt_ref[0, jb, rows, :]], axis=1)
        lhs = jnp.concatenate([vts, ones_rows], axis=0)
        return jnp.dot(lhs, p_ref[slot, hh], preferred_element_type=F32)

    def accumulate(pvs, w, slot):
        par = field(_F_QBLK, w) % ACC_BUFFERS
        for hh in heads:
            acc_ref[par, hh] = alpha_ref[slot, hh] * acc_ref[par, hh] + pvs[hh]

    def finalize(w):
        i = field(_F_QBLK, w)
        par = i % ACC_BUFFERS
        o_t = jnp.concatenate([acc_ref[par, hh, :HEAD_DIM] / acc_ref[par, hh, HEAD_DIM:HEAD_DIM + 1]
                               for hh in heads], axis=0)
        o_ref[0, pl.ds(pl.multiple_of(i * blk, blk), blk), :] = o_t.T.astype(o_ref.dtype)

    acc_ref[...] = jnp.zeros_like(acc_ref)
    m_ref[...] = jnp.zeros_like(m_ref)

    stage_scores([score_dot(0, hh) for hh in heads], 0, 0, heads)
    stage_scores([score_dot(1, hh) for hh in heads], 1, 1, heads)
    softmax(0, 0)

    def item(w, slot):
        sm = 1 - slot
        sts, pvs = {}, {}
        for g, hhs in enumerate(groups):
            for hh in hhs:
                sts[hh] = score_dot(w + 2, hh)
            offs = {hh: softmax_prep(w + 1, sm, hh) for hh in hhs}
            if g > 0:
                stage_scores(sts, w + 2, slot, groups[g - 1])
            for hh in hhs:
                for u in range(2):
                    softmax_block(sm, hh, u, offs[hh][u])
        for hh in heads:
            pvs[hh] = pv_dot(w, slot, hh)
        stage_scores(sts, w + 2, slot, groups[-1])
        accumulate(pvs, w, slot)

    def body(trip, carry):
        for j in range(ITEMS_PER_TRIP):
            item(ITEMS_PER_TRIP * trip + j, j % 2)
        for j in range(ITEMS_PER_TRIP):
            w = ITEMS_PER_TRIP * trip + j
            pl.when(field(_F_LAST, w) == 1)(functools.partial(finalize, w))
        return carry

    assert n_items % ITEMS_PER_TRIP == 0 and ITEMS_PER_TRIP % 2 == 0
    lax.fori_loop(0, n_items // ITEMS_PER_TRIP, body, 0)


def _attn_call(slopes, bias, qt, k, vt, kmean):
    B, S, W = k.shape
    nb = S // MOBA_BLOCK
    step_lanes = GROUPS_PER_STEP * LANES
    n_steps = W // step_lanes
    hps = HEADS_PER_STEP
    n_items, sched = _attn_schedule(nb)
    grid_spec = pltpu.PrefetchScalarGridSpec(
        num_scalar_prefetch=2,
        grid=(B, n_steps),
        in_specs=[
            pl.BlockSpec((1, nb, step_lanes, MOBA_BLOCK), lambda b, p, *_: (b, 0, p, 0)),
            pl.BlockSpec((1, S, step_lanes), lambda b, p, *_: (b, 0, p)),
            pl.BlockSpec((1, nb, step_lanes, MOBA_BLOCK), lambda b, p, *_: (b, 0, p, 0)),
            pl.BlockSpec((1, nb, step_lanes), lambda b, p, *_: (b, 0, p)),
            pl.BlockSpec((hps, 2, MOBA_BLOCK, MOBA_BLOCK), lambda b, p, *_: (p, 0, 0, 0)),
        ],
        out_specs=pl.BlockSpec((1, S, step_lanes), lambda b, p, *_: (b, 0, p)),
        scratch_shapes=[
            pltpu.VMEM((hps, nb, nb, MOBA_BLOCK), F32),
            pltpu.VMEM((2, hps, 2, MOBA_BLOCK, MOBA_BLOCK), F32),
            pltpu.VMEM((2, hps, 2, 1, MOBA_BLOCK), F32),
            pltpu.VMEM((2, hps, 2 * MOBA_BLOCK, MOBA_BLOCK), BF16),
            pltpu.VMEM((2, hps, 1, MOBA_BLOCK), F32),
            pltpu.VMEM((hps, 1, MOBA_BLOCK), F32),
            pltpu.VMEM((ACC_BUFFERS, hps, HEAD_DIM + SUM_ROWS, MOBA_BLOCK), F32),
        ],
    )
    return pl.pallas_call(
        functools.partial(_attn_kernel, n_items),
        grid_spec=grid_spec,
        out_shape=jax.ShapeDtypeStruct((B, S, W), BF16),
        compiler_params=pltpu.CompilerParams(
            dimension_semantics=("arbitrary", "arbitrary"), vmem_limit_bytes=VMEM_LIMIT),
        name="moba_attn",
    )(slopes, jnp.asarray(sched), qt, k, vt, kmean, bias)


def _ffn_kernel(x_ref, a_ref, p_ref, wo_ref, g2_ref, wup_ref, wdn_ref, g3_ref, o_ref):
    sub = ROW_SUBTILE_FFN
    n_sub = x_ref.shape[0] // sub

    def head(i):
        rows = slice(i * sub, (i + 1) * sub)
        x1 = (x_ref[rows, :]
              + jnp.dot(a_ref[rows, :], wo_ref[:ATTN_WIDTH, :], preferred_element_type=F32)
              + jnp.dot(p_ref[rows, :], wo_ref[ATTN_WIDTH:, :], preferred_element_type=F32))
        o_ref[rows, :] = x1
        return _rms_norm(x1, g2_ref[...]).astype(BF16)

    def ffn(i, h):
        rows = slice(i * sub, (i + 1) * sub)
        for c in range(wup_ref.shape[1] // FF_CHUNK):
            cols = slice(c * FF_CHUNK, (c + 1) * FF_CHUNK)
            up = jnp.dot(h, wup_ref[:, cols], preferred_element_type=F32)
            act = jnp.square(jnp.maximum(up, 0.0)).astype(BF16)
            o_ref[rows, :] += jnp.dot(act, wdn_ref[cols, :], preferred_element_type=F32)

    def tail(i):
        rows = slice(i * sub, (i + 1) * sub)
        o_ref[rows, :] = _rms_norm(o_ref[rows, :], g3_ref[...])

    h_prev = head(0)
    for i in range(1, n_sub):
        h_next = head(i)
        ffn(i - 1, h_prev)
        tail(i - 1)
        h_prev = h_next
    ffn(n_sub - 1, h_prev)
    tail(n_sub - 1)


def _ffn_call(x, a, p, wo, g2, wup, wdn, g3):
    N, D = x.shape
    T = ROW_TILE_FFN
    const = lambda shape: pl.BlockSpec(shape, lambda t: (0,) * len(shape),
                                       pipeline_mode=pl.Buffered(1))
    return pl.pallas_call(
        _ffn_kernel,
        grid=(N // T,),
        in_specs=[
            pl.BlockSpec((T, D), lambda t: (t, 0)),
            pl.BlockSpec((T, ATTN_WIDTH), lambda t: (t, 0)),
            pl.BlockSpec((T, POOL_WIDTH), lambda t: (t, 0)),
            const(wo.shape), const((1, D)),
            const(wup.shape), const(wdn.shape), const((1, D)),
        ],
        out_specs=pl.BlockSpec((T, D), lambda t: (t, 0)),
        out_shape=jax.ShapeDtypeStruct((N, D), F32),
        compiler_params=pltpu.CompilerParams(
            dimension_semantics=("arbitrary",), vmem_limit_bytes=VMEM_LIMIT),
        name="outproj_ffn",
    )(x, a, p, wo, g2, wup, wdn, g3)


def _alibi_constants():
    slopes = (2.0 ** (-8.0 * np.arange(1, N_HEADS + 1) / N_HEADS) * LOG2E).astype(np.float32)
    key = np.arange(MOBA_BLOCK, dtype=np.float32)[:, None]
    qry = np.arange(MOBA_BLOCK, dtype=np.float32)[None, :]
    past = -slopes[:, None, None] * (qry - key)[None]
    own = np.where((key <= qry)[None], past, -np.inf).astype(np.float32)
    return jnp.asarray(slopes), jnp.asarray(np.stack([past, own], axis=1))


def kernel(x, norm_mix, w_in, w_pool, pool_scale, w_out, norm_mlp, w_up, w_down, norm_final):
    B, S, D = x.shape
    assert w_in.shape[0] == 1, "single trunk layer (the final norm is fused into the FFN call)"
    assert S % ROW_TILE_IN == 0 and (B * S) % ROW_TILE_FFN == 0 and S % MOBA_BLOCK == 0
    assert w_in.shape[2] == 3 * ATTN_WIDTH + POOL_WIDTH and D == ATTN_WIDTH + POOL_WIDTH
    wi = w_in[0].astype(BF16)
    wqvt = jnp.concatenate([wi[:, :ATTN_WIDTH], wi[:, 2 * ATTN_WIDTH:3 * ATTN_WIDTH]], axis=1).T
    qt, k, vt, kmean, p = _inproj_call(
        x, norm_mix[0][None, :], wi, wqvt, w_pool[0].astype(BF16), pool_scale[0][None, :])
    slopes, bias = _alibi_constants()
    a = _attn_call(slopes, bias, qt, k, vt, kmean)
    wo = w_out[0].astype(BF16)
    y = _ffn_call(x.reshape(B * S, D), a.reshape(B * S, ATTN_WIDTH),
                  p.reshape(B * S, POOL_WIDTH), wo,
                  norm_mlp[0][None, :], w_up[0].astype(BF16), w_down[0].astype(BF16),
                  norm_final[None, :])
    return y.reshape(B, S, D)
```

```python
import functools

import jax
import jax.numpy as jnp
import numpy as np
from jax import lax
from jax.experimental import pallas as pl
from jax.experimental.pallas import tpu as pltpu

F32 = jnp.float32
BF16 = jnp.bfloat16

N_HEADS = 8
HEAD_DIM = 64
ATTN_WIDTH = N_HEADS * HEAD_DIM
POOL_WINDOWS = (2, 4, 8, 16)
POOL_GROUP = 128
POOL_WIDTH = POOL_GROUP * len(POOL_WINDOWS)
MOBA_BLOCK = 256
MOBA_TOPK = 3
EPS = 1e-6
LOG2E = 1.4426950408889634
QK_SCALE = HEAD_DIM ** -0.5 * LOG2E
SUM_ROWS = 16
POOL_HALO = 16
LANES = 128
HEADS_PER_GROUP = LANES // HEAD_DIM
GROUPS_PER_STEP = 2
HEADS_PER_STEP = GROUPS_PER_STEP * HEADS_PER_GROUP
ITEMS_PER_TRIP = 2
ACC_BUFFERS = 2

ROW_TILE_IN = 2048
ROW_SUBTILE_IN = 512
ROW_TILE_FFN = 1024
ROW_SUBTILE_FFN = 512
FF_CHUNK = 1024
VMEM_LIMIT = 56 * 1024 * 1024

NT_DIMS = (((1,), (1,)), ((), ()))
TN_DIMS = (((0,), (0,)), ((), ()))


def _rms_norm(x, g):
    ms = jnp.mean(x * x, axis=-1, keepdims=True)
    return x * lax.rsqrt(ms + EPS) * g


def _inproj_kernel(x_ref, g_ref, wi_ref, wqvt_ref, wpool_ref, pscale_ref,
                   qt_ref, k_ref, vt_ref, kmean_ref, p_ref, halo_ref):
    t = pl.program_id(1)
    rows = x_ref.shape[1]
    sub = ROW_SUBTILE_IN
    nblk = sub // MOBA_BLOCK
    u_cols = slice(3 * ATTN_WIDTH, 3 * ATTN_WIDTH + POOL_WIDTH)

    @pl.when(t == 0)
    def _():
        halo_ref[...] = jnp.zeros_like(halo_ref)

    def project(i):
        r0 = i * sub
        h = _rms_norm(x_ref[0, r0:r0 + sub, :], g_ref[...]).astype(BF16)
        kf = jnp.dot(h, wi_ref[:, ATTN_WIDTH:2 * ATTN_WIDTH], preferred_element_type=F32)
        k_ref[0, r0:r0 + sub, :] = kf.astype(BF16)
        for b in range(nblk):
            blk_row = (t * (rows // sub) + i) * nblk + b
            kmean_ref[0, pl.ds(blk_row, 1), :] = (
                jnp.sum(kf[b * MOBA_BLOCK:(b + 1) * MOBA_BLOCK], axis=0, keepdims=True)
                * (1.0 / MOBA_BLOCK))
        qvt = lax.dot_general(wqvt_ref[...], h, NT_DIMS, preferred_element_type=F32)
        for b in range(nblk):
            cols = slice(b * MOBA_BLOCK, (b + 1) * MOBA_BLOCK)
            qt_ref[0, i * nblk + b] = (qvt[:ATTN_WIDTH, cols] * QK_SCALE).astype(BF16)
            vt_ref[0, i * nblk + b] = qvt[ATTN_WIDTH:, cols].astype(BF16)
        return jnp.dot(h, wi_ref[:, u_cols], preferred_element_type=F32)

    def pool(i, u, halo):
        r0 = i * sub
        ext = jnp.concatenate([halo, u], axis=0)
        pos = t * rows + r0 + lax.broadcasted_iota(jnp.int32, (sub, POOL_GROUP), 0)
        for g, w in enumerate(POOL_WINDOWS):
            cols = slice(g * POOL_GROUP, (g + 1) * POOL_GROUP)
            s = ext[:, cols]
            shift = 1
            while shift < w:
                s = s + pltpu.roll(s, shift, axis=0)
                shift *= 2
            cnt = jnp.minimum(pos + 1, w).astype(F32)
            mixed = s[POOL_HALO:, :] / cnt - u[:, cols]
            y = jnp.dot(mixed.astype(BF16), wpool_ref[g], preferred_element_type=F32)
            p_ref[0, r0:r0 + sub, cols] = (y * pscale_ref[:, cols]).astype(BF16)
        return u[sub - POOL_HALO:, :]

    n_sub = rows // sub
    halo = halo_ref[...]
    u_prev = project(0)
    for i in range(1, n_sub):
        u_next = project(i)
        halo = pool(i - 1, u_prev, halo)
        u_prev = u_next
    halo_ref[...] = pool(n_sub - 1, u_prev, halo)


def _inproj_call(x, g, wi, wqvt, wpool, pscale):
    B, S, D = x.shape
    T = ROW_TILE_IN
    nblk = T // MOBA_BLOCK
    nb = S // MOBA_BLOCK
    const = lambda shape: pl.BlockSpec(shape, lambda b, t: (0,) * len(shape),
                                       pipeline_mode=pl.Buffered(1))
    return pl.pallas_call(
        _inproj_kernel,
        grid=(B, S // T),
        in_specs=[
            pl.BlockSpec((1, T, D), lambda b, t: (b, t, 0)),
            const((1, D)),
            const(wi.shape),
            const(wqvt.shape),
            const(wpool.shape),
            const((1, POOL_WIDTH)),
        ],
        out_specs=[
            pl.BlockSpec((1, nblk, ATTN_WIDTH, MOBA_BLOCK), lambda b, t: (b, t, 0, 0)),
            pl.BlockSpec((1, T, ATTN_WIDTH), lambda b, t: (b, t, 0)),
            pl.BlockSpec((1, nblk, ATTN_WIDTH, MOBA_BLOCK), lambda b, t: (b, t, 0, 0)),
            pl.BlockSpec((1, nb, ATTN_WIDTH), lambda b, t: (b, 0, 0)),
            pl.BlockSpec((1, T, POOL_WIDTH), lambda b, t: (b, t, 0)),
        ],
        out_shape=[
            jax.ShapeDtypeStruct((B, nb, ATTN_WIDTH, MOBA_BLOCK), BF16),
            jax.ShapeDtypeStruct((B, S, ATTN_WIDTH), BF16),
            jax.ShapeDtypeStruct((B, nb, ATTN_WIDTH, MOBA_BLOCK), BF16),
            jax.ShapeDtypeStruct((B, nb, ATTN_WIDTH), F32),
            jax.ShapeDtypeStruct((B, S, POOL_WIDTH), BF16),
        ],
        scratch_shapes=[pltpu.VMEM((POOL_HALO, POOL_WIDTH), F32)],
        compiler_params=pltpu.CompilerParams(
            dimension_semantics=("arbitrary", "arbitrary"), vmem_limit_bytes=VMEM_LIMIT),
        name="inproj_pool",
    )(x, g, wi, wqvt, wpool, pscale)


_F_QBLK, _F_JA, _F_JB, _F_A_VALID, _F_B_VALID, _F_FIRST, _F_LAST = range(7)


def _attn_schedule(nb):
    items = []
    for i in range(nb):
        steps = (i + 2) // 2
        for t in range(steps):
            jb = i - 2 * t - 1
            items.append((i, i - 2 * t, max(jb, 0), 1, int(jb >= 0), int(t == 0), int(t == steps - 1)))
    n_items = len(items)
    items += [(nb - 1, 0, 0, 0, 0, 0, 0)] * 2
    return n_items, np.asarray(items, np.int32).T.reshape(-1)


def _attn_kernel(n_items, slopes_ref, sched_ref, qt_ref, k_ref, vt_ref, kmean_ref, bias_ref, o_ref,
                 shift_ref, s_ref, mx_ref, p_ref, alpha_ref, m_ref, acc_ref):
    step = pl.program_id(1)
    nb = kmean_ref.shape[1]
    seq = k_ref.shape[1]
    blk = MOBA_BLOCK
    neg_inf = -jnp.inf
    heads = range(HEADS_PER_STEP)
    groups = [tuple(range(g * HEADS_PER_GROUP, (g + 1) * HEADS_PER_GROUP)) for g in range(GROUPS_PER_STEP)]
    stride = n_items + 2

    def group_lanes(hh):
        g = hh // HEADS_PER_GROUP
        return slice(g * LANES, (g + 1) * LANES)

    def head_weights(hh, rows):
        pad = jnp.zeros_like(rows)
        return jnp.concatenate([rows, pad] if hh % HEADS_PER_GROUP == 0 else [pad, rows], axis=0)

    def head_rows(hh):
        return slice(hh * HEAD_DIM, (hh + 1) * HEAD_DIM)

    def field(f, w):
        return sched_ref[f * stride + w]

    kmean = kmean_ref[0].astype(BF16)
    key_blk = lax.broadcasted_iota(jnp.int32, (nb, seq), 0)
    qry_blk = lax.broadcasted_iota(jnp.int32, (nb, seq), 1) // blk
    dist = ((qry_blk - key_blk) * blk).astype(F32)
    for hh in heads:
        qt_all = jnp.concatenate([qt_ref[0, i, head_rows(hh), :] for i in range(nb)], axis=1)
        gate = jnp.dot(kmean[:, group_lanes(hh)], head_weights(hh, qt_all),
                       preferred_element_type=F32)
        gate = jnp.where(key_blk < qry_blk, gate, neg_inf)
        sel = key_blk == qry_blk
        for _ in range(MOBA_TOPK):
            top = jnp.max(gate, axis=0, keepdims=True)
            first = jnp.min(jnp.where(gate == top, key_blk, nb), axis=0, keepdims=True)
            pick = (key_blk == first) & (top > neg_inf)
            sel = sel | pick
            gate = jnp.where(pick, neg_inf, gate)
        slope = slopes_ref[step * HEADS_PER_STEP + hh]
        shift = jnp.where(sel, slope * dist, jnp.inf)
        for i in range(nb):
            shift_ref[hh, i] = shift[:, i * blk:(i + 1) * blk]

    def score_dot(w, hh):
        i, ja, jb = field(_F_QBLK, w), field(_F_JA, w), field(_F_JB, w)
        kk = jnp.concatenate(
            [k_ref[0, pl.ds(pl.multiple_of(ja * blk, blk), blk), group_lanes(hh)],
             k_ref[0, pl.ds(pl.multiple_of(jb * blk, blk), blk), group_lanes(hh)]], axis=0)
        return jnp.dot(kk, head_weights(hh, qt_ref[0, i, head_rows(hh), :]),
                       preferred_element_type=F32)

    def stage_unit(st, w, slot, hh, u):
        table = field(_F_FIRST, w) if u == 0 else 0
        s_u = st[u * blk:(u + 1) * blk] + bias_ref[hh, table]
        s_ref[slot, hh, u] = s_u
        mx_ref[slot, hh, u] = jnp.max(s_u, axis=0, keepdims=True)

    def stage_scores(sts, w, slot, hhs):
        for hh in hhs:
            for u in range(2):
                stage_unit(sts[hh], w, slot, hh, u)

    def softmax_prep(w, slot, hh):
        i, ja, jb = field(_F_QBLK, w), field(_F_JA, w), field(_F_JB, w)
        a_valid, b_valid = field(_F_A_VALID, w) == 1, field(_F_B_VALID, w) == 1
        first = field(_F_FIRST, w) == 1
        sh_a = jnp.where(a_valid, shift_ref[hh, i, pl.ds(ja, 1), :], jnp.inf)
        sh_b = jnp.where(b_valid, shift_ref[hh, i, pl.ds(jb, 1), :], jnp.inf)
        m_old = jnp.where(first, neg_inf, m_ref[hh])
        m_new = jnp.maximum(m_old, jnp.maximum(mx_ref[slot, hh, 0] - sh_a,
                                               mx_ref[slot, hh, 1] - sh_b))
        m_ref[hh] = m_new
        alpha_ref[slot, hh] = jnp.exp2(m_old - m_new)
        return m_new + sh_a, m_new + sh_b

    def softmax_block(slot, hh, u, offset):
        p_ref[slot, hh, u * blk:(u + 1) * blk] = jnp.exp2(s_ref[slot, hh, u] - offset).astype(BF16)

    def softmax(w, slot):
        for hh in heads:
            offsets = softmax_prep(w, slot, hh)
            for u in range(2):
                softmax_block(slot, hh, u, offsets[u])

    ones_rows = jnp.ones((SUM_ROWS, 2 * blk), BF16)

    def pv_dot(w, slot, hh):
        ja, jb = field(_F_JA, w), field(_F_JB, w)
        rows = slice(hh * HEAD_DIM, (hh + 1) * HEAD_DIM)
        vts = jnp.concatenate([vt_ref[0, ja, rows, :], vt_ref[0, jb, rows, :]], axis=1)
        lhs = jnp.concatenate([vts, ones_rows], axis=0)
        return jnp.dot(lhs, p_ref[slot, hh], preferred_element_type=F32)

    def accumulate(pvs, w, slot):
        par = field(_F_QBLK, w) % ACC_BUFFERS
        for hh in heads:
            acc_ref[par, hh] = alpha_ref[slot, hh] * acc_ref[par, hh] + pvs[hh]

    def finalize(w):
        i = field(_F_QBLK, w)
        par = i % ACC_BUFFERS
        for hh in heads:
            o_ref[0, i, head_rows(hh), :] = (
                acc_ref[par, hh, :HEAD_DIM] / acc_ref[par, hh, HEAD_DIM:HEAD_DIM + 1]).astype(o_ref.dtype)

    acc_ref[...] = jnp.zeros_like(acc_ref)
    m_ref[...] = jnp.zeros_like(m_ref)

    stage_scores([score_dot(0, hh) for hh in heads], 0, 0, heads)
    stage_scores([score_dot(1, hh) for hh in heads], 1, 1, heads)
    softmax(0, 0)

    def item(w, slot):
        sm = 1 - slot
        sts, pvs = {}, {}
        for g, hhs in enumerate(groups):
            for hh in hhs:
                sts[hh] = score_dot(w + 2, hh)
            offs = {hh: softmax_prep(w + 1, sm, hh) for hh in hhs}
            if g > 0:
                stage_scores(sts, w + 2, slot, groups[g - 1])
            for hh in hhs:
                for u in range(2):
                    softmax_block(sm, hh, u, offs[hh][u])
        for hh in heads:
            pvs[hh] = pv_dot(w, slot, hh)
        stage_scores(sts, w + 2, slot, groups[-1])
        accumulate(pvs, w, slot)

    def body(trip, carry):
        for j in range(ITEMS_PER_TRIP):
            item(ITEMS_PER_TRIP * trip + j, j % 2)
        for j in range(ITEMS_PER_TRIP):
            w = ITEMS_PER_TRIP * trip + j
            pl.when(field(_F_LAST, w) == 1)(functools.partial(finalize, w))
        return carry

    assert n_items % ITEMS_PER_TRIP == 0 and ITEMS_PER_TRIP % 2 == 0
    lax.fori_loop(0, n_items // ITEMS_PER_TRIP, body, 0)


def _attn_call(slopes, bias, qt, k, vt, kmean):
    B, S, W = k.shape
    nb = S // MOBA_BLOCK
    step_lanes = GROUPS_PER_STEP * LANES
    n_steps = W // step_lanes
    hps = HEADS_PER_STEP
    n_items, sched = _attn_schedule(nb)
    grid_spec = pltpu.PrefetchScalarGridSpec(
        num_scalar_prefetch=2,
        grid=(B, n_steps),
        in_specs=[
            pl.BlockSpec((1, nb, step_lanes, MOBA_BLOCK), lambda b, p, *_: (b, 0, p, 0)),
            pl.BlockSpec((1, S, step_lanes), lambda b, p, *_: (b, 0, p)),
            pl.BlockSpec((1, nb, step_lanes, MOBA_BLOCK), lambda b, p, *_: (b, 0, p, 0)),
            pl.BlockSpec((1, nb, step_lanes), lambda b, p, *_: (b, 0, p)),
            pl.BlockSpec((hps, 2, MOBA_BLOCK, MOBA_BLOCK), lambda b, p, *_: (p, 0, 0, 0)),
        ],
        out_specs=pl.BlockSpec((1, nb, step_lanes, MOBA_BLOCK), lambda b, p, *_: (b, 0, p, 0)),
        scratch_shapes=[
            pltpu.VMEM((hps, nb, nb, MOBA_BLOCK), F32),
            pltpu.VMEM((2, hps, 2, MOBA_BLOCK, MOBA_BLOCK), F32),
            pltpu.VMEM((2, hps, 2, 1, MOBA_BLOCK), F32),
            pltpu.VMEM((2, hps, 2 * MOBA_BLOCK, MOBA_BLOCK), BF16),
            pltpu.VMEM((2, hps, 1, MOBA_BLOCK), F32),
            pltpu.VMEM((hps, 1, MOBA_BLOCK), F32),
            pltpu.VMEM((ACC_BUFFERS, hps, HEAD_DIM + SUM_ROWS, MOBA_BLOCK), F32),
        ],
    )
    return pl.pallas_call(
        functools.partial(_attn_kernel, n_items),
        grid_spec=grid_spec,
        out_shape=jax.ShapeDtypeStruct((B, nb, W, MOBA_BLOCK), BF16),
        compiler_params=pltpu.CompilerParams(
            dimension_semantics=("arbitrary", "arbitrary"), vmem_limit_bytes=VMEM_LIMIT),
        name="moba_attn",
    )(slopes, jnp.asarray(sched), qt, k, vt, kmean, bias)


def _ffn_kernel(x_ref, at_ref, p_ref, wo_ref, g2_ref, wup_ref, wdn_ref, g3_ref, o_ref):
    sub = ROW_SUBTILE_FFN
    n_sub = x_ref.shape[0] // sub

    def head(i):
        rows = slice(i * sub, (i + 1) * sub)
        nblk = sub // MOBA_BLOCK
        attn = jnp.concatenate(
            [lax.dot_general(at_ref[i * nblk + b], wo_ref[:ATTN_WIDTH, :], TN_DIMS,
                             preferred_element_type=F32) for b in range(nblk)], axis=0)
        x1 = (x_ref[rows, :] + attn
              + jnp.dot(p_ref[rows, :], wo_ref[ATTN_WIDTH:, :], preferred_element_type=F32))
        o_ref[rows, :] = x1
        return _rms_norm(x1, g2_ref[...]).astype(BF16)

    def ffn(i, h):
        rows = slice(i * sub, (i + 1) * sub)
        for c in range(wup_ref.shape[1] // FF_CHUNK):
            cols = slice(c * FF_CHUNK, (c + 1) * FF_CHUNK)
            up = jnp.dot(h, wup_ref[:, cols], preferred_element_type=F32)
            act = jnp.square(jnp.maximum(up, 0.0)).astype(BF16)
            o_ref[rows, :] += jnp.dot(act, wdn_ref[cols, :], preferred_element_type=F32)

    def tail(i):
        rows = slice(i * sub, (i + 1) * sub)
        o_ref[rows, :] = _rms_norm(o_ref[rows, :], g3_ref[...])

    h_prev = head(0)
    for i in range(1, n_sub):
        h_next = head(i)
        ffn(i - 1, h_prev)
        tail(i - 1)
        h_prev = h_next
    ffn(n_sub - 1, h_prev)
    tail(n_sub - 1)


def _ffn_call(x, at, p, wo, g2, wup, wdn, g3):
    N, D = x.shape
    T = ROW_TILE_FFN
    const = lambda shape: pl.BlockSpec(shape, lambda t: (0,) * len(shape),
                                       pipeline_mode=pl.Buffered(1))
    return pl.pallas_call(
        _ffn_kernel,
        grid=(N // T,),
        in_specs=[
            pl.BlockSpec((T, D), lambda t: (t, 0)),
            pl.BlockSpec((T // MOBA_BLOCK, ATTN_WIDTH, MOBA_BLOCK), lambda t: (t, 0, 0)),
            pl.BlockSpec((T, POOL_WIDTH), lambda t: (t, 0)),
            const(wo.shape), const((1, D)),
            const(wup.shape), const(wdn.shape), const((1, D)),
        ],
        out_specs=pl.BlockSpec((T, D), lambda t: (t, 0)),
        out_shape=jax.ShapeDtypeStruct((N, D), F32),
        compiler_params=pltpu.CompilerParams(
            dimension_semantics=("arbitrary",), vmem_limit_bytes=VMEM_LIMIT),
        name="outproj_ffn",
    )(x, at, p, wo, g2, wup, wdn, g3)


def _alibi_constants():
    slopes = (2.0 ** (-8.0 * np.arange(1, N_HEADS + 1) / N_HEADS) * LOG2E).astype(np.float32)
    key = np.arange(MOBA_BLOCK, dtype=np.float32)[:, None]
    qry = np.arange(MOBA_BLOCK, dtype=np.float32)[None, :]
    past = -slopes[:, None, None] * (qry - key)[None]
    own = np.where((key <= qry)[None], past, -np.inf).astype(np.float32)
    return jnp.asarray(slopes), jnp.asarray(np.stack([past, own], axis=1))


def kernel(x, norm_mix, w_in, w_pool, pool_scale, w_out, norm_mlp, w_up, w_down, norm_final):
    B, S, D = x.shape
    assert w_in.shape[0] == 1, "single trunk layer (the final norm is fused into the FFN call)"
    assert S % ROW_TILE_IN == 0 and (B * S) % ROW_TILE_FFN == 0 and S % MOBA_BLOCK == 0
    assert w_in.shape[2] == 3 * ATTN_WIDTH + POOL_WIDTH and D == ATTN_WIDTH + POOL_WIDTH
    wi = w_in[0].astype(BF16)
    wqvt = jnp.concatenate([wi[:, :ATTN_WIDTH], wi[:, 2 * ATTN_WIDTH:3 * ATTN_WIDTH]], axis=1).T
    qt, k, vt, kmean, p = _inproj_call(
        x, norm_mix[0][None, :], wi, wqvt, w_pool[0].astype(BF16), pool_scale[0][None, :])
    slopes, bias = _alibi_constants()
    at = _attn_call(slopes, bias, qt, k, vt, kmean)
    wo = w_out[0].astype(BF16)
    y = _ffn_call(x.reshape(B * S, D), at.reshape(B * (S // MOBA_BLOCK), ATTN_WIDTH, MOBA_BLOCK),
                  p.reshape(B * S, POOL_WIDTH), wo,
                  norm_mlp[0][None, :], w_up[0].astype(BF16), w_down[0].astype(BF16),
                  norm_final[None, :])
    return y.reshape(B, S, D)
```

```python
import functools

import jax
import jax.numpy as jnp
import numpy as np
from jax import lax
from jax.experimental import pallas as pl
from jax.experimental.pallas import tpu as pltpu

F32 = jnp.float32
BF16 = jnp.bfloat16

N_HEADS = 8
HEAD_DIM = 64
ATTN_WIDTH = N_HEADS * HEAD_DIM
POOL_WINDOWS = (2, 4, 8, 16)
POOL_GROUP = 128
POOL_WIDTH = POOL_GROUP * len(POOL_WINDOWS)
MOBA_BLOCK = 256
MOBA_TOPK = 3
EPS = 1e-6
LOG2E = 1.4426950408889634
QK_SCALE = HEAD_DIM ** -0.5 * LOG2E
SUM_ROWS = 16
POOL_HALO = 16
LANES = 128
HEADS_PER_GROUP = LANES // HEAD_DIM
GROUPS_PER_STEP = 2
HEADS_PER_STEP = GROUPS_PER_STEP * HEADS_PER_GROUP
ITEMS_PER_TRIP = 2
STAGE_SLOTS = 2
ACC_BUFFERS = 2

ROW_TILE_IN = 2048
ROW_SUBTILE_IN = 512
ROW_TILE_FFN = 1024
ROW_SUBTILE_FFN = 512
FF_CHUNK = 1024
VMEM_LIMIT = 56 * 1024 * 1024

NT_DIMS = (((1,), (1,)), ((), ()))
TN_DIMS = (((0,), (0,)), ((), ()))


def _rms_norm(x, g):
    ms = jnp.mean(x * x, axis=-1, keepdims=True)
    return x * lax.rsqrt(ms + EPS) * g


def _inproj_kernel(x_ref, g_ref, wi_ref, wqvt_ref, wpool_ref, pscale_ref,
                   qt_ref, k_ref, vt_ref, kmean_ref, p_ref, halo_ref):
    t = pl.program_id(1)
    rows = x_ref.shape[1]
    sub = ROW_SUBTILE_IN
    nblk = sub // MOBA_BLOCK
    u_cols = slice(3 * ATTN_WIDTH, 3 * ATTN_WIDTH + POOL_WIDTH)

    @pl.when(t == 0)
    def _():
        halo_ref[...] = jnp.zeros_like(halo_ref)

    def project(i):
        r0 = i * sub
        h = _rms_norm(x_ref[0, r0:r0 + sub, :], g_ref[...]).astype(BF16)
        kf = jnp.dot(h, wi_ref[:, ATTN_WIDTH:2 * ATTN_WIDTH], preferred_element_type=F32)
        k_ref[0, r0:r0 + sub, :] = kf.astype(BF16)
        for b in range(nblk):
            blk_row = (t * (rows // sub) + i) * nblk + b
            kmean_ref[0, pl.ds(blk_row, 1), :] = (
                jnp.sum(kf[b * MOBA_BLOCK:(b + 1) * MOBA_BLOCK], axis=0, keepdims=True)
                * (1.0 / MOBA_BLOCK))
        qvt = lax.dot_general(wqvt_ref[...], h, NT_DIMS, preferred_element_type=F32)
        for b in range(nblk):
            cols = slice(b * MOBA_BLOCK, (b + 1) * MOBA_BLOCK)
            qt_ref[0, i * nblk + b] = (qvt[:ATTN_WIDTH, cols] * QK_SCALE).astype(BF16)
            vt_ref[0, i * nblk + b] = qvt[ATTN_WIDTH:, cols].astype(BF16)
        return jnp.dot(h, wi_ref[:, u_cols], preferred_element_type=F32)

    def pool(i, u, halo):
        r0 = i * sub
        ext = jnp.concatenate([halo, u], axis=0)
        pos = t * rows + r0 + lax.broadcasted_iota(jnp.int32, (sub, POOL_GROUP), 0)
        for g, w in enumerate(POOL_WINDOWS):
            cols = slice(g * POOL_GROUP, (g + 1) * POOL_GROUP)
            s = ext[:, cols]
            shift = 1
            while shift < w:
                s = s + pltpu.roll(s, shift, axis=0)
                shift *= 2
            cnt = jnp.minimum(pos + 1, w).astype(F32)
            mixed = s[POOL_HALO:, :] / cnt - u[:, cols]
            y = jnp.dot(mixed.astype(BF16), wpool_ref[g], preferred_element_type=F32)
            p_ref[0, r0:r0 + sub, cols] = (y * pscale_ref[:, cols]).astype(BF16)
        return u[sub - POOL_HALO:, :]

    n_sub = rows // sub
    halo = halo_ref[...]
    u_prev = project(0)
    for i in range(1, n_sub):
        u_next = project(i)
        halo = pool(i - 1, u_prev, halo)
        u_prev = u_next
    halo_ref[...] = pool(n_sub - 1, u_prev, halo)


def _inproj_call(x, g, wi, wqvt, wpool, pscale):
    B, S, D = x.shape
    T = ROW_TILE_IN
    nblk = T // MOBA_BLOCK
    nb = S // MOBA_BLOCK
    const = lambda shape: pl.BlockSpec(shape, lambda b, t: (0,) * len(shape),
                                       pipeline_mode=pl.Buffered(1))
    return pl.pallas_call(
        _inproj_kernel,
        grid=(B, S // T),
        in_specs=[
            pl.BlockSpec((1, T, D), lambda b, t: (b, t, 0)),
            const((1, D)),
            const(wi.shape),
            const(wqvt.shape),
            const(wpool.shape),
            const((1, POOL_WIDTH)),
        ],
        out_specs=[
            pl.BlockSpec((1, nblk, ATTN_WIDTH, MOBA_BLOCK), lambda b, t: (b, t, 0, 0)),
            pl.BlockSpec((1, T, ATTN_WIDTH), lambda b, t: (b, t, 0)),
            pl.BlockSpec((1, nblk, ATTN_WIDTH, MOBA_BLOCK), lambda b, t: (b, t, 0, 0)),
            pl.BlockSpec((1, nb, ATTN_WIDTH), lambda b, t: (b, 0, 0)),
            pl.BlockSpec((1, T, POOL_WIDTH), lambda b, t: (b, t, 0)),
        ],
        out_shape=[
            jax.ShapeDtypeStruct((B, nb, ATTN_WIDTH, MOBA_BLOCK), BF16),
            jax.ShapeDtypeStruct((B, S, ATTN_WIDTH), BF16),
            jax.ShapeDtypeStruct((B, nb, ATTN_WIDTH, MOBA_BLOCK), BF16),
            jax.ShapeDtypeStruct((B, nb, ATTN_WIDTH), F32),
            jax.ShapeDtypeStruct((B, S, POOL_WIDTH), BF16),
        ],
        scratch_shapes=[pltpu.VMEM((POOL_HALO, POOL_WIDTH), F32)],
        compiler_params=pltpu.CompilerParams(
            dimension_semantics=("arbitrary", "arbitrary"), vmem_limit_bytes=VMEM_LIMIT),
        name="inproj_pool",
    )(x, g, wi, wqvt, wpool, pscale)


_F_QBLK, _F_JA, _F_JB, _F_A_VALID, _F_B_VALID, _F_FIRST, _F_LAST = range(7)


def _attn_schedule(nb):
    items = []
    for i in range(nb):
        steps = (i + 2) // 2
        for t in range(steps):
            jb = i - 2 * t - 1
            items.append((i, i - 2 * t, max(jb, 0), 1, int(jb >= 0), int(t == 0), int(t == steps - 1)))
    n_items = len(items)
    items += [(nb - 1, 0, 0, 0, 0, 0, 0)] * 2
    return n_items, np.asarray(items, np.int32).T.reshape(-1)


def _attn_kernel(n_items, slopes_ref, sched_ref, qt_ref, k_ref, vt_ref, kmean_ref, bias_ref, o_ref,
                 shift_ref, s_ref, mx_ref, p_ref, alpha_ref, m_ref, acc_ref):
    step = pl.program_id(1)
    nb = kmean_ref.shape[1]
    seq = k_ref.shape[1]
    blk = MOBA_BLOCK
    neg_inf = -jnp.inf
    heads = range(HEADS_PER_STEP)
    groups = [tuple(range(g * HEADS_PER_GROUP, (g + 1) * HEADS_PER_GROUP)) for g in range(GROUPS_PER_STEP)]
    stride = n_items + 2

    def group_lanes(hh):
        g = hh // HEADS_PER_GROUP
        return slice(g * LANES, (g + 1) * LANES)

    def head_weights(hh, rows):
        pad = jnp.zeros_like(rows)
        return jnp.concatenate([rows, pad] if hh % HEADS_PER_GROUP == 0 else [pad, rows], axis=0)

    def head_rows(hh):
        return slice(hh * HEAD_DIM, (hh + 1) * HEAD_DIM)

    def field(f, w):
        return sched_ref[f * stride + w]

    kmean = kmean_ref[0].astype(BF16)
    key_blk = lax.broadcasted_iota(jnp.int32, (nb, seq), 0)
    qry_blk = lax.broadcasted_iota(jnp.int32, (nb, seq), 1) // blk
    dist = ((qry_blk - key_blk) * blk).astype(F32)
    key_idx = key_blk.astype(F32)
    for hh in heads:
        qt_all = jnp.concatenate([qt_ref[0, i, head_rows(hh), :] for i in range(nb)], axis=1)
        gate = jnp.dot(kmean[:, group_lanes(hh)], head_weights(hh, qt_all),
                       preferred_element_type=F32)
        gate = jnp.where(key_blk < qry_blk, gate, neg_inf)
        sel = key_blk == qry_blk
        for _ in range(MOBA_TOPK):
            top = jnp.max(gate, axis=0, keepdims=True)
            first = jnp.min(jnp.where(gate == top, key_idx, float(nb)), axis=0, keepdims=True)
            pick = (key_idx == first) & (top > neg_inf)
            sel = sel | pick
            gate = jnp.where(pick, neg_inf, gate)
        slope = slopes_ref[step * HEADS_PER_STEP + hh]
        shift = jnp.where(sel, slope * dist, jnp.inf)
        for i in range(nb):
            shift_ref[hh, i] = shift[:, i * blk:(i + 1) * blk]

    def score_dot(w, hh):
        i, ja, jb = field(_F_QBLK, w), field(_F_JA, w), field(_F_JB, w)
        kk = jnp.concatenate(
            [k_ref[0, pl.ds(pl.multiple_of(ja * blk, blk), blk), group_lanes(hh)],
             k_ref[0, pl.ds(pl.multiple_of(jb * blk, blk), blk), group_lanes(hh)]], axis=0)
        return jnp.dot(kk, head_weights(hh, qt_ref[0, i, head_rows(hh), :]),
                       preferred_element_type=F32)

    def stage_unit(st, w, slot, hh, u):
        table = field(_F_FIRST, w) if u == 0 else 0
        s_u = st[u * blk:(u + 1) * blk] + bias_ref[hh, table]
        s_ref[slot, hh, u] = s_u
        mx_ref[slot, hh, u] = jnp.max(s_u, axis=0, keepdims=True)

    def stage_scores(sts, w, slot, hhs):
        for hh in hhs:
            for u in range(2):
                stage_unit(sts[hh], w, slot, hh, u)

    def softmax_prep(w, slot, hh):
        i, ja, jb = field(_F_QBLK, w), field(_F_JA, w), field(_F_JB, w)
        a_valid, b_valid = field(_F_A_VALID, w) == 1, field(_F_B_VALID, w) == 1
        first = field(_F_FIRST, w) == 1
        sh_a = jnp.where(a_valid, shift_ref[hh, i, pl.ds(ja, 1), :], jnp.inf)
        sh_b = jnp.where(b_valid, shift_ref[hh, i, pl.ds(jb, 1), :], jnp.inf)
        m_old = jnp.where(first, neg_inf, m_ref[hh])
        m_new = jnp.maximum(m_old, jnp.maximum(mx_ref[slot, hh, 0] - sh_a,
                                               mx_ref[slot, hh, 1] - sh_b))
        m_ref[hh] = m_new
        alpha_ref[slot, hh] = jnp.exp2(m_old - m_new)
        return m_new + sh_a, m_new + sh_b

    def softmax_block(slot, hh, u, offset):
        p_ref[slot, hh, u * blk:(u + 1) * blk] = jnp.exp2((s_ref[slot, hh, u] - offset).astype(BF16))

    def softmax(w, slot):
        for hh in heads:
            offsets = softmax_prep(w, slot, hh)
            for u in range(2):
                softmax_block(slot, hh, u, offsets[u])

    ones_rows = jnp.ones((SUM_ROWS, 2 * blk), BF16)

    def pv_dot(w, slot, hh):
        ja, jb = field(_F_JA, w), field(_F_JB, w)
        rows = slice(hh * HEAD_DIM, (hh + 1) * HEAD_DIM)
        vts = jnp.concatenate([vt_ref[0, ja, rows, :], vt_ref[0, jb, rows, :]], axis=1)
        lhs = jnp.concatenate([vts, ones_rows], axis=0)
        return jnp.dot(lhs, p_ref[slot, hh], preferred_element_type=F32)

    def accumulate(pvs, w, slot):
        par = field(_F_QBLK, w) % ACC_BUFFERS
        for hh in heads:
            acc_ref[par, hh] = alpha_ref[slot, hh] * acc_ref[par, hh] + pvs[hh]

    def finalize(w):
        i = field(_F_QBLK, w)
        par = i % ACC_BUFFERS
        for hh in heads:
            o_ref[0, i, head_rows(hh), :] = (
                acc_ref[par, hh, :HEAD_DIM] / acc_ref[par, hh, HEAD_DIM:HEAD_DIM + 1]).astype(o_ref.dtype)

    acc_ref[...] = jnp.zeros_like(acc_ref)
    m_ref[...] = jnp.zeros_like(m_ref)

    stage_scores([score_dot(0, hh) for hh in heads], 0, 0, heads)
    stage_scores([score_dot(1, hh) for hh in heads], 1, 1, heads)
    softmax(0, 0)

    def item(w, j):
        slot, sm, sc = j % STAGE_SLOTS, (j + 1) % STAGE_SLOTS, (j + 2) % STAGE_SLOTS
        sts, pvs = {}, {}
        for g, hhs in enumerate(groups):
            for hh in hhs:
                sts[hh] = score_dot(w + 2, hh)
            offs = {hh: softmax_prep(w + 1, sm, hh) for hh in hhs}
            if g > 0:
                stage_scores(sts, w + 2, sc, groups[g - 1])
            for hh in hhs:
                for u in range(2):
                    softmax_block(sm, hh, u, offs[hh][u])
        for hh in heads:
            pvs[hh] = pv_dot(w, slot, hh)
        stage_scores(sts, w + 2, sc, groups[-1])
        accumulate(pvs, w, slot)

    def body(trip, carry):
        for j in range(ITEMS_PER_TRIP):
            item(ITEMS_PER_TRIP * trip + j, j)
        for j in range(ITEMS_PER_TRIP):
            w = ITEMS_PER_TRIP * trip + j
            pl.when(field(_F_LAST, w) == 1)(functools.partial(finalize, w))
        return carry

    assert n_items % ITEMS_PER_TRIP == 0 and ITEMS_PER_TRIP % STAGE_SLOTS == 0
    lax.fori_loop(0, n_items // ITEMS_PER_TRIP, body, 0)


def _attn_call(slopes, bias, qt, k, vt, kmean):
    B, S, W = k.shape
    nb = S // MOBA_BLOCK
    step_lanes = GROUPS_PER_STEP * LANES
    n_steps = W // step_lanes
    hps = HEADS_PER_STEP
    n_items, sched = _attn_schedule(nb)
    grid_spec = pltpu.PrefetchScalarGridSpec(
        num_scalar_prefetch=2,
        grid=(B, n_steps),
        in_specs=[
            pl.BlockSpec((1, nb, step_lanes, MOBA_BLOCK), lambda b, p, *_: (b, 0, p, 0)),
            pl.BlockSpec((1, S, step_lanes), lambda b, p, *_: (b, 0, p)),
            pl.BlockSpec((1, nb, step_lanes, MOBA_BLOCK), lambda b, p, *_: (b, 0, p, 0)),
            pl.BlockSpec((1, nb, step_lanes), lambda b, p, *_: (b, 0, p)),
            pl.BlockSpec((hps, 2, MOBA_BLOCK, MOBA_BLOCK), lambda b, p, *_: (p, 0, 0, 0)),
        ],
        out_specs=pl.BlockSpec((1, nb, step_lanes, MOBA_BLOCK), lambda b, p, *_: (b, 0, p, 0)),
        scratch_shapes=[
            pltpu.VMEM((hps, nb, nb, MOBA_BLOCK), F32),
            pltpu.VMEM((STAGE_SLOTS, hps, 2, MOBA_BLOCK, MOBA_BLOCK), F32),
            pltpu.VMEM((STAGE_SLOTS, hps, 2, 1, MOBA_BLOCK), F32),
            pltpu.VMEM((STAGE_SLOTS, hps, 2 * MOBA_BLOCK, MOBA_BLOCK), BF16),
            pltpu.VMEM((STAGE_SLOTS, hps, 1, MOBA_BLOCK), F32),
            pltpu.VMEM((hps, 1, MOBA_BLOCK), F32),
            pltpu.VMEM((ACC_BUFFERS, hps, HEAD_DIM + SUM_ROWS, MOBA_BLOCK), F32),
        ],
    )
    return pl.pallas_call(
        functools.partial(_attn_kernel, n_items),
        grid_spec=grid_spec,
        out_shape=jax.ShapeDtypeStruct((B, nb, W, MOBA_BLOCK), BF16),
        compiler_params=pltpu.CompilerParams(
            dimension_semantics=("arbitrary", "arbitrary"), vmem_limit_bytes=VMEM_LIMIT),
        name="moba_attn",
    )(slopes, jnp.asarray(sched), qt, k, vt, kmean, bias)


def _ffn_kernel(x_ref, at_ref, p_ref, wo_ref, g2_ref, wup_ref, wdn_ref, g3_ref, o_ref):
    sub = ROW_SUBTILE_FFN
    n_sub = x_ref.shape[0] // sub

    def head(i):
        rows = slice(i * sub, (i + 1) * sub)
        nblk = sub // MOBA_BLOCK
        attn = jnp.concatenate(
            [lax.dot_general(at_ref[i * nblk + b], wo_ref[:ATTN_WIDTH, :], TN_DIMS,
                             preferred_element_type=F32) for b in range(nblk)], axis=0)
        x1 = (x_ref[rows, :] + attn
              + jnp.dot(p_ref[rows, :], wo_ref[ATTN_WIDTH:, :], preferred_element_type=F32))
        o_ref[rows, :] = x1
        return _rms_norm(x1, g2_ref[...]).astype(BF16)

    def ffn(i, h):
        rows = slice(i * sub, (i + 1) * sub)
        for c in range(wup_ref.shape[1] // FF_CHUNK):
            cols = slice(c * FF_CHUNK, (c + 1) * FF_CHUNK)
            up = jnp.dot(h, wup_ref[:, cols], preferred_element_type=F32)
            act = jnp.square(jnp.maximum(up, 0.0)).astype(BF16)
            o_ref[rows, :] += jnp.dot(act, wdn_ref[cols, :], preferred_element_type=F32)

    def tail(i):
        rows = slice(i * sub, (i + 1) * sub)
        o_ref[rows, :] = _rms_norm(o_ref[rows, :], g3_ref[...])

    h_prev = head(0)
    for i in range(1, n_sub):
        h_next = head(i)
        ffn(i - 1, h_prev)
        tail(i - 1)
        h_prev = h_next
    ffn(n_sub - 1, h_prev)
    tail(n_sub - 1)


def _ffn_call(x, at, p, wo, g2, wup, wdn, g3):
    N, D = x.shape
    T = ROW_TILE_FFN
    const = lambda shape: pl.BlockSpec(shape, lambda t: (0,) * len(shape),
                                       pipeline_mode=pl.Buffered(1))
    return pl.pallas_call(
        _ffn_kernel,
        grid=(N // T,),
        in_specs=[
            pl.BlockSpec((T, D), lambda t: (t, 0)),
            pl.BlockSpec((T // MOBA_BLOCK, ATTN_WIDTH, MOBA_BLOCK), lambda t: (t, 0, 0)),
            pl.BlockSpec((T, POOL_WIDTH), lambda t: (t, 0)),
            const(wo.shape), const((1, D)),
            const(wup.shape), const(wdn.shape), const((1, D)),
        ],
        out_specs=pl.BlockSpec((T, D), lambda t: (t, 0)),
        out_shape=jax.ShapeDtypeStruct((N, D), F32),
        compiler_params=pltpu.CompilerParams(
            dimension_semantics=("arbitrary",), vmem_limit_bytes=VMEM_LIMIT),
        name="outproj_ffn",
    )(x, at, p, wo, g2, wup, wdn, g3)


def _alibi_constants():
    slopes = (2.0 ** (-8.0 * np.arange(1, N_HEADS + 1) / N_HEADS) * LOG2E).astype(np.float32)
    key = np.arange(MOBA_BLOCK, dtype=np.float32)[:, None]
    qry = np.arange(MOBA_BLOCK, dtype=np.float32)[None, :]
    past = -slopes[:, None, None] * (qry - key)[None]
    own = np.where((key <= qry)[None], past, -np.inf).astype(np.float32)
    return jnp.asarray(slopes), jnp.asarray(np.stack([past, own], axis=1))


def kernel(x, norm_mix, w_in, w_pool, pool_scale, w_out, norm_mlp, w_up, w_down, norm_final):
    B, S, D = x.shape
    assert w_in.shape[0] == 1, "single trunk layer (the final norm is fused into the FFN call)"
    assert S % ROW_TILE_IN == 0 and (B * S) % ROW_TILE_FFN == 0 and S % MOBA_BLOCK == 0
    assert w_in.shape[2] == 3 * ATTN_WIDTH + POOL_WIDTH and D == ATTN_WIDTH + POOL_WIDTH
    wi = w_in[0].astype(BF16)
    wqvt = jnp.concatenate([wi[:, :ATTN_WIDTH], wi[:, 2 * ATTN_WIDTH:3 * ATTN_WIDTH]], axis=1).T
    qt, k, vt, kmean, p = _inproj_call(
        x, norm_mix[0][None, :], wi, wqvt, w_pool[0].astype(BF16), pool_scale[0][None, :])
    slopes, bias = _alibi_constants()
    at = _attn_call(slopes, bias, qt, k, vt, kmean)
    wo = w_out[0].astype(BF16)
    y = _ffn_call(x.reshape(B * S, D), at.reshape(B * (S // MOBA_BLOCK), ATTN_WIDTH, MOBA_BLOCK),
                  p.reshape(B * S, POOL_WIDTH), wo,
                  norm_mlp[0][None, :], w_up[0].astype(BF16), w_down[0].astype(BF16),
                  norm_final[None, :])
    return y.reshape(B, S, D)
```

```python
import functools

import jax
import jax.numpy as jnp
import numpy as np
from jax import lax
from jax.experimental import pallas as pl
from jax.experimental.pallas import tpu as pltpu

F32 = jnp.float32
BF16 = jnp.bfloat16

N_HEADS = 8
HEAD_DIM = 64
ATTN_WIDTH = N_HEADS * HEAD_DIM
POOL_WINDOWS = (2, 4, 8, 16)
POOL_GROUP = 128
POOL_WIDTH = POOL_GROUP * len(POOL_WINDOWS)
MOBA_BLOCK = 256
MOBA_TOPK = 3
EPS = 1e-6
LOG2E = 1.4426950408889634
QK_SCALE = HEAD_DIM ** -0.5 * LOG2E
SUM_ROWS = 16
POOL_HALO = 16
LANES = 128
HEADS_PER_GROUP = LANES // HEAD_DIM
GROUPS_PER_STEP = 2
HEADS_PER_STEP = GROUPS_PER_STEP * HEADS_PER_GROUP
ITEMS_PER_TRIP = 2
STAGE_SLOTS = 2
ACC_BUFFERS = 2

ROW_TILE_IN = 2048
ROW_SUBTILE_IN = 512
ROW_TILE_FFN = 1024
ROW_SUBTILE_FFN = 512
FF_CHUNK = 1024
VMEM_LIMIT = 56 * 1024 * 1024

NT_DIMS = (((1,), (1,)), ((), ()))
TN_DIMS = (((0,), (0,)), ((), ()))


def _rms_norm(x, g):
    ms = jnp.mean(x * x, axis=-1, keepdims=True)
    return x * lax.rsqrt(ms + EPS) * g


def _inproj_kernel(x_ref, g_ref, wi_ref, wqvt_ref, wpool_ref, pscale_ref,
                   qt_ref, k_ref, vt_ref, kmean_ref, p_ref, halo_ref):
    t = pl.program_id(1)
    rows = x_ref.shape[1]
    sub = ROW_SUBTILE_IN
    nblk = sub // MOBA_BLOCK
    u_cols = slice(3 * ATTN_WIDTH, 3 * ATTN_WIDTH + POOL_WIDTH)

    @pl.when(t == 0)
    def _():
        halo_ref[...] = jnp.zeros_like(halo_ref)

    def project(i):
        r0 = i * sub
        h = _rms_norm(x_ref[0, r0:r0 + sub, :], g_ref[...]).astype(BF16)
        kf = jnp.dot(h, wi_ref[:, ATTN_WIDTH:2 * ATTN_WIDTH], preferred_element_type=F32)
        k_ref[0, r0:r0 + sub, :] = kf.astype(BF16)
        for b in range(nblk):
            blk_row = (t * (rows // sub) + i) * nblk + b
            kmean_ref[0, pl.ds(blk_row, 1), :] = (
                jnp.sum(kf[b * MOBA_BLOCK:(b + 1) * MOBA_BLOCK], axis=0, keepdims=True)
                * (1.0 / MOBA_BLOCK))
        qvt = lax.dot_general(wqvt_ref[...], h, NT_DIMS, preferred_element_type=F32)
        for b in range(nblk):
            cols = slice(b * MOBA_BLOCK, (b + 1) * MOBA_BLOCK)
            qt_ref[0, i * nblk + b] = (qvt[:ATTN_WIDTH, cols] * QK_SCALE).astype(BF16)
            vt_ref[0, i * nblk + b] = qvt[ATTN_WIDTH:, cols].astype(BF16)
        return jnp.dot(h, wi_ref[:, u_cols], preferred_element_type=F32)

    def pool(i, u, halo):
        r0 = i * sub
        ext = jnp.concatenate([halo, u], axis=0)
        pos = t * rows + r0 + lax.broadcasted_iota(jnp.int32, (sub, POOL_GROUP), 0)
        for g, w in enumerate(POOL_WINDOWS):
            cols = slice(g * POOL_GROUP, (g + 1) * POOL_GROUP)
            s = ext[:, cols]
            shift = 1
            while shift < w:
                s = s + pltpu.roll(s, shift, axis=0)
                shift *= 2
            cnt = jnp.minimum(pos + 1, w).astype(F32)
            mixed = s[POOL_HALO:, :] / cnt - u[:, cols]
            y = jnp.dot(mixed.astype(BF16), wpool_ref[g], preferred_element_type=F32)
            p_ref[0, r0:r0 + sub, cols] = (y * pscale_ref[:, cols]).astype(BF16)
        return u[sub - POOL_HALO:, :]

    n_sub = rows // sub
    halo = halo_ref[...]
    u_prev = project(0)
    for i in range(1, n_sub):
        u_next = project(i)
        halo = pool(i - 1, u_prev, halo)
        u_prev = u_next
    halo_ref[...] = pool(n_sub - 1, u_prev, halo)


def _inproj_call(x, g, wi, wqvt, wpool, pscale):
    B, S, D = x.shape
    T = ROW_TILE_IN
    nblk = T // MOBA_BLOCK
    nb = S // MOBA_BLOCK
    const = lambda shape: pl.BlockSpec(shape, lambda b, t: (0,) * len(shape),
                                       pipeline_mode=pl.Buffered(1))
    return pl.pallas_call(
        _inproj_kernel,
        grid=(B, S // T),
        in_specs=[
            pl.BlockSpec((1, T, D), lambda b, t: (b, t, 0)),
            const((1, D)),
            const(wi.shape),
            const(wqvt.shape),
            const(wpool.shape),
            const((1, POOL_WIDTH)),
        ],
        out_specs=[
            pl.BlockSpec((1, nblk, ATTN_WIDTH, MOBA_BLOCK), lambda b, t: (b, t, 0, 0)),
            pl.BlockSpec((1, T, ATTN_WIDTH), lambda b, t: (b, t, 0)),
            pl.BlockSpec((1, nblk, ATTN_WIDTH, MOBA_BLOCK), lambda b, t: (b, t, 0, 0)),
            pl.BlockSpec((1, nb, ATTN_WIDTH), lambda b, t: (b, 0, 0)),
            pl.BlockSpec((1, T, POOL_WIDTH), lambda b, t: (b, t, 0)),
        ],
        out_shape=[
            jax.ShapeDtypeStruct((B, nb, ATTN_WIDTH, MOBA_BLOCK), BF16),
            jax.ShapeDtypeStruct((B, S, ATTN_WIDTH), BF16),
            jax.ShapeDtypeStruct((B, nb, ATTN_WIDTH, MOBA_BLOCK), BF16),
            jax.ShapeDtypeStruct((B, nb, ATTN_WIDTH), F32),
            jax.ShapeDtypeStruct((B, S, POOL_WIDTH), BF16),
        ],
        scratch_shapes=[pltpu.VMEM((POOL_HALO, POOL_WIDTH), F32)],
        compiler_params=pltpu.CompilerParams(
            dimension_semantics=("arbitrary", "arbitrary"), vmem_limit_bytes=VMEM_LIMIT),
        name="inproj_pool",
    )(x, g, wi, wqvt, wpool, pscale)


_F_QBLK, _F_JA, _F_JB, _F_A_VALID, _F_B_VALID, _F_FIRST, _F_LAST = range(7)


def _attn_schedule(nb):
    items = []
    for i in range(nb):
        steps = (i + 2) // 2
        for t in range(steps):
            jb = i - 2 * t - 1
            items.append((i, i - 2 * t, max(jb, 0), 1, int(jb >= 0), int(t == 0), int(t == steps - 1)))
    n_items = len(items)
    items += [(nb - 1, 0, 0, 0, 0, 0, 0)] * 2
    return n_items, np.asarray(items, np.int32).T.reshape(-1)


def _attn_kernel(n_items, slopes_ref, sched_ref, qt_ref, k_ref, vt_ref, kmean_ref, bias_ref, o_ref,
                 shift_ref, s_ref, mx_ref, p_ref, alpha_ref, m_ref, acc_ref):
    step = pl.program_id(1)
    nb = kmean_ref.shape[1]
    seq = k_ref.shape[1]
    blk = MOBA_BLOCK
    neg_inf = -jnp.inf
    heads = range(HEADS_PER_STEP)
    groups = [tuple(range(g * HEADS_PER_GROUP, (g + 1) * HEADS_PER_GROUP)) for g in range(GROUPS_PER_STEP)]
    stride = n_items + 2

    def group_lanes(hh):
        g = hh // HEADS_PER_GROUP
        return slice(g * LANES, (g + 1) * LANES)

    def head_weights(hh, rows):
        pad = jnp.zeros_like(rows)
        return jnp.concatenate([rows, pad] if hh % HEADS_PER_GROUP == 0 else [pad, rows], axis=0)

    def head_rows(hh):
        return slice(hh * HEAD_DIM, (hh + 1) * HEAD_DIM)

    def field(f, w):
        return sched_ref[f * stride + w]

    kmean = kmean_ref[0].astype(BF16)
    key_blk = lax.broadcasted_iota(jnp.int32, (nb, seq), 0)
    qry_blk = lax.broadcasted_iota(jnp.int32, (nb, seq), 1) // blk
    dist = ((qry_blk - key_blk) * blk).astype(F32)
    key_idx = key_blk.astype(F32)
    for hh in heads:
        qt_all = jnp.concatenate([qt_ref[0, i, head_rows(hh), :] for i in range(nb)], axis=1)
        gate = jnp.dot(kmean[:, group_lanes(hh)], head_weights(hh, qt_all),
                       preferred_element_type=F32)
        gate = jnp.where(key_blk < qry_blk, gate, neg_inf)
        sel = key_blk == qry_blk
        for _ in range(MOBA_TOPK):
            top = jnp.max(gate, axis=0, keepdims=True)
            first = jnp.min(jnp.where(gate == top, key_idx, float(nb)), axis=0, keepdims=True)
            pick = (key_idx == first) & (top > neg_inf)
            sel = sel | pick
            gate = jnp.where(pick, neg_inf, gate)
        slope = slopes_ref[step * HEADS_PER_STEP + hh]
        shift = jnp.where(sel, slope * dist, jnp.inf)
        for i in range(nb):
            shift_ref[hh, i] = shift[:, i * blk:(i + 1) * blk]

    def score_dot(w, hh):
        i, ja, jb = field(_F_QBLK, w), field(_F_JA, w), field(_F_JB, w)
        kk = jnp.concatenate(
            [k_ref[0, pl.ds(pl.multiple_of(ja * blk, blk), blk), group_lanes(hh)],
             k_ref[0, pl.ds(pl.multiple_of(jb * blk, blk), blk), group_lanes(hh)]], axis=0)
        return jnp.dot(kk, head_weights(hh, qt_ref[0, i, head_rows(hh), :]),
                       preferred_element_type=F32)

    def stage_unit(st, w, slot, hh, u):
        table = field(_F_FIRST, w) if u == 0 else 0
        s_u = st[u * blk:(u + 1) * blk] + bias_ref[hh, table]
        s_ref[slot, hh, u] = s_u
        mx_ref[slot, hh, u] = jnp.max(s_u, axis=0, keepdims=True)

    def stage_scores(sts, w, slot, hhs):
        for hh in hhs:
            for u in range(2):
                stage_unit(sts[hh], w, slot, hh, u)

    def softmax_prep(w, slot, hh):
        i, ja, jb = field(_F_QBLK, w), field(_F_JA, w), field(_F_JB, w)
        a_valid, b_valid = field(_F_A_VALID, w) == 1, field(_F_B_VALID, w) == 1
        first = field(_F_FIRST, w) == 1
        sh_a = jnp.where(a_valid, shift_ref[hh, i, pl.ds(ja, 1), :], jnp.inf)
        sh_b = jnp.where(b_valid, shift_ref[hh, i, pl.ds(jb, 1), :], jnp.inf)
        m_old = jnp.where(first, neg_inf, m_ref[hh])
        m_new = jnp.maximum(m_old, jnp.maximum(mx_ref[slot, hh, 0] - sh_a,
                                               mx_ref[slot, hh, 1] - sh_b))
        m_ref[hh] = m_new
        alpha_ref[slot, hh] = jnp.exp2(m_old - m_new)
        return m_new + sh_a, m_new + sh_b

    def softmax_block(slot, hh, u, offset):
        p_ref[slot, hh, u * blk:(u + 1) * blk] = jnp.exp2(s_ref[slot, hh, u] - offset).astype(BF16)

    def softmax(w, slot):
        for hh in heads:
            offsets = softmax_prep(w, slot, hh)
            for u in range(2):
                softmax_block(slot, hh, u, offsets[u])

    ones_rows = jnp.ones((SUM_ROWS, 2 * blk), BF16)

    def pv_dot(w, slot, hh):
        ja, jb = field(_F_JA, w), field(_F_JB, w)
        rows = slice(hh * HEAD_DIM, (hh + 1) * HEAD_DIM)
        vts = jnp.concatenate([vt_ref[0, ja, rows, :], vt_ref[0, jb, rows, :]], axis=1)
        lhs = jnp.concatenate([vts, ones_rows], axis=0)
        return jnp.dot(lhs, p_ref[slot, hh], preferred_element_type=F32)

    def accumulate(pvs, w, slot):
        par = field(_F_QBLK, w) % ACC_BUFFERS
        for hh in heads:
            acc_ref[par, hh] = alpha_ref[slot, hh] * acc_ref[par, hh] + pvs[hh]

    def finalize(w):
        i = field(_F_QBLK, w)
        par = i % ACC_BUFFERS
        for hh in heads:
            o_ref[0, i, head_rows(hh), :] = (
                acc_ref[par, hh, :HEAD_DIM] / acc_ref[par, hh, HEAD_DIM:HEAD_DIM + 1]).astype(o_ref.dtype)

    acc_ref[...] = jnp.zeros_like(acc_ref)
    m_ref[...] = jnp.zeros_like(m_ref)

    stage_scores([score_dot(0, hh) for hh in heads], 0, 0, heads)
    stage_scores([score_dot(1, hh) for hh in heads], 1, 1, heads)
    softmax(0, 0)

    def item(w, j):
        slot, sm, sc = j % STAGE_SLOTS, (j + 1) % STAGE_SLOTS, (j + 2) % STAGE_SLOTS
        sts, pvs = {}, {}
        for g, hhs in enumerate(groups):
            for hh in hhs:
                sts[hh] = score_dot(w + 2, hh)
            offs = {hh: softmax_prep(w + 1, sm, hh) for hh in hhs}
            if g > 0:
                stage_scores(sts, w + 2, sc, groups[g - 1])
            for hh in hhs:
                for u in range(2):
                    softmax_block(sm, hh, u, offs[hh][u])
        for hh in heads:
            pvs[hh] = pv_dot(w, slot, hh)
        stage_scores(sts, w + 2, sc, groups[-1])
        accumulate(pvs, w, slot)

    def body(trip, carry):
        for j in range(ITEMS_PER_TRIP):
            item(ITEMS_PER_TRIP * trip + j, j)
        for j in range(ITEMS_PER_TRIP):
            w = ITEMS_PER_TRIP * trip + j
            pl.when(field(_F_LAST, w) == 1)(functools.partial(finalize, w))
        return carry

    assert n_items % ITEMS_PER_TRIP == 0 and ITEMS_PER_TRIP % STAGE_SLOTS == 0
    lax.fori_loop(0, n_items // ITEMS_PER_TRIP, body, 0)


def _attn_call(slopes, bias, qt, k, vt, kmean):
    B, S, W = k.shape
    nb = S // MOBA_BLOCK
    step_lanes = GROUPS_PER_STEP * LANES
    n_steps = W // step_lanes
    hps = HEADS_PER_STEP
    n_items, sched = _attn_schedule(nb)
    grid_spec = pltpu.PrefetchScalarGridSpec(
        num_scalar_prefetch=2,
        grid=(B, n_steps),
        in_specs=[
            pl.BlockSpec((1, nb, step_lanes, MOBA_BLOCK), lambda b, p, *_: (b, 0, p, 0)),
            pl.BlockSpec((1, S, step_lanes), lambda b, p, *_: (b, 0, p)),
            pl.BlockSpec((1, nb, step_lanes, MOBA_BLOCK), lambda b, p, *_: (b, 0, p, 0)),
            pl.BlockSpec((1, nb, step_lanes), lambda b, p, *_: (b, 0, p)),
            pl.BlockSpec((hps, 2, MOBA_BLOCK, MOBA_BLOCK), lambda b, p, *_: (p, 0, 0, 0)),
        ],
        out_specs=pl.BlockSpec((1, nb, step_lanes, MOBA_BLOCK), lambda b, p, *_: (b, 0, p, 0)),
        scratch_shapes=[
            pltpu.VMEM((hps, nb, nb, MOBA_BLOCK), F32),
            pltpu.VMEM((STAGE_SLOTS, hps, 2, MOBA_BLOCK, MOBA_BLOCK), F32),
            pltpu.VMEM((STAGE_SLOTS, hps, 2, 1, MOBA_BLOCK), F32),
            pltpu.VMEM((STAGE_SLOTS, hps, 2 * MOBA_BLOCK, MOBA_BLOCK), BF16),
            pltpu.VMEM((STAGE_SLOTS, hps, 1, MOBA_BLOCK), F32),
            pltpu.VMEM((hps, 1, MOBA_BLOCK), F32),
            pltpu.VMEM((ACC_BUFFERS, hps, HEAD_DIM + SUM_ROWS, MOBA_BLOCK), F32),
        ],
    )
    return pl.pallas_call(
        functools.partial(_attn_kernel, n_items),
        grid_spec=grid_spec,
        out_shape=jax.ShapeDtypeStruct((B, nb, W, MOBA_BLOCK), BF16),
        compiler_params=pltpu.CompilerParams(
            dimension_semantics=("arbitrary", "arbitrary"), vmem_limit_bytes=VMEM_LIMIT),
        name="moba_attn",
    )(slopes, jnp.asarray(sched), qt, k, vt, kmean, bias)


def _ffn_kernel(x_ref, at_ref, p_ref, wo_ref, g2_ref, wup_ref, wdn_ref, g3_ref, o_ref):
    sub = ROW_SUBTILE_FFN
    n_sub = x_ref.shape[0] // sub

    def head(i):
        rows = slice(i * sub, (i + 1) * sub)
        nblk = sub // MOBA_BLOCK
        attn = jnp.concatenate(
            [lax.dot_general(at_ref[i * nblk + b], wo_ref[:ATTN_WIDTH, :], TN_DIMS,
                             preferred_element_type=F32) for b in range(nblk)], axis=0)
        x1 = (x_ref[rows, :] + attn
              + jnp.dot(p_ref[rows, :], wo_ref[ATTN_WIDTH:, :], preferred_element_type=F32))
        o_ref[rows, :] = x1
        return _rms_norm(x1, g2_ref[...]).astype(BF16)

    def ffn(i, h):
        rows = slice(i * sub, (i + 1) * sub)
        for c in range(wup_ref.shape[1] // FF_CHUNK):
            cols = slice(c * FF_CHUNK, (c + 1) * FF_CHUNK)
            up = jnp.dot(h, wup_ref[:, cols], preferred_element_type=F32)
            act = jnp.square(jnp.maximum(up, 0.0)).astype(BF16)
            o_ref[rows, :] += jnp.dot(act, wdn_ref[cols, :], preferred_element_type=F32)

    def tail(i):
        rows = slice(i * sub, (i + 1) * sub)
        o_ref[rows, :] = _rms_norm(o_ref[rows, :], g3_ref[...])

    h_prev = head(0)
    for i in range(1, n_sub):
        h_next = head(i)
        ffn(i - 1, h_prev)
        tail(i - 1)
        h_prev = h_next
    ffn(n_sub - 1, h_prev)
    tail(n_sub - 1)


def _ffn_call(x, at, p, wo, g2, wup, wdn, g3):
    N, D = x.shape
    T = ROW_TILE_FFN
    const = lambda shape: pl.BlockSpec(shape, lambda t: (0,) * len(shape),
                                       pipeline_mode=pl.Buffered(1))
    return pl.pallas_call(
        _ffn_kernel,
        grid=(N // T,),
        in_specs=[
            pl.BlockSpec((T, D), lambda t: (t, 0)),
            pl.BlockSpec((T // MOBA_BLOCK, ATTN_WIDTH, MOBA_BLOCK), lambda t: (t, 0, 0)),
            pl.BlockSpec((T, POOL_WIDTH), lambda t: (t, 0)),
            const(wo.shape), const((1, D)),
            const(wup.shape), const(wdn.shape), const((1, D)),
        ],
        out_specs=pl.BlockSpec((T, D), lambda t: (t, 0)),
        out_shape=jax.ShapeDtypeStruct((N, D), F32),
        compiler_params=pltpu.CompilerParams(
            dimension_semantics=("arbitrary",), vmem_limit_bytes=VMEM_LIMIT),
        name="outproj_ffn",
    )(x, at, p, wo, g2, wup, wdn, g3)


def _alibi_constants():
    slopes = (2.0 ** (-8.0 * np.arange(1, N_HEADS + 1) / N_HEADS) * LOG2E).astype(np.float32)
    key = np.arange(MOBA_BLOCK, dtype=np.float32)[:, None]
    qry = np.arange(MOBA_BLOCK, dtype=np.float32)[None, :]
    past = -slopes[:, None, None] * (qry - key)[None]
    own = np.where((key <= qry)[None], past, -np.inf).astype(np.float32)
    return jnp.asarray(slopes), jnp.asarray(np.stack([past, own], axis=1))


def kernel(x, norm_mix, w_in, w_pool, pool_scale, w_out, norm_mlp, w_up, w_down, norm_final):
    B, S, D = x.shape
    assert w_in.shape[0] == 1, "single trunk layer (the final norm is fused into the FFN call)"
    assert S % ROW_TILE_IN == 0 and (B * S) % ROW_TILE_FFN == 0 and S % MOBA_BLOCK == 0
    assert w_in.shape[2] == 3 * ATTN_WIDTH + POOL_WIDTH and D == ATTN_WIDTH + POOL_WIDTH
    wi = w_in[0].astype(BF16)
    wqvt = jnp.concatenate([wi[:, :ATTN_WIDTH], wi[:, 2 * ATTN_WIDTH:3 * ATTN_WIDTH]], axis=1).T
    qt, k, vt, kmean, p = _inproj_call(
        x, norm_mix[0][None, :], wi, wqvt, w_pool[0].astype(BF16), pool_scale[0][None, :])
    slopes, bias = _alibi_constants()
    at = _attn_call(slopes, bias, qt, k, vt, kmean)
    wo = w_out[0].astype(BF16)
    y = _ffn_call(x.reshape(B * S, D), at.reshape(B * (S // MOBA_BLOCK), ATTN_WIDTH, MOBA_BLOCK),
                  p.reshape(B * S, POOL_WIDTH), wo,
                  norm_mlp[0][None, :], w_up[0].astype(BF16), w_down[0].astype(BF16),
                  norm_final[None, :])
    return y.reshape(B, S, D)
```

```python
import functools

import jax
import jax.numpy as jnp
import numpy as np
from jax import lax
from jax.experimental import pallas as pl
from jax.experimental.pallas import tpu as pltpu

F32 = jnp.float32
BF16 = jnp.bfloat16

N_HEADS = 8
HEAD_DIM = 64
ATTN_WIDTH = N_HEADS * HEAD_DIM
POOL_WINDOWS = (2, 4, 8, 16)
POOL_GROUP = 128
POOL_WIDTH = POOL_GROUP * len(POOL_WINDOWS)
MOBA_BLOCK = 256
MOBA_TOPK = 3
EPS = 1e-6
LOG2E = 1.4426950408889634
QK_SCALE = HEAD_DIM ** -0.5 * LOG2E
SUM_ROWS = 16
POOL_HALO = 16
LANES = 128
HEADS_PER_GROUP = LANES // HEAD_DIM
GROUPS_PER_STEP = 2
HEADS_PER_STEP = GROUPS_PER_STEP * HEADS_PER_GROUP
ITEMS_PER_TRIP = 2
STAGE_SLOTS = 2
ACC_BUFFERS = 2

ROW_TILE_IN = 2048
ROW_SUBTILE_IN = 512
ROW_TILE_FFN = 1024
ROW_SUBTILE_FFN = 512
FF_CHUNK = 1024
WEIGHT_STAGE_ELEMS = 512 * 1024
VMEM_LIMIT = 56 * 1024 * 1024

NT_DIMS = (((1,), (1,)), ((), ()))
TN_DIMS = (((0,), (0,)), ((), ()))


def _rms_norm(x, g):
    ms = jnp.mean(x * x, axis=-1, keepdims=True)
    return x * lax.rsqrt(ms + EPS) * g


def _inproj_kernel(x_ref, g_ref, wi_ref, wqvt_ref, wpool_ref, pscale_ref,
                   qt_ref, k_ref, vt_ref, kmean_ref, p_ref, halo_ref):
    t = pl.program_id(1)
    rows = x_ref.shape[1]
    sub = ROW_SUBTILE_IN
    nblk = sub // MOBA_BLOCK
    u_cols = slice(3 * ATTN_WIDTH, 3 * ATTN_WIDTH + POOL_WIDTH)

    @pl.when(t == 0)
    def _():
        halo_ref[...] = jnp.zeros_like(halo_ref)

    def project(i):
        r0 = i * sub
        h = _rms_norm(x_ref[0, r0:r0 + sub, :], g_ref[...]).astype(BF16)
        kf = jnp.dot(h, wi_ref[:, ATTN_WIDTH:2 * ATTN_WIDTH], preferred_element_type=F32)
        k_ref[0, r0:r0 + sub, :] = kf.astype(BF16)
        for b in range(nblk):
            blk_row = (t * (rows // sub) + i) * nblk + b
            kmean_ref[0, pl.ds(blk_row, 1), :] = (
                jnp.sum(kf[b * MOBA_BLOCK:(b + 1) * MOBA_BLOCK], axis=0, keepdims=True)
                * (1.0 / MOBA_BLOCK))
        qvt = lax.dot_general(wqvt_ref[...], h, NT_DIMS, preferred_element_type=F32)
        for b in range(nblk):
            cols = slice(b * MOBA_BLOCK, (b + 1) * MOBA_BLOCK)
            qt_ref[0, i * nblk + b] = (qvt[:ATTN_WIDTH, cols] * QK_SCALE).astype(BF16)
            vt_ref[0, i * nblk + b] = qvt[ATTN_WIDTH:, cols].astype(BF16)
        return jnp.dot(h, wi_ref[:, u_cols], preferred_element_type=F32)

    def pool(i, u, halo):
        r0 = i * sub
        ext = jnp.concatenate([halo, u], axis=0)
        pos = t * rows + r0 + lax.broadcasted_iota(jnp.int32, (sub, POOL_GROUP), 0)
        for g, w in enumerate(POOL_WINDOWS):
            cols = slice(g * POOL_GROUP, (g + 1) * POOL_GROUP)
            s = ext[:, cols]
            shift = 1
            while shift < w:
                s = s + pltpu.roll(s, shift, axis=0)
                shift *= 2
            cnt = jnp.minimum(pos + 1, w).astype(F32)
            mixed = s[POOL_HALO:, :] / cnt - u[:, cols]
            y = jnp.dot(mixed.astype(BF16), wpool_ref[g], preferred_element_type=F32)
            p_ref[0, r0:r0 + sub, cols] = (y * pscale_ref[:, cols]).astype(BF16)
        return u[sub - POOL_HALO:, :]

    n_sub = rows // sub
    halo = halo_ref[...]
    u_prev = project(0)
    for i in range(1, n_sub):
        u_next = project(i)
        halo = pool(i - 1, u_prev, halo)
        u_prev = u_next
    halo_ref[...] = pool(n_sub - 1, u_prev, halo)


def _inproj_call(x, g, wi, wqvt, wpool, pscale):
    B, S, D = x.shape
    T = ROW_TILE_IN
    nblk = T // MOBA_BLOCK
    nb = S // MOBA_BLOCK
    const = lambda shape: pl.BlockSpec(shape, lambda b, t: (0,) * len(shape),
                                       pipeline_mode=pl.Buffered(1))
    return pl.pallas_call(
        _inproj_kernel,
        grid=(B, S // T),
        in_specs=[
            pl.BlockSpec((1, T, D), lambda b, t: (b, t, 0)),
            const((1, D)),
            const(wi.shape),
            const(wqvt.shape),
            const(wpool.shape),
            const((1, POOL_WIDTH)),
        ],
        out_specs=[
            pl.BlockSpec((1, nblk, ATTN_WIDTH, MOBA_BLOCK), lambda b, t: (b, t, 0, 0)),
            pl.BlockSpec((1, T, ATTN_WIDTH), lambda b, t: (b, t, 0)),
            pl.BlockSpec((1, nblk, ATTN_WIDTH, MOBA_BLOCK), lambda b, t: (b, t, 0, 0)),
            pl.BlockSpec((1, nb, ATTN_WIDTH), lambda b, t: (b, 0, 0)),
            pl.BlockSpec((1, T, POOL_WIDTH), lambda b, t: (b, t, 0)),
        ],
        out_shape=[
            jax.ShapeDtypeStruct((B, nb, ATTN_WIDTH, MOBA_BLOCK), BF16),
            jax.ShapeDtypeStruct((B, S, ATTN_WIDTH), BF16),
            jax.ShapeDtypeStruct((B, nb, ATTN_WIDTH, MOBA_BLOCK), BF16),
            jax.ShapeDtypeStruct((B, nb, ATTN_WIDTH), F32),
            jax.ShapeDtypeStruct((B, S, POOL_WIDTH), BF16),
        ],
        scratch_shapes=[pltpu.VMEM((POOL_HALO, POOL_WIDTH), F32)],
        compiler_params=pltpu.CompilerParams(
            dimension_semantics=("arbitrary", "arbitrary"), vmem_limit_bytes=VMEM_LIMIT),
        name="inproj_pool",
    )(x, g, wi, wqvt, wpool, pscale)


_F_QBLK, _F_JA, _F_JB, _F_A_VALID, _F_B_VALID, _F_FIRST, _F_LAST = range(7)


def _attn_schedule(nb):
    items = []
    for i in range(nb):
        steps = (i + 2) // 2
        for t in range(steps):
            jb = i - 2 * t - 1
            items.append((i, i - 2 * t, max(jb, 0), 1, int(jb >= 0), int(t == 0), int(t == steps - 1)))
    n_items = len(items)
    items += [(nb - 1, 0, 0, 0, 0, 0, 0)] * 2
    return n_items, np.asarray(items, np.int32).T.reshape(-1)


def _attn_kernel(n_items, slopes_ref, sched_ref, qt_ref, k_ref, vt_ref, kmean_ref, bias_ref, o_ref,
                 shift_ref, s_ref, mx_ref, p_ref, alpha_ref, m_ref, acc_ref):
    step = pl.program_id(1)
    nb = kmean_ref.shape[1]
    seq = k_ref.shape[1]
    blk = MOBA_BLOCK
    neg_inf = -jnp.inf
    heads = range(HEADS_PER_STEP)
    groups = [tuple(range(g * HEADS_PER_GROUP, (g + 1) * HEADS_PER_GROUP)) for g in range(GROUPS_PER_STEP)]
    stride = n_items + 2

    def group_lanes(hh):
        g = hh // HEADS_PER_GROUP
        return slice(g * LANES, (g + 1) * LANES)

    def head_weights(hh, rows):
        pad = jnp.zeros_like(rows)
        return jnp.concatenate([rows, pad] if hh % HEADS_PER_GROUP == 0 else [pad, rows], axis=0)

    def head_rows(hh):
        return slice(hh * HEAD_DIM, (hh + 1) * HEAD_DIM)

    def field(f, w):
        return sched_ref[f * stride + w]

    kmean = kmean_ref[0].astype(BF16)
    key_blk = lax.broadcasted_iota(jnp.int32, (nb, seq), 0)
    qry_blk = lax.broadcasted_iota(jnp.int32, (nb, seq), 1) // blk
    dist = ((qry_blk - key_blk) * blk).astype(F32)
    key_idx = key_blk.astype(F32)
    for hh in heads:
        qt_all = jnp.concatenate([qt_ref[0, i, head_rows(hh), :] for i in range(nb)], axis=1)
        gate = jnp.dot(kmean[:, group_lanes(hh)], head_weights(hh, qt_all),
                       preferred_element_type=F32)
        gate = jnp.where(key_blk < qry_blk, gate, neg_inf)
        sel = key_blk == qry_blk
        for _ in range(MOBA_TOPK):
            top = jnp.max(gate, axis=0, keepdims=True)
            first = jnp.min(jnp.where(gate == top, key_idx, float(nb)), axis=0, keepdims=True)
            pick = (key_idx == first) & (top > neg_inf)
            sel = sel | pick
            gate = jnp.where(pick, neg_inf, gate)
        slope = slopes_ref[step * HEADS_PER_STEP + hh]
        shift = jnp.where(sel, slope * dist, jnp.inf)
        for i in range(nb):
            shift_ref[hh, i] = shift[:, i * blk:(i + 1) * blk]

    def score_dot(w, hh):
        i, ja, jb = field(_F_QBLK, w), field(_F_JA, w), field(_F_JB, w)
        kk = jnp.concatenate(
            [k_ref[0, pl.ds(pl.multiple_of(ja * blk, blk), blk), group_lanes(hh)],
             k_ref[0, pl.ds(pl.multiple_of(jb * blk, blk), blk), group_lanes(hh)]], axis=0)
        return jnp.dot(kk, head_weights(hh, qt_ref[0, i, head_rows(hh), :]),
                       preferred_element_type=F32)

    def stage_unit(st, w, slot, hh, u):
        table = field(_F_FIRST, w) if u == 0 else 0
        s_u = st[u * blk:(u + 1) * blk] + bias_ref[hh, table]
        s_ref[slot, hh, u] = s_u
        mx_ref[slot, hh, u] = jnp.max(s_u, axis=0, keepdims=True)

    def stage_scores(sts, w, slot, hhs):
        for hh in hhs:
            for u in range(2):
                stage_unit(sts[hh], w, slot, hh, u)

    def softmax_prep(w, slot, hh):
        i, ja, jb = field(_F_QBLK, w), field(_F_JA, w), field(_F_JB, w)
        a_valid, b_valid = field(_F_A_VALID, w) == 1, field(_F_B_VALID, w) == 1
        first = field(_F_FIRST, w) == 1
        sh_a = jnp.where(a_valid, shift_ref[hh, i, pl.ds(ja, 1), :], jnp.inf)
        sh_b = jnp.where(b_valid, shift_ref[hh, i, pl.ds(jb, 1), :], jnp.inf)
        m_old = jnp.where(first, neg_inf, m_ref[hh])
        m_new = jnp.maximum(m_old, jnp.maximum(mx_ref[slot, hh, 0] - sh_a,
                                               mx_ref[slot, hh, 1] - sh_b))
        m_ref[hh] = m_new
        alpha_ref[slot, hh] = jnp.exp2(m_old - m_new)
        return m_new + sh_a, m_new + sh_b

    def softmax_block(slot, hh, u, offset):
        p_ref[slot, hh, u * blk:(u + 1) * blk] = jnp.exp2(s_ref[slot, hh, u] - offset).astype(BF16)

    def softmax(w, slot):
        for hh in heads:
            offsets = softmax_prep(w, slot, hh)
            for u in range(2):
                softmax_block(slot, hh, u, offsets[u])

    ones_rows = jnp.ones((SUM_ROWS, 2 * blk), BF16)

    def pv_dot(w, slot, hh):
        ja, jb = field(_F_JA, w), field(_F_JB, w)
        rows = slice(hh * HEAD_DIM, (hh + 1) * HEAD_DIM)
        vts = jnp.concatenate([vt_ref[0, ja, rows, :], vt_ref[0, jb, rows, :]], axis=1)
        lhs = jnp.concatenate([vts, ones_rows], axis=0)
        return jnp.dot(lhs, p_ref[slot, hh], preferred_element_type=F32)

    def accumulate(pvs, w, slot):
        par = field(_F_QBLK, w) % ACC_BUFFERS
        for hh in heads:
            acc_ref[par, hh] = alpha_ref[slot, hh] * acc_ref[par, hh] + pvs[hh]

    def finalize(w):
        i = field(_F_QBLK, w)
        par = i % ACC_BUFFERS
        for hh in heads:
            o_ref[0, i, head_rows(hh), :] = (
                acc_ref[par, hh, :HEAD_DIM] / acc_ref[par, hh, HEAD_DIM:HEAD_DIM + 1]).astype(o_ref.dtype)

    acc_ref[...] = jnp.zeros_like(acc_ref)
    m_ref[...] = jnp.zeros_like(m_ref)

    stage_scores([score_dot(0, hh) for hh in heads], 0, 0, heads)
    stage_scores([score_dot(1, hh) for hh in heads], 1, 1, heads)
    softmax(0, 0)

    def item(w, j):
        slot, sm, sc = j % STAGE_SLOTS, (j + 1) % STAGE_SLOTS, (j + 2) % STAGE_SLOTS
        sts, pvs = {}, {}
        for g, hhs in enumerate(groups):
            for hh in hhs:
                sts[hh] = score_dot(w + 2, hh)
            offs = {hh: softmax_prep(w + 1, sm, hh) for hh in hhs}
            if g > 0:
                stage_scores(sts, w + 2, sc, groups[g - 1])
            for hh in hhs:
                for u in range(2):
                    softmax_block(sm, hh, u, offs[hh][u])
        for hh in heads:
            pvs[hh] = pv_dot(w, slot, hh)
        stage_scores(sts, w + 2, sc, groups[-1])
        accumulate(pvs, w, slot)

    def body(trip, carry):
        for j in range(ITEMS_PER_TRIP):
            item(ITEMS_PER_TRIP * trip + j, j)
        for j in range(ITEMS_PER_TRIP):
            w = ITEMS_PER_TRIP * trip + j
            pl.when(field(_F_LAST, w) == 1)(functools.partial(finalize, w))
        return carry

    assert n_items % ITEMS_PER_TRIP == 0 and ITEMS_PER_TRIP % STAGE_SLOTS == 0
    lax.fori_loop(0, n_items // ITEMS_PER_TRIP, body, 0)


def _attn_call(slopes, bias, qt, k, vt, kmean):
    B, S, W = k.shape
    nb = S // MOBA_BLOCK
    step_lanes = GROUPS_PER_STEP * LANES
    n_steps = W // step_lanes
    hps = HEADS_PER_STEP
    n_items, sched = _attn_schedule(nb)
    grid_spec = pltpu.PrefetchScalarGridSpec(
        num_scalar_prefetch=2,
        grid=(B, n_steps),
        in_specs=[
            pl.BlockSpec((1, nb, step_lanes, MOBA_BLOCK), lambda b, p, *_: (b, 0, p, 0)),
            pl.BlockSpec((1, S, step_lanes), lambda b, p, *_: (b, 0, p)),
            pl.BlockSpec((1, nb, step_lanes, MOBA_BLOCK), lambda b, p, *_: (b, 0, p, 0)),
            pl.BlockSpec((1, nb, step_lanes), lambda b, p, *_: (b, 0, p)),
            pl.BlockSpec((hps, 2, MOBA_BLOCK, MOBA_BLOCK), lambda b, p, *_: (p, 0, 0, 0)),
        ],
        out_specs=pl.BlockSpec((1, nb, step_lanes, MOBA_BLOCK), lambda b, p, *_: (b, 0, p, 0)),
        scratch_shapes=[
            pltpu.VMEM((hps, nb, nb, MOBA_BLOCK), F32),
            pltpu.VMEM((STAGE_SLOTS, hps, 2, MOBA_BLOCK, MOBA_BLOCK), F32),
            pltpu.VMEM((STAGE_SLOTS, hps, 2, 1, MOBA_BLOCK), F32),
            pltpu.VMEM((STAGE_SLOTS, hps, 2 * MOBA_BLOCK, MOBA_BLOCK), BF16),
            pltpu.VMEM((STAGE_SLOTS, hps, 1, MOBA_BLOCK), F32),
            pltpu.VMEM((hps, 1, MOBA_BLOCK), F32),
            pltpu.VMEM((ACC_BUFFERS, hps, HEAD_DIM + SUM_ROWS, MOBA_BLOCK), F32),
        ],
    )
    return pl.pallas_call(
        functools.partial(_attn_kernel, n_items),
        grid_spec=grid_spec,
        out_shape=jax.ShapeDtypeStruct((B, nb, W, MOBA_BLOCK), BF16),
        compiler_params=pltpu.CompilerParams(
            dimension_semantics=("arbitrary", "arbitrary"), vmem_limit_bytes=VMEM_LIMIT),
        name="moba_attn",
    )(slopes, jnp.asarray(sched), qt, k, vt, kmean, bias)


def _ffn_kernel(x_ref, at_ref, p_ref, wo_hbm, g2_ref, wup_hbm, wdn_hbm, g3_ref, o_ref,
                wo_ref, wup_ref, wdn_ref, wide_stage, narrow_stage, stage_sem):
    sub = ROW_SUBTILE_FFN
    n_sub = x_ref.shape[0] // sub

    @pl.when(pl.program_id(0) == 0)
    def _():
        jobs = []
        for src, dst in ((wo_hbm, wo_ref), (wup_hbm, wup_ref), (wdn_hbm, wdn_ref)):
            stage = wide_stage if src.shape[1] == wide_stage.shape[2] else narrow_stage
            assert src.shape[1] == stage.shape[2] and src.shape[0] % stage.shape[1] == 0
            jobs += [(src, dst, stage, r0) for r0 in range(0, src.shape[0], stage.shape[1])]

        def copy(n):
            src, _, stage, r0 = jobs[n]
            return pltpu.make_async_copy(src.at[pl.ds(r0, stage.shape[1]), :], stage.at[n % 2],
                                         stage_sem.at[n % 2])

        copy(0).start()
        for n, (_, dst, stage, r0) in enumerate(jobs):
            if n + 1 < len(jobs):
                copy(n + 1).start()
            copy(n).wait()
            dst[r0:r0 + stage.shape[1], :] = stage[n % 2].astype(BF16)

    def head(i):
        rows = slice(i * sub, (i + 1) * sub)
        nblk = sub // MOBA_BLOCK
        attn = jnp.concatenate(
            [lax.dot_general(at_ref[i * nblk + b], wo_ref[:ATTN_WIDTH, :], TN_DIMS,
                             preferred_element_type=F32) for b in range(nblk)], axis=0)
        x1 = (x_ref[rows, :] + attn
              + jnp.dot(p_ref[rows, :], wo_ref[ATTN_WIDTH:, :], preferred_element_type=F32))
        o_ref[rows, :] = x1
        return _rms_norm(x1, g2_ref[...]).astype(BF16)

    def ffn(i, h):
        rows = slice(i * sub, (i + 1) * sub)
        for c in range(wup_ref.shape[1] // FF_CHUNK):
            cols = slice(c * FF_CHUNK, (c + 1) * FF_CHUNK)
            up = jnp.dot(h, wup_ref[:, cols], preferred_element_type=F32)
            act = jnp.square(jnp.maximum(up, 0.0)).astype(BF16)
            o_ref[rows, :] += jnp.dot(act, wdn_ref[cols, :], preferred_element_type=F32)

    def tail(i):
        rows = slice(i * sub, (i + 1) * sub)
        o_ref[rows, :] = _rms_norm(o_ref[rows, :], g3_ref[...])

    h_prev = head(0)
    for i in range(1, n_sub):
        h_next = head(i)
        ffn(i - 1, h_prev)
        tail(i - 1)
        h_prev = h_next
    ffn(n_sub - 1, h_prev)
    tail(n_sub - 1)


def _ffn_call(x, at, p, wo, g2, wup, wdn, g3):
    N, D = x.shape
    T = ROW_TILE_FFN
    const = lambda shape: pl.BlockSpec(shape, lambda t: (0,) * len(shape),
                                       pipeline_mode=pl.Buffered(1))
    in_hbm = pl.BlockSpec(memory_space=pl.ANY)
    d_ff = wup.shape[1]
    assert wup.shape == (D, d_ff) and wdn.shape == (d_ff, D) and wo.shape == (D, D)
    return pl.pallas_call(
        _ffn_kernel,
        grid=(N // T,),
        in_specs=[
            pl.BlockSpec((T, D), lambda t: (t, 0)),
            pl.BlockSpec((T // MOBA_BLOCK, ATTN_WIDTH, MOBA_BLOCK), lambda t: (t, 0, 0)),
            pl.BlockSpec((T, POOL_WIDTH), lambda t: (t, 0)),
            in_hbm, const((1, D)),
            in_hbm, in_hbm, const((1, D)),
        ],
        out_specs=pl.BlockSpec((T, D), lambda t: (t, 0)),
        out_shape=jax.ShapeDtypeStruct((N, D), F32),
        scratch_shapes=[
            pltpu.VMEM(wo.shape, BF16), pltpu.VMEM(wup.shape, BF16), pltpu.VMEM(wdn.shape, BF16),
            pltpu.VMEM((2, WEIGHT_STAGE_ELEMS // d_ff, d_ff), F32),
            pltpu.VMEM((2, WEIGHT_STAGE_ELEMS // D, D), F32),
            pltpu.SemaphoreType.DMA((2,)),
        ],
        compiler_params=pltpu.CompilerParams(
            dimension_semantics=("arbitrary",), vmem_limit_bytes=VMEM_LIMIT),
        name="outproj_ffn",
    )(x, at, p, wo, g2, wup, wdn, g3)


def _alibi_constants():
    slopes = (2.0 ** (-8.0 * np.arange(1, N_HEADS + 1) / N_HEADS) * LOG2E).astype(np.float32)
    key = np.arange(MOBA_BLOCK, dtype=np.float32)[:, None]
    qry = np.arange(MOBA_BLOCK, dtype=np.float32)[None, :]
    past = -slopes[:, None, None] * (qry - key)[None]
    own = np.where((key <= qry)[None], past, -np.inf).astype(np.float32)
    return jnp.asarray(slopes), jnp.asarray(np.stack([past, own], axis=1))


def kernel(x, norm_mix, w_in, w_pool, pool_scale, w_out, norm_mlp, w_up, w_down, norm_final):
    B, S, D = x.shape
    assert w_in.shape[0] == 1, "single trunk layer (the final norm is fused into the FFN call)"
    assert S % ROW_TILE_IN == 0 and (B * S) % ROW_TILE_FFN == 0 and S % MOBA_BLOCK == 0
    assert w_in.shape[2] == 3 * ATTN_WIDTH + POOL_WIDTH and D == ATTN_WIDTH + POOL_WIDTH
    wi = w_in[0].astype(BF16)
    wqvt = jnp.concatenate([wi[:, :ATTN_WIDTH], wi[:, 2 * ATTN_WIDTH:3 * ATTN_WIDTH]], axis=1).T
    qt, k, vt, kmean, p = _inproj_call(
        x, norm_mix[0][None, :], wi, wqvt, w_pool[0].astype(BF16), pool_scale[0][None, :])
    slopes, bias = _alibi_constants()
    at = _attn_call(slopes, bias, qt, k, vt, kmean)
    y = _ffn_call(x.reshape(B * S, D), at.reshape(B * (S // MOBA_BLOCK), ATTN_WIDTH, MOBA_BLOCK),
                  p.reshape(B * S, POOL_WIDTH), w_out[0],
                  norm_mlp[0][None, :], w_up[0], w_down[0], norm_final[None, :])
    return y.reshape(B, S, D)
```

```python
import functools

import jax
import jax.numpy as jnp
import numpy as np
from jax import lax
from jax.experimental import pallas as pl
from jax.experimental.pallas import tpu as pltpu

F32 = jnp.float32
BF16 = jnp.bfloat16

N_HEADS = 8
HEAD_DIM = 64
ATTN_WIDTH = N_HEADS * HEAD_DIM
POOL_WINDOWS = (2, 4, 8, 16)
POOL_GROUP = 128
POOL_WIDTH = POOL_GROUP * len(POOL_WINDOWS)
MOBA_BLOCK = 256
MOBA_TOPK = 3
EPS = 1e-6
LOG2E = 1.4426950408889634
QK_SCALE = HEAD_DIM ** -0.5 * LOG2E
SUM_ROWS = 16
POOL_HALO = 16
LANES = 128
HEADS_PER_GROUP = LANES // HEAD_DIM
GROUPS_PER_STEP = 2
HEADS_PER_STEP = GROUPS_PER_STEP * HEADS_PER_GROUP
ITEMS_PER_TRIP = 2
STAGE_SLOTS = 2
ACC_BUFFERS = 2

ROW_TILE_IN = 2048
ROW_SUBTILE_IN = 512
ROW_TILE_FFN = 1024
ROW_SUBTILE_FFN = 512
FF_CHUNK = 1024
WEIGHT_STAGE_ELEMS = 512 * 1024
VMEM_LIMIT = 56 * 1024 * 1024

NT_DIMS = (((1,), (1,)), ((), ()))
TN_DIMS = (((0,), (0,)), ((), ()))


def _rms_norm(x, g):
    ms = jnp.mean(x * x, axis=-1, keepdims=True)
    return x * lax.rsqrt(ms + EPS) * g


def _inproj_kernel(x_ref, g_ref, wi_hbm, wpool_ref, pscale_ref,
                   qt_ref, k_ref, vt_ref, kmean_ref, p_ref,
                   halo_ref, wi_ref, wqvt_ref, stage_ref, stage_sem):
    t = pl.program_id(1)
    rows = x_ref.shape[1]
    sub = ROW_SUBTILE_IN
    nblk = sub // MOBA_BLOCK
    u_cols = slice(3 * ATTN_WIDTH, 3 * ATTN_WIDTH + POOL_WIDTH)

    @pl.when((pl.program_id(0) == 0) & (t == 0))
    def _():
        chunk = stage_ref.shape[1]
        n_chunks = wi_hbm.shape[0] // chunk

        def copy(n):
            return pltpu.make_async_copy(wi_hbm.at[pl.ds(n * chunk, chunk), :], stage_ref.at[n % 2],
                                         stage_sem.at[n % 2])

        copy(0).start()
        for n in range(n_chunks):
            if n + 1 < n_chunks:
                copy(n + 1).start()
            copy(n).wait()
            wi_ref[n * chunk:(n + 1) * chunk, :] = stage_ref[n % 2].astype(BF16)
        wqvt_ref[:ATTN_WIDTH, :] = wi_ref[:, :ATTN_WIDTH].T
        wqvt_ref[ATTN_WIDTH:, :] = wi_ref[:, 2 * ATTN_WIDTH:3 * ATTN_WIDTH].T

    @pl.when(t == 0)
    def _():
        halo_ref[...] = jnp.zeros_like(halo_ref)

    def project(i):
        r0 = i * sub
        h = _rms_norm(x_ref[0, r0:r0 + sub, :], g_ref[...]).astype(BF16)
        kf = jnp.dot(h, wi_ref[:, ATTN_WIDTH:2 * ATTN_WIDTH], preferred_element_type=F32)
        k_ref[0, r0:r0 + sub, :] = kf.astype(BF16)
        for b in range(nblk):
            blk_row = (t * (rows // sub) + i) * nblk + b
            kmean_ref[0, pl.ds(blk_row, 1), :] = (
                jnp.sum(kf[b * MOBA_BLOCK:(b + 1) * MOBA_BLOCK], axis=0, keepdims=True)
                * (1.0 / MOBA_BLOCK))
        qvt = lax.dot_general(wqvt_ref[...], h, NT_DIMS, preferred_element_type=F32)
        for b in range(nblk):
            cols = slice(b * MOBA_BLOCK, (b + 1) * MOBA_BLOCK)
            qt_ref[0, i * nblk + b] = (qvt[:ATTN_WIDTH, cols] * QK_SCALE).astype(BF16)
            vt_ref[0, i * nblk + b] = qvt[ATTN_WIDTH:, cols].astype(BF16)
        return jnp.dot(h, wi_ref[:, u_cols], preferred_element_type=F32)

    def pool(i, u, halo):
        r0 = i * sub
        ext = jnp.concatenate([halo, u], axis=0)
        pos = t * rows + r0 + lax.broadcasted_iota(jnp.int32, (sub, POOL_GROUP), 0)
        for g, w in enumerate(POOL_WINDOWS):
            cols = slice(g * POOL_GROUP, (g + 1) * POOL_GROUP)
            s = ext[:, cols]
            shift = 1
            while shift < w:
                s = s + pltpu.roll(s, shift, axis=0)
                shift *= 2
            cnt = jnp.minimum(pos + 1, w).astype(F32)
            mixed = s[POOL_HALO:, :] / cnt - u[:, cols]
            y = jnp.dot(mixed.astype(BF16), wpool_ref[g], preferred_element_type=F32)
            p_ref[0, r0:r0 + sub, cols] = (y * pscale_ref[:, cols]).astype(BF16)
        return u[sub - POOL_HALO:, :]

    n_sub = rows // sub
    halo = halo_ref[...]
    u_prev = project(0)
    for i in range(1, n_sub):
        u_next = project(i)
        halo = pool(i - 1, u_prev, halo)
        u_prev = u_next
    halo_ref[...] = pool(n_sub - 1, u_prev, halo)


def _inproj_call(x, g, wi, wpool, pscale):
    B, S, D = x.shape
    T = ROW_TILE_IN
    nblk = T // MOBA_BLOCK
    nb = S // MOBA_BLOCK
    const = lambda shape: pl.BlockSpec(shape, lambda b, t: (0,) * len(shape),
                                       pipeline_mode=pl.Buffered(1))
    return pl.pallas_call(
        _inproj_kernel,
        grid=(B, S // T),
        in_specs=[
            pl.BlockSpec((1, T, D), lambda b, t: (b, t, 0)),
            const((1, D)),
            pl.BlockSpec(memory_space=pl.ANY),
            const(wpool.shape),
            const((1, POOL_WIDTH)),
        ],
        out_specs=[
            pl.BlockSpec((1, nblk, ATTN_WIDTH, MOBA_BLOCK), lambda b, t: (b, t, 0, 0)),
            pl.BlockSpec((1, T, ATTN_WIDTH), lambda b, t: (b, t, 0)),
            pl.BlockSpec((1, nblk, ATTN_WIDTH, MOBA_BLOCK), lambda b, t: (b, t, 0, 0)),
            pl.BlockSpec((1, nb, ATTN_WIDTH), lambda b, t: (b, 0, 0)),
            pl.BlockSpec((1, T, POOL_WIDTH), lambda b, t: (b, t, 0)),
        ],
        out_shape=[
            jax.ShapeDtypeStruct((B, nb, ATTN_WIDTH, MOBA_BLOCK), BF16),
            jax.ShapeDtypeStruct((B, S, ATTN_WIDTH), BF16),
            jax.ShapeDtypeStruct((B, nb, ATTN_WIDTH, MOBA_BLOCK), BF16),
            jax.ShapeDtypeStruct((B, nb, ATTN_WIDTH), F32),
            jax.ShapeDtypeStruct((B, S, POOL_WIDTH), BF16),
        ],
        scratch_shapes=[
            pltpu.VMEM((POOL_HALO, POOL_WIDTH), F32),
            pltpu.VMEM(wi.shape, BF16),
            pltpu.VMEM((2 * ATTN_WIDTH, D), BF16),
            pltpu.VMEM((2, WEIGHT_STAGE_ELEMS // wi.shape[1], wi.shape[1]), F32),
            pltpu.SemaphoreType.DMA((2,)),
        ],
        compiler_params=pltpu.CompilerParams(
            dimension_semantics=("arbitrary", "arbitrary"), vmem_limit_bytes=VMEM_LIMIT),
        name="inproj_pool",
    )(x, g, wi, wpool, pscale)


_F_QBLK, _F_JA, _F_JB, _F_A_VALID, _F_B_VALID, _F_FIRST, _F_LAST = range(7)


def _attn_schedule(nb):
    items = []
    for i in range(nb):
        steps = (i + 2) // 2
        for t in range(steps):
            jb = i - 2 * t - 1
            items.append((i, i - 2 * t, max(jb, 0), 1, int(jb >= 0), int(t == 0), int(t == steps - 1)))
    n_items = len(items)
    items += [(nb - 1, 0, 0, 0, 0, 0, 0)] * 2
    return n_items, np.asarray(items, np.int32).T.reshape(-1)


def _attn_kernel(n_items, slopes_ref, sched_ref, qt_ref, k_ref, vt_ref, kmean_ref, bias_ref, o_ref,
                 shift_ref, s_ref, mx_ref, p_ref, alpha_ref, m_ref, acc_ref):
    step = pl.program_id(1)
    nb = kmean_ref.shape[1]
    seq = k_ref.shape[1]
    blk = MOBA_BLOCK
    neg_inf = -jnp.inf
    heads = range(HEADS_PER_STEP)
    groups = [tuple(range(g * HEADS_PER_GROUP, (g + 1) * HEADS_PER_GROUP)) for g in range(GROUPS_PER_STEP)]
    stride = n_items + 2

    def group_lanes(hh):
        g = hh // HEADS_PER_GROUP
        return slice(g * LANES, (g + 1) * LANES)

    def head_weights(hh, rows):
        pad = jnp.zeros_like(rows)
        return jnp.concatenate([rows, pad] if hh % HEADS_PER_GROUP == 0 else [pad, rows], axis=0)

    def head_rows(hh):
        return slice(hh * HEAD_DIM, (hh + 1) * HEAD_DIM)

    def field(f, w):
        return sched_ref[f * stride + w]

    kmean = kmean_ref[0].astype(BF16)
    key_blk = lax.broadcasted_iota(jnp.int32, (nb, seq), 0)
    qry_blk = lax.broadcasted_iota(jnp.int32, (nb, seq), 1) // blk
    dist = ((qry_blk - key_blk) * blk).astype(F32)
    key_idx = key_blk.astype(F32)
    for hh in heads:
        qt_all = jnp.concatenate([qt_ref[0, i, head_rows(hh), :] for i in range(nb)], axis=1)
        gate = jnp.dot(kmean[:, group_lanes(hh)], head_weights(hh, qt_all),
                       preferred_element_type=F32)
        gate = jnp.where(key_blk < qry_blk, gate, neg_inf)
        sel = key_blk == qry_blk
        for _ in range(MOBA_TOPK):
            top = jnp.max(gate, axis=0, keepdims=True)
            first = jnp.min(jnp.where(gate == top, key_idx, float(nb)), axis=0, keepdims=True)
            pick = (key_idx == first) & (top > neg_inf)
            sel = sel | pick
            gate = jnp.where(pick, neg_inf, gate)
        slope = slopes_ref[step * HEADS_PER_STEP + hh]
        shift = jnp.where(sel, slope * dist, jnp.inf)
        for i in range(nb):
            shift_ref[hh, i] = shift[:, i * blk:(i + 1) * blk]

    def score_dot(w, hh):
        i, ja, jb = field(_F_QBLK, w), field(_F_JA, w), field(_F_JB, w)
        kk = jnp.concatenate(
            [k_ref[0, pl.ds(pl.multiple_of(ja * blk, blk), blk), group_lanes(hh)],
             k_ref[0, pl.ds(pl.multiple_of(jb * blk, blk), blk), group_lanes(hh)]], axis=0)
        return jnp.dot(kk, head_weights(hh, qt_ref[0, i, head_rows(hh), :]),
                       preferred_element_type=F32)

    def stage_unit(st, w, slot, hh, u):
        table = field(_F_FIRST, w) if u == 0 else 0
        s_u = st[u * blk:(u + 1) * blk] + bias_ref[hh, table]
        s_ref[slot, hh, u] = s_u
        mx_ref[slot, hh, u] = jnp.max(s_u, axis=0, keepdims=True)

    def stage_scores(sts, w, slot, hhs):
        for hh in hhs:
            for u in range(2):
                stage_unit(sts[hh], w, slot, hh, u)

    def softmax_prep(w, slot, hh):
        i, ja, jb = field(_F_QBLK, w), field(_F_JA, w), field(_F_JB, w)
        a_valid, b_valid = field(_F_A_VALID, w) == 1, field(_F_B_VALID, w) == 1
        first = field(_F_FIRST, w) == 1
        sh_a = jnp.where(a_valid, shift_ref[hh, i, pl.ds(ja, 1), :], jnp.inf)
        sh_b = jnp.where(b_valid, shift_ref[hh, i, pl.ds(jb, 1), :], jnp.inf)
        m_old = jnp.where(first, neg_inf, m_ref[hh])
        m_new = jnp.maximum(m_old, jnp.maximum(mx_ref[slot, hh, 0] - sh_a,
                                               mx_ref[slot, hh, 1] - sh_b))
        m_ref[hh] = m_new
        alpha_ref[slot, hh] = jnp.exp2(m_old - m_new)
        return m_new + sh_a, m_new + sh_b

    def softmax_block(slot, hh, u, offset):
        p_ref[slot, hh, u * blk:(u + 1) * blk] = jnp.exp2(s_ref[slot, hh, u] - offset).astype(BF16)

    def softmax(w, slot):
        for hh in heads:
            offsets = softmax_prep(w, slot, hh)
            for u in range(2):
                softmax_block(slot, hh, u, offsets[u])

    ones_rows = jnp.ones((SUM_ROWS, 2 * blk), BF16)

    def pv_dot(w, slot, hh):
        ja, jb = field(_F_JA, w), field(_F_JB, w)
        rows = slice(hh * HEAD_DIM, (hh + 1) * HEAD_DIM)
        vts = jnp.concatenate([vt_ref[0, ja, rows, :], vt_ref[0, jb, rows, :]], axis=1)
        lhs = jnp.concatenate([vts, ones_rows], axis=0)
        return jnp.dot(lhs, p_ref[slot, hh], preferred_element_type=F32)

    def accumulate(pvs, w, slot):
        par = field(_F_QBLK, w) % ACC_BUFFERS
        for hh in heads:
            acc_ref[par, hh] = alpha_ref[slot, hh] * acc_ref[par, hh] + pvs[hh]

    def finalize(w):
        i = field(_F_QBLK, w)
        par = i % ACC_BUFFERS
        for hh in heads:
            o_ref[0, i, head_rows(hh), :] = (
                acc_ref[par, hh, :HEAD_DIM] / acc_ref[par, hh, HEAD_DIM:HEAD_DIM + 1]).astype(o_ref.dtype)

    acc_ref[...] = jnp.zeros_like(acc_ref)
    m_ref[...] = jnp.zeros_like(m_ref)

    stage_scores([score_dot(0, hh) for hh in heads], 0, 0, heads)
    stage_scores([score_dot(1, hh) for hh in heads], 1, 1, heads)
    softmax(0, 0)

    def item(w, j):
        slot, sm, sc = j % STAGE_SLOTS, (j + 1) % STAGE_SLOTS, (j + 2) % STAGE_SLOTS
        sts, pvs = {}, {}
        for g, hhs in enumerate(groups):
            for hh in hhs:
                sts[hh] = score_dot(w + 2, hh)
            offs = {hh: softmax_prep(w + 1, sm, hh) for hh in hhs}
            if g > 0:
                stage_scores(sts, w + 2, sc, groups[g - 1])
            for hh in hhs:
                for u in range(2):
                    softmax_block(sm, hh, u, offs[hh][u])
        for hh in heads:
            pvs[hh] = pv_dot(w, slot, hh)
        stage_scores(sts, w + 2, sc, groups[-1])
        accumulate(pvs, w, slot)

    def body(trip, carry):
        for j in range(ITEMS_PER_TRIP):
            item(ITEMS_PER_TRIP * trip + j, j)
        for j in range(ITEMS_PER_TRIP):
            w = ITEMS_PER_TRIP * trip + j
            pl.when(field(_F_LAST, w) == 1)(functools.partial(finalize, w))
        return carry

    assert n_items % ITEMS_PER_TRIP == 0 and ITEMS_PER_TRIP % STAGE_SLOTS == 0
    lax.fori_loop(0, n_items // ITEMS_PER_TRIP, body, 0)


def _attn_call(slopes, bias, qt, k, vt, kmean):
    B, S, W = k.shape
    nb = S // MOBA_BLOCK
    step_lanes = GROUPS_PER_STEP * LANES
    n_steps = W // step_lanes
    hps = HEADS_PER_STEP
    n_items, sched = _attn_schedule(nb)
    grid_spec = pltpu.PrefetchScalarGridSpec(
        num_scalar_prefetch=2,
        grid=(B, n_steps),
        in_specs=[
            pl.BlockSpec((1, nb, step_lanes, MOBA_BLOCK), lambda b, p, *_: (b, 0, p, 0)),
            pl.BlockSpec((1, S, step_lanes), lambda b, p, *_: (b, 0, p)),
            pl.BlockSpec((1, nb, step_lanes, MOBA_BLOCK), lambda b, p, *_: (b, 0, p, 0)),
            pl.BlockSpec((1, nb, step_lanes), lambda b, p, *_: (b, 0, p)),
            pl.BlockSpec((hps, 2, MOBA_BLOCK, MOBA_BLOCK), lambda b, p, *_: (p, 0, 0, 0)),
        ],
        out_specs=pl.BlockSpec((1, nb, step_lanes, MOBA_BLOCK), lambda b, p, *_: (b, 0, p, 0)),
        scratch_shapes=[
            pltpu.VMEM((hps, nb, nb, MOBA_BLOCK), F32),
            pltpu.VMEM((STAGE_SLOTS, hps, 2, MOBA_BLOCK, MOBA_BLOCK), F32),
            pltpu.VMEM((STAGE_SLOTS, hps, 2, 1, MOBA_BLOCK), F32),
            pltpu.VMEM((STAGE_SLOTS, hps, 2 * MOBA_BLOCK, MOBA_BLOCK), BF16),
            pltpu.VMEM((STAGE_SLOTS, hps, 1, MOBA_BLOCK), F32),
            pltpu.VMEM((hps, 1, MOBA_BLOCK), F32),
            pltpu.VMEM((ACC_BUFFERS, hps, HEAD_DIM + SUM_ROWS, MOBA_BLOCK), F32),
        ],
    )
    return pl.pallas_call(
        functools.partial(_attn_kernel, n_items),
        grid_spec=grid_spec,
        out_shape=jax.ShapeDtypeStruct((B, nb, W, MOBA_BLOCK), BF16),
        compiler_params=pltpu.CompilerParams(
            dimension_semantics=("arbitrary", "arbitrary"), vmem_limit_bytes=VMEM_LIMIT),
        name="moba_attn",
    )(slopes, jnp.asarray(sched), qt, k, vt, kmean, bias)


def _ffn_kernel(x_ref, at_ref, p_ref, wo_hbm, g2_ref, wup_hbm, wdn_hbm, g3_ref, o_ref,
                wo_ref, wup_ref, wdn_ref, wide_stage, narrow_stage, stage_sem):
    sub = ROW_SUBTILE_FFN
    n_sub = x_ref.shape[0] // sub

    @pl.when(pl.program_id(0) == 0)
    def _():
        jobs = []
        for src, dst in ((wo_hbm, wo_ref), (wup_hbm, wup_ref), (wdn_hbm, wdn_ref)):
            stage = wide_stage if src.shape[1] == wide_stage.shape[2] else narrow_stage
            assert src.shape[1] == stage.shape[2] and src.shape[0] % stage.shape[1] == 0
            jobs += [(src, dst, stage, r0) for r0 in range(0, src.shape[0], stage.shape[1])]

        def copy(n):
            src, _, stage, r0 = jobs[n]
            return pltpu.make_async_copy(src.at[pl.ds(r0, stage.shape[1]), :], stage.at[n % 2],
                                         stage_sem.at[n % 2])

        copy(0).start()
        for n, (_, dst, stage, r0) in enumerate(jobs):
            if n + 1 < len(jobs):
                copy(n + 1).start()
            copy(n).wait()
            dst[r0:r0 + stage.shape[1], :] = stage[n % 2].astype(BF16)

    def head(i):
        rows = slice(i * sub, (i + 1) * sub)
        nblk = sub // MOBA_BLOCK
        attn = jnp.concatenate(
            [lax.dot_general(at_ref[i * nblk + b], wo_ref[:ATTN_WIDTH, :], TN_DIMS,
                             preferred_element_type=F32) for b in range(nblk)], axis=0)
        x1 = (x_ref[rows, :] + attn
              + jnp.dot(p_ref[rows, :], wo_ref[ATTN_WIDTH:, :], preferred_element_type=F32))
        o_ref[rows, :] = x1
        return _rms_norm(x1, g2_ref[...]).astype(BF16)

    def ffn(i, h):
        rows = slice(i * sub, (i + 1) * sub)
        for c in range(wup_ref.shape[1] // FF_CHUNK):
            cols = slice(c * FF_CHUNK, (c + 1) * FF_CHUNK)
            up = jnp.dot(h, wup_ref[:, cols], preferred_element_type=F32)
            act = jnp.square(jnp.maximum(up, 0.0)).astype(BF16)
            o_ref[rows, :] += jnp.dot(act, wdn_ref[cols, :], preferred_element_type=F32)

    def tail(i):
        rows = slice(i * sub, (i + 1) * sub)
        o_ref[rows, :] = _rms_norm(o_ref[rows, :], g3_ref[...])

    h_prev = head(0)
    for i in range(1, n_sub):
        h_next = head(i)
        ffn(i - 1, h_prev)
        tail(i - 1)
        h_prev = h_next
    ffn(n_sub - 1, h_prev)
    tail(n_sub - 1)


def _ffn_call(x, at, p, wo, g2, wup, wdn, g3):
    N, D = x.shape
    T = ROW_TILE_FFN
    const = lambda shape: pl.BlockSpec(shape, lambda t: (0,) * len(shape),
                                       pipeline_mode=pl.Buffered(1))
    in_hbm = pl.BlockSpec(memory_space=pl.ANY)
    d_ff = wup.shape[1]
    assert wup.shape == (D, d_ff) and wdn.shape == (d_ff, D) and wo.shape == (D, D)
    return pl.pallas_call(
        _ffn_kernel,
        grid=(N // T,),
        in_specs=[
            pl.BlockSpec((T, D), lambda t: (t, 0)),
            pl.BlockSpec((T // MOBA_BLOCK, ATTN_WIDTH, MOBA_BLOCK), lambda t: (t, 0, 0)),
            pl.BlockSpec((T, POOL_WIDTH), lambda t: (t, 0)),
            in_hbm, const((1, D)),
            in_hbm, in_hbm, const((1, D)),
        ],
        out_specs=pl.BlockSpec((T, D), lambda t: (t, 0)),
        out_shape=jax.ShapeDtypeStruct((N, D), F32),
        scratch_shapes=[
            pltpu.VMEM(wo.shape, BF16), pltpu.VMEM(wup.shape, BF16), pltpu.VMEM(wdn.shape, BF16),
            pltpu.VMEM((2, WEIGHT_STAGE_ELEMS // d_ff, d_ff), F32),
            pltpu.VMEM((2, WEIGHT_STAGE_ELEMS // D, D), F32),
            pltpu.SemaphoreType.DMA((2,)),
        ],
        compiler_params=pltpu.CompilerParams(
            dimension_semantics=("arbitrary",), vmem_limit_bytes=VMEM_LIMIT),
        name="outproj_ffn",
    )(x, at, p, wo, g2, wup, wdn, g3)


def _alibi_constants():
    slopes = (2.0 ** (-8.0 * np.arange(1, N_HEADS + 1) / N_HEADS) * LOG2E).astype(np.float32)
    key = np.arange(MOBA_BLOCK, dtype=np.float32)[:, None]
    qry = np.arange(MOBA_BLOCK, dtype=np.float32)[None, :]
    past = -slopes[:, None, None] * (qry - key)[None]
    own = np.where((key <= qry)[None], past, -np.inf).astype(np.float32)
    return jnp.asarray(slopes), jnp.asarray(np.stack([past, own], axis=1))


def kernel(x, norm_mix, w_in, w_pool, pool_scale, w_out, norm_mlp, w_up, w_down, norm_final):
    B, S, D = x.shape
    assert w_in.shape[0] == 1, "single trunk layer (the final norm is fused into the FFN call)"
    assert S % ROW_TILE_IN == 0 and (B * S) % ROW_TILE_FFN == 0 and S % MOBA_BLOCK == 0
    assert w_in.shape[2] == 3 * ATTN_WIDTH + POOL_WIDTH and D == ATTN_WIDTH + POOL_WIDTH
    qt, k, vt, kmean, p = _inproj_call(
        x, norm_mix[0][None, :], w_in[0], w_pool[0].astype(BF16), pool_scale[0][None, :])
    slopes, bias = _alibi_constants()
    at = _attn_call(slopes, bias, qt, k, vt, kmean)
    y = _ffn_call(x.reshape(B * S, D), at.reshape(B * (S // MOBA_BLOCK), ATTN_WIDTH, MOBA_BLOCK),
                  p.reshape(B * S, POOL_WIDTH), w_out[0],
                  norm_mlp[0][None, :], w_up[0], w_down[0], norm_final[None, :])
    return y.reshape(B, S, D)
```

```python
import functools

import jax
import jax.numpy as jnp
import numpy as np
from jax import lax
from jax.experimental import pallas as pl
from jax.experimental.pallas import tpu as pltpu

F32 = jnp.float32
BF16 = jnp.bfloat16

N_HEADS = 8
HEAD_DIM = 64
ATTN_WIDTH = N_HEADS * HEAD_DIM
POOL_WINDOWS = (2, 4, 8, 16)
POOL_GROUP = 128
POOL_WIDTH = POOL_GROUP * len(POOL_WINDOWS)
MOBA_BLOCK = 256
MOBA_TOPK = 3
EPS = 1e-6
LOG2E = 1.4426950408889634
QK_SCALE = HEAD_DIM ** -0.5 * LOG2E
SUM_ROWS = 16
POOL_HALO = 16
LANES = 128
HEADS_PER_GROUP = LANES // HEAD_DIM
GROUPS_PER_STEP = 2
HEADS_PER_STEP = GROUPS_PER_STEP * HEADS_PER_GROUP
ITEMS_PER_TRIP = 2
STAGE_SLOTS = 2
ACC_BUFFERS = 2

ROW_TILE_IN = 2048
ROW_SUBTILE_IN = 512
ROW_TILE_FFN = 1024
ROW_SUBTILE_FFN = 512
FF_CHUNK = 1024
WEIGHT_STAGE_ELEMS = 512 * 1024
VMEM_LIMIT = 56 * 1024 * 1024

NT_DIMS = (((1,), (1,)), ((), ()))
TN_DIMS = (((0,), (0,)), ((), ()))


def _rms_norm(x, g):
    ms = jnp.mean(x * x, axis=-1, keepdims=True)
    return x * lax.rsqrt(ms + EPS) * g


def _inproj_kernel(x_ref, g_ref, wi_hbm, wpool_ref, pscale_ref,
                   qt_ref, k_ref, vt_ref, kmean_ref, p_ref,
                   halo_ref, wi_ref, wqvt_ref, stage_ref, stage_sem):
    t = pl.program_id(1)
    rows = x_ref.shape[1]
    sub = ROW_SUBTILE_IN
    nblk = sub // MOBA_BLOCK
    u_cols = slice(3 * ATTN_WIDTH, 3 * ATTN_WIDTH + POOL_WIDTH)

    @pl.when((pl.program_id(0) == 0) & (t == 0))
    def _():
        chunk = stage_ref.shape[1]
        n_chunks = wi_hbm.shape[0] // chunk

        def copy(n):
            return pltpu.make_async_copy(wi_hbm.at[pl.ds(n * chunk, chunk), :], stage_ref.at[n % 2],
                                         stage_sem.at[n % 2])

        copy(0).start()
        for n in range(n_chunks):
            if n + 1 < n_chunks:
                copy(n + 1).start()
            copy(n).wait()
            wi_ref[n * chunk:(n + 1) * chunk, :] = stage_ref[n % 2].astype(BF16)
        wqvt_ref[:ATTN_WIDTH, :] = wi_ref[:, :ATTN_WIDTH].T
        wqvt_ref[ATTN_WIDTH:, :] = wi_ref[:, 2 * ATTN_WIDTH:3 * ATTN_WIDTH].T

    @pl.when(t == 0)
    def _():
        halo_ref[...] = jnp.zeros_like(halo_ref)

    def project(i):
        r0 = i * sub
        h = _rms_norm(x_ref[0, r0:r0 + sub, :], g_ref[...]).astype(BF16)
        kf = jnp.dot(h, wi_ref[:, ATTN_WIDTH:2 * ATTN_WIDTH], preferred_element_type=F32)
        k_ref[0, r0:r0 + sub, :] = kf.astype(BF16)
        for b in range(nblk):
            blk_row = (t * (rows // sub) + i) * nblk + b
            kmean_ref[0, pl.ds(blk_row, 1), :] = (
                jnp.sum(kf[b * MOBA_BLOCK:(b + 1) * MOBA_BLOCK], axis=0, keepdims=True)
                * (1.0 / MOBA_BLOCK))
        qvt = lax.dot_general(wqvt_ref[...], h, NT_DIMS, preferred_element_type=F32)
        for b in range(nblk):
            cols = slice(b * MOBA_BLOCK, (b + 1) * MOBA_BLOCK)
            qt_ref[0, i * nblk + b] = (qvt[:ATTN_WIDTH, cols] * QK_SCALE).astype(BF16)
            vt_ref[0, i * nblk + b] = qvt[ATTN_WIDTH:, cols].astype(BF16)
        return jnp.dot(h, wi_ref[:, u_cols], preferred_element_type=F32)

    def pool(i, u, halo):
        r0 = i * sub
        ext = jnp.concatenate([halo, u], axis=0)
        pos = t * rows + r0 + lax.broadcasted_iota(jnp.int32, (sub, POOL_GROUP), 0)
        for g, w in enumerate(POOL_WINDOWS):
            cols = slice(g * POOL_GROUP, (g + 1) * POOL_GROUP)
            s = ext[:, cols]
            shift = 1
            while shift < w:
                s = s + pltpu.roll(s, shift, axis=0)
                shift *= 2
            cnt = jnp.minimum(pos + 1, w).astype(F32)
            mixed = s[POOL_HALO:, :] / cnt - u[:, cols]
            y = jnp.dot(mixed.astype(BF16), wpool_ref[g], preferred_element_type=F32)
            p_ref[0, r0:r0 + sub, cols] = (y * pscale_ref[:, cols]).astype(BF16)
        return u[sub - POOL_HALO:, :]

    n_sub = rows // sub
    halo = halo_ref[...]
    u_prev = project(0)
    for i in range(1, n_sub):
        u_next = project(i)
        halo = pool(i - 1, u_prev, halo)
        u_prev = u_next
    halo_ref[...] = pool(n_sub - 1, u_prev, halo)


def _inproj_call(x, g, wi, wpool, pscale):
    B, S, D = x.shape
    T = ROW_TILE_IN
    nblk = T // MOBA_BLOCK
    nb = S // MOBA_BLOCK
    const = lambda shape: pl.BlockSpec(shape, lambda b, t: (0,) * len(shape),
                                       pipeline_mode=pl.Buffered(1))
    return pl.pallas_call(
        _inproj_kernel,
        grid=(B, S // T),
        in_specs=[
            pl.BlockSpec((1, T, D), lambda b, t: (b, t, 0)),
            const((1, D)),
            pl.BlockSpec(memory_space=pl.ANY),
            const(wpool.shape),
            const((1, POOL_WIDTH)),
        ],
        out_specs=[
            pl.BlockSpec((1, nblk, ATTN_WIDTH, MOBA_BLOCK), lambda b, t: (b, t, 0, 0)),
            pl.BlockSpec((1, T, ATTN_WIDTH), lambda b, t: (b, t, 0)),
            pl.BlockSpec((1, nblk, ATTN_WIDTH, MOBA_BLOCK), lambda b, t: (b, t, 0, 0)),
            pl.BlockSpec((1, nb, ATTN_WIDTH), lambda b, t: (b, 0, 0)),
            pl.BlockSpec((1, T, POOL_WIDTH), lambda b, t: (b, t, 0)),
        ],
        out_shape=[
            jax.ShapeDtypeStruct((B, nb, ATTN_WIDTH, MOBA_BLOCK), BF16),
            jax.ShapeDtypeStruct((B, S, ATTN_WIDTH), BF16),
            jax.ShapeDtypeStruct((B, nb, ATTN_WIDTH, MOBA_BLOCK), BF16),
            jax.ShapeDtypeStruct((B, nb, ATTN_WIDTH), F32),
            jax.ShapeDtypeStruct((B, S, POOL_WIDTH), BF16),
        ],
        scratch_shapes=[
            pltpu.VMEM((POOL_HALO, POOL_WIDTH), F32),
            pltpu.VMEM(wi.shape, BF16),
            pltpu.VMEM((2 * ATTN_WIDTH, D), BF16),
            pltpu.VMEM((2, WEIGHT_STAGE_ELEMS // wi.shape[1], wi.shape[1]), F32),
            pltpu.SemaphoreType.DMA((2,)),
        ],
        compiler_params=pltpu.CompilerParams(
            dimension_semantics=("arbitrary", "arbitrary"), vmem_limit_bytes=VMEM_LIMIT),
        name="inproj_pool",
    )(x, g, wi, wpool, pscale)


_F_QBLK, _F_JA, _F_JB, _F_A_VALID, _F_B_VALID, _F_FIRST, _F_LAST = range(7)


def _attn_schedule(nb):
    items = []
    for i in range(nb):
        steps = (i + 2) // 2
        for t in range(steps):
            jb = i - 2 * t - 1
            items.append((i, i - 2 * t, max(jb, 0), 1, int(jb >= 0), int(t == 0), int(t == steps - 1)))
    n_items = len(items)
    items += [(nb - 1, 0, 0, 0, 0, 0, 0)] * 2
    return n_items, np.asarray(items, np.int32).T.reshape(-1)


def _attn_kernel(n_items, slopes_ref, sched_ref, qt_ref, k_ref, vt_ref, kmean_ref, bias_ref, o_ref,
                 shift_ref, s_ref, mx_ref, p_ref, alpha_ref, m_ref, acc_ref):
    step = pl.program_id(1)
    nb = kmean_ref.shape[1]
    seq = k_ref.shape[1]
    blk = MOBA_BLOCK
    neg_inf = -jnp.inf
    heads = range(HEADS_PER_STEP)
    groups = [tuple(range(g * HEADS_PER_GROUP, (g + 1) * HEADS_PER_GROUP)) for g in range(GROUPS_PER_STEP)]
    stride = n_items + 2

    def group_lanes(hh):
        g = hh // HEADS_PER_GROUP
        return slice(g * LANES, (g + 1) * LANES)

    def head_weights(hh, rows):
        pad = jnp.zeros_like(rows)
        return jnp.concatenate([rows, pad] if hh % HEADS_PER_GROUP == 0 else [pad, rows], axis=0)

    def head_rows(hh):
        return slice(hh * HEAD_DIM, (hh + 1) * HEAD_DIM)

    def field(f, w):
        return sched_ref[f * stride + w]

    kmean = kmean_ref[0].astype(BF16)
    key_blk = lax.broadcasted_iota(jnp.int32, (nb, seq), 0)
    qry_blk = lax.broadcasted_iota(jnp.int32, (nb, seq), 1) // blk
    dist = ((qry_blk - key_blk) * blk).astype(F32)
    key_idx = key_blk.astype(F32)
    for hh in heads:
        qt_all = jnp.concatenate([qt_ref[0, i, head_rows(hh), :] for i in range(nb)], axis=1)
        gate = jnp.dot(kmean[:, group_lanes(hh)], head_weights(hh, qt_all),
                       preferred_element_type=F32)
        gate = jnp.where(key_blk < qry_blk, gate, neg_inf)
        sel = key_blk == qry_blk
        for _ in range(MOBA_TOPK):
            top = jnp.max(gate, axis=0, keepdims=True)
            first = jnp.min(jnp.where(gate == top, key_idx, float(nb)), axis=0, keepdims=True)
            pick = (key_idx == first) & (top > neg_inf)
            sel = sel | pick
            gate = jnp.where(pick, neg_inf, gate)
        slope = slopes_ref[step * HEADS_PER_STEP + hh]
        shift = jnp.where(sel, slope * dist, jnp.inf)
        for i in range(nb):
            shift_ref[hh, i] = shift[:, i * blk:(i + 1) * blk]

    def score_dot(w, hh):
        i, ja, jb = field(_F_QBLK, w), field(_F_JA, w), field(_F_JB, w)
        kk = jnp.concatenate(
            [k_ref[0, pl.ds(pl.multiple_of(ja * blk, blk), blk), group_lanes(hh)],
             k_ref[0, pl.ds(pl.multiple_of(jb * blk, blk), blk), group_lanes(hh)]], axis=0)
        return jnp.dot(kk, head_weights(hh, qt_ref[0, i, head_rows(hh), :]),
                       preferred_element_type=F32)

    def stage_unit(st, w, slot, hh, u):
        table = field(_F_FIRST, w) if u == 0 else 0
        s_u = st[u * blk:(u + 1) * blk] + bias_ref[hh, table]
        s_ref[slot, hh, u] = s_u
        mx_ref[slot, hh, u] = jnp.max(s_u, axis=0, keepdims=True)

    def stage_scores(sts, w, slot, hhs):
        for hh in hhs:
            for u in range(2):
                stage_unit(sts[hh], w, slot, hh, u)

    def softmax_prep(w, slot, hh):
        i, ja, jb = field(_F_QBLK, w), field(_F_JA, w), field(_F_JB, w)
        a_valid, b_valid = field(_F_A_VALID, w) == 1, field(_F_B_VALID, w) == 1
        first = field(_F_FIRST, w) == 1
        sh_a = jnp.where(a_valid, shift_ref[hh, i, pl.ds(ja, 1), :], jnp.inf)
        sh_b = jnp.where(b_valid, shift_ref[hh, i, pl.ds(jb, 1), :], jnp.inf)
        m_old = jnp.where(first, neg_inf, m_ref[hh])
        m_new = jnp.maximum(m_old, jnp.maximum(mx_ref[slot, hh, 0] - sh_a,
                                               mx_ref[slot, hh, 1] - sh_b))
        m_ref[hh] = m_new
        alpha_ref[slot, hh] = jnp.exp2(m_old - m_new)
        return m_new + sh_a, m_new + sh_b

    def softmax_block(slot, hh, u, offset):
        p_ref[slot, hh, u * blk:(u + 1) * blk] = jnp.exp2(s_ref[slot, hh, u] - offset).astype(BF16)

    def softmax(w, slot):
        for hh in heads:
            offsets = softmax_prep(w, slot, hh)
            for u in range(2):
                softmax_block(slot, hh, u, offsets[u])

    ones_rows = jnp.ones((SUM_ROWS, 2 * blk), BF16)

    def pv_dot(w, slot, hh):
        ja, jb = field(_F_JA, w), field(_F_JB, w)
        rows = slice(hh * HEAD_DIM, (hh + 1) * HEAD_DIM)
        vts = jnp.concatenate([vt_ref[0, ja, rows, :], vt_ref[0, jb, rows, :]], axis=1)
        lhs = jnp.concatenate([vts, ones_rows], axis=0)
        return jnp.dot(lhs, p_ref[slot, hh], preferred_element_type=F32)

    def accumulate(pvs, w, slot):
        par = field(_F_QBLK, w) % ACC_BUFFERS
        for hh in heads:
            acc_ref[par, hh] = alpha_ref[slot, hh] * acc_ref[par, hh] + pvs[hh]

    def finalize(w):
        i = field(_F_QBLK, w)
        par = i % ACC_BUFFERS
        for hh in heads:
            o_ref[0, i, head_rows(hh), :] = (
                acc_ref[par, hh, :HEAD_DIM] / acc_ref[par, hh, HEAD_DIM:HEAD_DIM + 1]).astype(o_ref.dtype)

    acc_ref[...] = jnp.zeros_like(acc_ref)
    m_ref[...] = jnp.zeros_like(m_ref)

    stage_scores([score_dot(0, hh) for hh in heads], 0, 0, heads)
    stage_scores([score_dot(1, hh) for hh in heads], 1, 1, heads)
    softmax(0, 0)

    def item(w, j):
        slot, sm, sc = j % STAGE_SLOTS, (j + 1) % STAGE_SLOTS, (j + 2) % STAGE_SLOTS
        sts, pvs = {}, {}
        for g, hhs in enumerate(groups):
            for hh in hhs:
                sts[hh] = score_dot(w + 2, hh)
            offs = {hh: softmax_prep(w + 1, sm, hh) for hh in hhs}
            if g > 0:
                stage_scores(sts, w + 2, sc, groups[g - 1])
            for hh in hhs:
                for u in range(2):
                    softmax_block(sm, hh, u, offs[hh][u])
        for hh in heads:
            pvs[hh] = pv_dot(w, slot, hh)
        stage_scores(sts, w + 2, sc, groups[-1])
        accumulate(pvs, w, slot)

    def body(trip, carry):
        for j in range(ITEMS_PER_TRIP):
            item(ITEMS_PER_TRIP * trip + j, j)
        for j in range(ITEMS_PER_TRIP):
            w = ITEMS_PER_TRIP * trip + j
            pl.when(field(_F_LAST, w) == 1)(functools.partial(finalize, w))
        return carry

    assert n_items % ITEMS_PER_TRIP == 0 and ITEMS_PER_TRIP % STAGE_SLOTS == 0
    lax.fori_loop(0, n_items // ITEMS_PER_TRIP, body, 0)


def _attn_call(slopes, bias, qt, k, vt, kmean):
    B, S, W = k.shape
    nb = S // MOBA_BLOCK
    step_lanes = GROUPS_PER_STEP * LANES
    n_steps = W // step_lanes
    hps = HEADS_PER_STEP
    n_items, sched = _attn_schedule(nb)
    grid_spec = pltpu.PrefetchScalarGridSpec(
        num_scalar_prefetch=2,
        grid=(B, n_steps),
        in_specs=[
            pl.BlockSpec((1, nb, step_lanes, MOBA_BLOCK), lambda b, p, *_: (b, 0, p, 0)),
            pl.BlockSpec((1, S, step_lanes), lambda b, p, *_: (b, 0, p)),
            pl.BlockSpec((1, nb, step_lanes, MOBA_BLOCK), lambda b, p, *_: (b, 0, p, 0)),
            pl.BlockSpec((1, nb, step_lanes), lambda b, p, *_: (b, 0, p)),
            pl.BlockSpec((hps, 2, MOBA_BLOCK, MOBA_BLOCK), lambda b, p, *_: (p, 0, 0, 0)),
        ],
        out_specs=pl.BlockSpec((1, nb, step_lanes, MOBA_BLOCK), lambda b, p, *_: (b, 0, p, 0)),
        scratch_shapes=[
            pltpu.VMEM((hps, nb, nb, MOBA_BLOCK), F32),
            pltpu.VMEM((STAGE_SLOTS, hps, 2, MOBA_BLOCK, MOBA_BLOCK), F32),
            pltpu.VMEM((STAGE_SLOTS, hps, 2, 1, MOBA_BLOCK), F32),
            pltpu.VMEM((STAGE_SLOTS, hps, 2 * MOBA_BLOCK, MOBA_BLOCK), BF16),
            pltpu.VMEM((STAGE_SLOTS, hps, 1, MOBA_BLOCK), F32),
            pltpu.VMEM((hps, 1, MOBA_BLOCK), F32),
            pltpu.VMEM((ACC_BUFFERS, hps, HEAD_DIM + SUM_ROWS, MOBA_BLOCK), F32),
        ],
    )
    return pl.pallas_call(
        functools.partial(_attn_kernel, n_items),
        grid_spec=grid_spec,
        out_shape=jax.ShapeDtypeStruct((B, nb, W, MOBA_BLOCK), BF16),
        compiler_params=pltpu.CompilerParams(
            dimension_semantics=("arbitrary", "arbitrary"), vmem_limit_bytes=VMEM_LIMIT),
        name="moba_attn",
    )(slopes, jnp.asarray(sched), qt, k, vt, kmean, bias)


def _ffn_kernel(x_ref, at_ref, p_ref, wo_hbm, g2_ref, wup_hbm, wdn_hbm, g3_ref, o_ref,
                wo_ref, wup_ref, wdn_ref, stage_ref, stage_sem):
    sub = ROW_SUBTILE_FFN
    n_sub = x_ref.shape[0] // sub
    n_chunks = wup_ref.shape[1] // FF_CHUNK
    stage_rows = stage_ref.shape[1]

    jobs = [(wo_hbm.at[pl.ds(r0, stage_rows), :], wo_ref.at[pl.ds(r0, stage_rows), :])
            for r0 in range(0, wo_hbm.shape[0], stage_rows)]
    for c in range(n_chunks):
        c0 = c * FF_CHUNK
        jobs += [(wup_hbm.at[pl.ds(r0, stage_rows), pl.ds(c0, FF_CHUNK)],
                  wup_ref.at[pl.ds(r0, stage_rows), pl.ds(c0, FF_CHUNK)])
                 for r0 in range(0, wup_hbm.shape[0], stage_rows)]
        jobs += [(wdn_hbm.at[pl.ds(c0 + r0, stage_rows), :], wdn_ref.at[pl.ds(c0 + r0, stage_rows), :])
                 for r0 in range(0, FF_CHUNK, stage_rows)]
    n_wo_jobs = wo_hbm.shape[0] // stage_rows
    jobs_per_chunk = (len(jobs) - n_wo_jobs) // n_chunks

    def copy(n):
        return pltpu.make_async_copy(jobs[n][0], stage_ref.at[n % 2], stage_sem.at[n % 2])

    def land_weights(lo, hi):
        @pl.when(pl.program_id(0) == 0)
        def _():
            if lo == 0:
                copy(0).start()
            for n in range(lo, hi):
                if n + 1 < len(jobs):
                    copy(n + 1).start()
                copy(n).wait()
                jobs[n][1][...] = stage_ref[n % 2].astype(BF16)

    def head(i):
        rows = slice(i * sub, (i + 1) * sub)
        nblk = sub // MOBA_BLOCK
        attn = jnp.concatenate(
            [lax.dot_general(at_ref[i * nblk + b], wo_ref[:ATTN_WIDTH, :], TN_DIMS,
                             preferred_element_type=F32) for b in range(nblk)], axis=0)
        x1 = (x_ref[rows, :] + attn
              + jnp.dot(p_ref[rows, :], wo_ref[ATTN_WIDTH:, :], preferred_element_type=F32))
        o_ref[rows, :] = x1
        return _rms_norm(x1, g2_ref[...]).astype(BF16)

    def ffn(i, h):
        rows = slice(i * sub, (i + 1) * sub)
        for c in range(n_chunks):
            if i == 0:
                land_weights(n_wo_jobs + c * jobs_per_chunk, n_wo_jobs + (c + 1) * jobs_per_chunk)
            cols = slice(c * FF_CHUNK, (c + 1) * FF_CHUNK)
            up = jnp.dot(h, wup_ref[:, cols], preferred_element_type=F32)
            act = jnp.square(jnp.maximum(up, 0.0)).astype(BF16)
            o_ref[rows, :] += jnp.dot(act, wdn_ref[cols, :], preferred_element_type=F32)

    def tail(i):
        rows = slice(i * sub, (i + 1) * sub)
        o_ref[rows, :] = _rms_norm(o_ref[rows, :], g3_ref[...])

    land_weights(0, n_wo_jobs)
    h_prev = head(0)
    for i in range(1, n_sub):
        h_next = head(i)
        ffn(i - 1, h_prev)
        tail(i - 1)
        h_prev = h_next
    ffn(n_sub - 1, h_prev)
    tail(n_sub - 1)


def _ffn_call(x, at, p, wo, g2, wup, wdn, g3):
    N, D = x.shape
    T = ROW_TILE_FFN
    const = lambda shape: pl.BlockSpec(shape, lambda t: (0,) * len(shape),
                                       pipeline_mode=pl.Buffered(1))
    in_hbm = pl.BlockSpec(memory_space=pl.ANY)
    d_ff = wup.shape[1]
    assert wup.shape == (D, d_ff) and wdn.shape == (d_ff, D) and wo.shape == (D, D) and D == FF_CHUNK
    return pl.pallas_call(
        _ffn_kernel,
        grid=(N // T,),
        in_specs=[
            pl.BlockSpec((T, D), lambda t: (t, 0)),
            pl.BlockSpec((T // MOBA_BLOCK, ATTN_WIDTH, MOBA_BLOCK), lambda t: (t, 0, 0)),
            pl.BlockSpec((T, POOL_WIDTH), lambda t: (t, 0)),
            in_hbm, const((1, D)),
            in_hbm, in_hbm, const((1, D)),
        ],
        out_specs=pl.BlockSpec((T, D), lambda t: (t, 0)),
        out_shape=jax.ShapeDtypeStruct((N, D), F32),
        scratch_shapes=[
            pltpu.VMEM(wo.shape, BF16), pltpu.VMEM(wup.shape, BF16), pltpu.VMEM(wdn.shape, BF16),
            pltpu.VMEM((2, WEIGHT_STAGE_ELEMS // FF_CHUNK, FF_CHUNK), F32),
            pltpu.SemaphoreType.DMA((2,)),
        ],
        compiler_params=pltpu.CompilerParams(
            dimension_semantics=("arbitrary",), vmem_limit_bytes=VMEM_LIMIT),
        name="outproj_ffn",
    )(x, at, p, wo, g2, wup, wdn, g3)


def _alibi_constants():
    slopes = (2.0 ** (-8.0 * np.arange(1, N_HEADS + 1) / N_HEADS) * LOG2E).astype(np.float32)
    key = np.arange(MOBA_BLOCK, dtype=np.float32)[:, None]
    qry = np.arange(MOBA_BLOCK, dtype=np.float32)[None, :]
    past = -slopes[:, None, None] * (qry - key)[None]
    own = np.where((key <= qry)[None], past, -np.inf).astype(np.float32)
    return jnp.asarray(slopes), jnp.asarray(np.stack([past, own], axis=1))


def kernel(x, norm_mix, w_in, w_pool, pool_scale, w_out, norm_mlp, w_up, w_down, norm_final):
    B, S, D = x.shape
    assert w_in.shape[0] == 1, "single trunk layer (the final norm is fused into the FFN call)"
    assert S % ROW_TILE_IN == 0 and (B * S) % ROW_TILE_FFN == 0 and S % MOBA_BLOCK == 0
    assert w_in.shape[2] == 3 * ATTN_WIDTH + POOL_WIDTH and D == ATTN_WIDTH + POOL_WIDTH
    qt, k, vt, kmean, p = _inproj_call(
        x, norm_mix[0][None, :], w_in[0], w_pool[0].astype(BF16), pool_scale[0][None, :])
    slopes, bias = _alibi_constants()
    at = _attn_call(slopes, bias, qt, k, vt, kmean)
    y = _ffn_call(x.reshape(B * S, D), at.reshape(B * (S // MOBA_BLOCK), ATTN_WIDTH, MOBA_BLOCK),
                  p.reshape(B * S, POOL_WIDTH), w_out[0],
                  norm_mlp[0][None, :], w_up[0], w_down[0], norm_final[None, :])
    return y.reshape(B, S, D)
```

```python
import functools

import jax
import jax.numpy as jnp
import numpy as np
from jax import lax
from jax.experimental import pallas as pl
from jax.experimental.pallas import tpu as pltpu

F32 = jnp.float32
BF16 = jnp.bfloat16

N_HEADS = 8
HEAD_DIM = 64
ATTN_WIDTH = N_HEADS * HEAD_DIM
POOL_WINDOWS = (2, 4, 8, 16)
POOL_GROUP = 128
POOL_WIDTH = POOL_GROUP * len(POOL_WINDOWS)
MOBA_BLOCK = 256
MOBA_TOPK = 3
EPS = 1e-6
LOG2E = 1.4426950408889634
QK_SCALE = HEAD_DIM ** -0.5 * LOG2E
SUM_ROWS = 16
POOL_HALO = 16
LANES = 128
HEADS_PER_GROUP = LANES // HEAD_DIM
GROUPS_PER_STEP = 2
HEADS_PER_STEP = GROUPS_PER_STEP * HEADS_PER_GROUP
ITEMS_PER_TRIP = 2
STAGE_SLOTS = 2
ACC_BUFFERS = 2

ROW_TILE_IN = 2048
ROW_SUBTILE_IN = 512
ROW_TILE_FFN = 1024
ROW_SUBTILE_FFN = 512
FF_CHUNK = 1024
WEIGHT_STAGE_ELEMS = 512 * 1024
VMEM_LIMIT = 56 * 1024 * 1024

NT_DIMS = (((1,), (1,)), ((), ()))
TN_DIMS = (((0,), (0,)), ((), ()))


def _rms_norm(x, g):
    ms = jnp.mean(x * x, axis=-1, keepdims=True)
    return x * lax.rsqrt(ms + EPS) * g


def _inproj_kernel(x_ref, g_ref, wi_hbm, wpool_ref, pscale_ref,
                   qt_ref, k_ref, vt_ref, kmean_ref, p_ref,
                   halo_ref, wi_ref, wqvt_ref, stage_ref, stage_sem):
    t = pl.program_id(1)
    rows = x_ref.shape[1]
    sub = ROW_SUBTILE_IN
    nblk = sub // MOBA_BLOCK
    u_cols = slice(3 * ATTN_WIDTH, 3 * ATTN_WIDTH + POOL_WIDTH)

    @pl.when((pl.program_id(0) == 0) & (t == 0))
    def _():
        chunk = stage_ref.shape[1]
        n_chunks = wi_hbm.shape[0] // chunk

        def copy(n):
            return pltpu.make_async_copy(wi_hbm.at[pl.ds(n * chunk, chunk), :], stage_ref.at[n % 2],
                                         stage_sem.at[n % 2])

        copy(0).start()
        for n in range(n_chunks):
            if n + 1 < n_chunks:
                copy(n + 1).start()
            copy(n).wait()
            wi_ref[n * chunk:(n + 1) * chunk, :] = stage_ref[n % 2].astype(BF16)
        wqvt_ref[:ATTN_WIDTH, :] = wi_ref[:, :ATTN_WIDTH].T
        wqvt_ref[ATTN_WIDTH:, :] = wi_ref[:, 2 * ATTN_WIDTH:3 * ATTN_WIDTH].T

    @pl.when(t == 0)
    def _():
        halo_ref[...] = jnp.zeros_like(halo_ref)

    def project(i):
        r0 = i * sub
        h = _rms_norm(x_ref[0, r0:r0 + sub, :], g_ref[...]).astype(BF16)
        kf = jnp.dot(h, wi_ref[:, ATTN_WIDTH:2 * ATTN_WIDTH], preferred_element_type=F32)
        k_ref[0, r0:r0 + sub, :] = kf.astype(BF16)
        for b in range(nblk):
            blk_row = (t * (rows // sub) + i) * nblk + b
            kmean_ref[0, pl.ds(blk_row, 1), :] = (
                jnp.sum(kf[b * MOBA_BLOCK:(b + 1) * MOBA_BLOCK], axis=0, keepdims=True)
                * (1.0 / MOBA_BLOCK))
        qvt = lax.dot_general(wqvt_ref[...], h, NT_DIMS, preferred_element_type=F32)
        for b in range(nblk):
            cols = slice(b * MOBA_BLOCK, (b + 1) * MOBA_BLOCK)
            qt_ref[0, i * nblk + b] = (qvt[:ATTN_WIDTH, cols] * QK_SCALE).astype(BF16)
            vt_ref[0, i * nblk + b] = qvt[ATTN_WIDTH:, cols].astype(BF16)
        return jnp.dot(h, wi_ref[:, u_cols], preferred_element_type=F32)

    def pool(i, u, halo):
        r0 = i * sub
        ext = jnp.concatenate([halo, u], axis=0)
        pos = t * rows + r0 + lax.broadcasted_iota(jnp.int32, (sub, POOL_GROUP), 0)
        for g, w in enumerate(POOL_WINDOWS):
            cols = slice(g * POOL_GROUP, (g + 1) * POOL_GROUP)
            s = ext[:, cols]
            shift = 1
            while shift < w:
                s = s + pltpu.roll(s, shift, axis=0)
                shift *= 2
            cnt = jnp.minimum(pos + 1, w).astype(F32)
            mixed = s[POOL_HALO:, :] / cnt - u[:, cols]
            y = jnp.dot(mixed.astype(BF16), wpool_ref[g].astype(BF16), preferred_element_type=F32)
            p_ref[0, r0:r0 + sub, cols] = (y * pscale_ref[:, cols]).astype(BF16)
        return u[sub - POOL_HALO:, :]

    n_sub = rows // sub
    halo = halo_ref[...]
    u_prev = project(0)
    for i in range(1, n_sub):
        u_next = project(i)
        halo = pool(i - 1, u_prev, halo)
        u_prev = u_next
    halo_ref[...] = pool(n_sub - 1, u_prev, halo)


def _inproj_call(x, g, wi, wpool, pscale):
    B, S, D = x.shape
    T = ROW_TILE_IN
    nblk = T // MOBA_BLOCK
    nb = S // MOBA_BLOCK
    const = lambda shape: pl.BlockSpec(shape, lambda b, t: (0,) * len(shape),
                                       pipeline_mode=pl.Buffered(1))
    return pl.pallas_call(
        _inproj_kernel,
        grid=(B, S // T),
        in_specs=[
            pl.BlockSpec((1, T, D), lambda b, t: (b, t, 0)),
            const((1, D)),
            pl.BlockSpec(memory_space=pl.ANY),
            const(wpool.shape),
            const((1, POOL_WIDTH)),
        ],
        out_specs=[
            pl.BlockSpec((1, nblk, ATTN_WIDTH, MOBA_BLOCK), lambda b, t: (b, t, 0, 0)),
            pl.BlockSpec((1, T, ATTN_WIDTH), lambda b, t: (b, t, 0)),
            pl.BlockSpec((1, nblk, ATTN_WIDTH, MOBA_BLOCK), lambda b, t: (b, t, 0, 0)),
            pl.BlockSpec((1, nb, ATTN_WIDTH), lambda b, t: (b, 0, 0)),
            pl.BlockSpec((1, T, POOL_WIDTH), lambda b, t: (b, t, 0)),
        ],
        out_shape=[
            jax.ShapeDtypeStruct((B, nb, ATTN_WIDTH, MOBA_BLOCK), BF16),
            jax.ShapeDtypeStruct((B, S, ATTN_WIDTH), BF16),
            jax.ShapeDtypeStruct((B, nb, ATTN_WIDTH, MOBA_BLOCK), BF16),
            jax.ShapeDtypeStruct((B, nb, ATTN_WIDTH), F32),
            jax.ShapeDtypeStruct((B, S, POOL_WIDTH), BF16),
        ],
        scratch_shapes=[
            pltpu.VMEM((POOL_HALO, POOL_WIDTH), F32),
            pltpu.VMEM(wi.shape, BF16),
            pltpu.VMEM((2 * ATTN_WIDTH, D), BF16),
            pltpu.VMEM((2, WEIGHT_STAGE_ELEMS // wi.shape[1], wi.shape[1]), F32),
            pltpu.SemaphoreType.DMA((2,)),
        ],
        compiler_params=pltpu.CompilerParams(
            dimension_semantics=("arbitrary", "arbitrary"), vmem_limit_bytes=VMEM_LIMIT),
        name="inproj_pool",
    )(x, g, wi, wpool, pscale)


_F_QBLK, _F_JA, _F_JB, _F_A_VALID, _F_B_VALID, _F_FIRST, _F_LAST = range(7)


def _attn_schedule(nb):
    items = []
    for i in range(nb):
        steps = (i + 2) // 2
        for t in range(steps):
            jb = i - 2 * t - 1
            items.append((i, i - 2 * t, max(jb, 0), 1, int(jb >= 0), int(t == 0), int(t == steps - 1)))
    n_items = len(items)
    items += [(nb - 1, 0, 0, 0, 0, 0, 0)] * 2
    return n_items, np.asarray(items, np.int32).T.reshape(-1)


def _attn_kernel(n_items, slopes_ref, sched_ref, qt_ref, k_ref, vt_ref, kmean_ref, bias_ref, o_ref,
                 shift_ref, s_ref, mx_ref, p_ref, alpha_ref, m_ref, acc_ref):
    step = pl.program_id(1)
    nb = kmean_ref.shape[1]
    seq = k_ref.shape[1]
    blk = MOBA_BLOCK
    neg_inf = -jnp.inf
    heads = range(HEADS_PER_STEP)
    groups = [tuple(range(g * HEADS_PER_GROUP, (g + 1) * HEADS_PER_GROUP)) for g in range(GROUPS_PER_STEP)]
    stride = n_items + 2

    def group_lanes(hh):
        g = hh // HEADS_PER_GROUP
        return slice(g * LANES, (g + 1) * LANES)

    def head_weights(hh, rows):
        pad = jnp.zeros_like(rows)
        return jnp.concatenate([rows, pad] if hh % HEADS_PER_GROUP == 0 else [pad, rows], axis=0)

    def head_rows(hh):
        return slice(hh * HEAD_DIM, (hh + 1) * HEAD_DIM)

    def field(f, w):
        return sched_ref[f * stride + w]

    kmean = kmean_ref[0].astype(BF16)
    key_blk = lax.broadcasted_iota(jnp.int32, (nb, seq), 0)
    qry_blk = lax.broadcasted_iota(jnp.int32, (nb, seq), 1) // blk
    dist = ((qry_blk - key_blk) * blk).astype(F32)
    key_idx = key_blk.astype(F32)
    for hh in heads:
        qt_all = jnp.concatenate([qt_ref[0, i, head_rows(hh), :] for i in range(nb)], axis=1)
        gate = jnp.dot(kmean[:, group_lanes(hh)], head_weights(hh, qt_all),
                       preferred_element_type=F32)
        gate = jnp.where(key_blk < qry_blk, gate, neg_inf)
        sel = key_blk == qry_blk
        for _ in range(MOBA_TOPK):
            top = jnp.max(gate, axis=0, keepdims=True)
            first = jnp.min(jnp.where(gate == top, key_idx, float(nb)), axis=0, keepdims=True)
            pick = (key_idx == first) & (top > neg_inf)
            sel = sel | pick
            gate = jnp.where(pick, neg_inf, gate)
        slope = slopes_ref[step * HEADS_PER_STEP + hh]
        shift = jnp.where(sel, slope * dist, jnp.inf)
        for i in range(nb):
            shift_ref[hh, i] = shift[:, i * blk:(i + 1) * blk]

    def score_dot(w, hh):
        i, ja, jb = field(_F_QBLK, w), field(_F_JA, w), field(_F_JB, w)
        kk = jnp.concatenate(
            [k_ref[0, pl.ds(pl.multiple_of(ja * blk, blk), blk), group_lanes(hh)],
             k_ref[0, pl.ds(pl.multiple_of(jb * blk, blk), blk), group_lanes(hh)]], axis=0)
        return jnp.dot(kk, head_weights(hh, qt_ref[0, i, head_rows(hh), :]),
                       preferred_element_type=F32)

    def stage_unit(st, w, slot, hh, u):
        table = field(_F_FIRST, w) if u == 0 else 0
        s_u = st[u * blk:(u + 1) * blk] + bias_ref[hh, table]
        s_ref[slot, hh, u] = s_u
        mx_ref[slot, hh, u] = jnp.max(s_u, axis=0, keepdims=True)

    def stage_scores(sts, w, slot, hhs):
        for hh in hhs:
            for u in range(2):
                stage_unit(sts[hh], w, slot, hh, u)

    def softmax_prep(w, slot, hh):
        i, ja, jb = field(_F_QBLK, w), field(_F_JA, w), field(_F_JB, w)
        a_valid, b_valid = field(_F_A_VALID, w) == 1, field(_F_B_VALID, w) == 1
        first = field(_F_FIRST, w) == 1
        sh_a = jnp.where(a_valid, shift_ref[hh, i, pl.ds(ja, 1), :], jnp.inf)
        sh_b = jnp.where(b_valid, shift_ref[hh, i, pl.ds(jb, 1), :], jnp.inf)
        m_old = jnp.where(first, neg_inf, m_ref[hh])
        m_new = jnp.maximum(m_old, jnp.maximum(mx_ref[slot, hh, 0] - sh_a,
                                               mx_ref[slot, hh, 1] - sh_b))
        m_ref[hh] = m_new
        alpha_ref[slot, hh] = jnp.exp2(m_old - m_new)
        return m_new + sh_a, m_new + sh_b

    def softmax_block(slot, hh, u, offset):
        p_ref[slot, hh, u * blk:(u + 1) * blk] = jnp.exp2(s_ref[slot, hh, u] - offset).astype(BF16)

    def softmax(w, slot):
        for hh in heads:
            offsets = softmax_prep(w, slot, hh)
            for u in range(2):
                softmax_block(slot, hh, u, offsets[u])

    ones_rows = jnp.ones((SUM_ROWS, 2 * blk), BF16)

    def pv_dot(w, slot, hh):
        ja, jb = field(_F_JA, w), field(_F_JB, w)
        rows = slice(hh * HEAD_DIM, (hh + 1) * HEAD_DIM)
        vts = jnp.concatenate([vt_ref[0, ja, rows, :], vt_ref[0, jb, rows, :]], axis=1)
        lhs = jnp.concatenate([vts, ones_rows], axis=0)
        return jnp.dot(lhs, p_ref[slot, hh], preferred_element_type=F32)

    def accumulate(pvs, w, slot):
        par = field(_F_QBLK, w) % ACC_BUFFERS
        for hh in heads:
            acc_ref[par, hh] = alpha_ref[slot, hh] * acc_ref[par, hh] + pvs[hh]

    def finalize(w):
        i = field(_F_QBLK, w)
        par = i % ACC_BUFFERS
        for hh in heads:
            o_ref[0, i, head_rows(hh), :] = (
                acc_ref[par, hh, :HEAD_DIM] / acc_ref[par, hh, HEAD_DIM:HEAD_DIM + 1]).astype(o_ref.dtype)

    acc_ref[...] = jnp.zeros_like(acc_ref)
    m_ref[...] = jnp.zeros_like(m_ref)

    stage_scores([score_dot(0, hh) for hh in heads], 0, 0, heads)
    stage_scores([score_dot(1, hh) for hh in heads], 1, 1, heads)
    softmax(0, 0)

    def item(w, j):
        slot, sm, sc = j % STAGE_SLOTS, (j + 1) % STAGE_SLOTS, (j + 2) % STAGE_SLOTS
        sts, pvs = {}, {}
        for g, hhs in enumerate(groups):
            for hh in hhs:
                sts[hh] = score_dot(w + 2, hh)
            offs = {hh: softmax_prep(w + 1, sm, hh) for hh in hhs}
            if g > 0:
                stage_scores(sts, w + 2, sc, groups[g - 1])
            for hh in hhs:
                for u in range(2):
                    softmax_block(sm, hh, u, offs[hh][u])
        for hh in heads:
            pvs[hh] = pv_dot(w, slot, hh)
        stage_scores(sts, w + 2, sc, groups[-1])
        accumulate(pvs, w, slot)

    def body(trip, carry):
        for j in range(ITEMS_PER_TRIP):
            item(ITEMS_PER_TRIP * trip + j, j)
        for j in range(ITEMS_PER_TRIP):
            w = ITEMS_PER_TRIP * trip + j
            pl.when(field(_F_LAST, w) == 1)(functools.partial(finalize, w))
        return carry

    assert n_items % ITEMS_PER_TRIP == 0 and ITEMS_PER_TRIP % STAGE_SLOTS == 0
    lax.fori_loop(0, n_items // ITEMS_PER_TRIP, body, 0)


def _attn_call(slopes, bias, qt, k, vt, kmean):
    B, S, W = k.shape
    nb = S // MOBA_BLOCK
    step_lanes = GROUPS_PER_STEP * LANES
    n_steps = W // step_lanes
    hps = HEADS_PER_STEP
    n_items, sched = _attn_schedule(nb)
    grid_spec = pltpu.PrefetchScalarGridSpec(
        num_scalar_prefetch=2,
        grid=(B, n_steps),
        in_specs=[
            pl.BlockSpec((1, nb, step_lanes, MOBA_BLOCK), lambda b, p, *_: (b, 0, p, 0)),
            pl.BlockSpec((1, S, step_lanes), lambda b, p, *_: (b, 0, p)),
            pl.BlockSpec((1, nb, step_lanes, MOBA_BLOCK), lambda b, p, *_: (b, 0, p, 0)),
            pl.BlockSpec((1, nb, step_lanes), lambda b, p, *_: (b, 0, p)),
            pl.BlockSpec((hps, 2, MOBA_BLOCK, MOBA_BLOCK), lambda b, p, *_: (p, 0, 0, 0)),
        ],
        out_specs=pl.BlockSpec((1, nb, step_lanes, MOBA_BLOCK), lambda b, p, *_: (b, 0, p, 0)),
        scratch_shapes=[
            pltpu.VMEM((hps, nb, nb, MOBA_BLOCK), F32),
            pltpu.VMEM((STAGE_SLOTS, hps, 2, MOBA_BLOCK, MOBA_BLOCK), F32),
            pltpu.VMEM((STAGE_SLOTS, hps, 2, 1, MOBA_BLOCK), F32),
            pltpu.VMEM((STAGE_SLOTS, hps, 2 * MOBA_BLOCK, MOBA_BLOCK), BF16),
            pltpu.VMEM((STAGE_SLOTS, hps, 1, MOBA_BLOCK), F32),
            pltpu.VMEM((hps, 1, MOBA_BLOCK), F32),
            pltpu.VMEM((ACC_BUFFERS, hps, HEAD_DIM + SUM_ROWS, MOBA_BLOCK), F32),
        ],
    )
    return pl.pallas_call(
        functools.partial(_attn_kernel, n_items),
        grid_spec=grid_spec,
        out_shape=jax.ShapeDtypeStruct((B, nb, W, MOBA_BLOCK), BF16),
        compiler_params=pltpu.CompilerParams(
            dimension_semantics=("arbitrary", "arbitrary"), vmem_limit_bytes=VMEM_LIMIT),
        name="moba_attn",
    )(slopes, jnp.asarray(sched), qt, k, vt, kmean, bias)


def _ffn_kernel(x_ref, at_ref, p_ref, wo_hbm, g2_ref, wup_hbm, wdn_hbm, g3_ref, o_ref,
                wo_ref, wup_ref, wdn_ref, wide_stage, narrow_stage, stage_sem):
    sub = ROW_SUBTILE_FFN
    n_sub = x_ref.shape[0] // sub

    @pl.when(pl.program_id(0) == 0)
    def _():
        jobs = []
        for src, dst in ((wo_hbm, wo_ref), (wup_hbm, wup_ref), (wdn_hbm, wdn_ref)):
            stage = wide_stage if src.shape[1] == wide_stage.shape[2] else narrow_stage
            assert src.shape[1] == stage.shape[2] and src.shape[0] % stage.shape[1] == 0
            jobs += [(src, dst, stage, r0) for r0 in range(0, src.shape[0], stage.shape[1])]

        def copy(n):
            src, _, stage, r0 = jobs[n]
            return pltpu.make_async_copy(src.at[pl.ds(r0, stage.shape[1]), :], stage.at[n % 2],
                                         stage_sem.at[n % 2])

        copy(0).start()
        for n, (_, dst, stage, r0) in enumerate(jobs):
            if n + 1 < len(jobs):
                copy(n + 1).start()
            copy(n).wait()
            dst[r0:r0 + stage.shape[1], :] = stage[n % 2].astype(BF16)

    def head(i):
        rows = slice(i * sub, (i + 1) * sub)
        nblk = sub // MOBA_BLOCK
        attn = jnp.concatenate(
            [lax.dot_general(at_ref[i * nblk + b], wo_ref[:ATTN_WIDTH, :], TN_DIMS,
                             preferred_element_type=F32) for b in range(nblk)], axis=0)
        x1 = (x_ref[rows, :] + attn
              + jnp.dot(p_ref[rows, :], wo_ref[ATTN_WIDTH:, :], preferred_element_type=F32))
        o_ref[rows, :] = x1
        return _rms_norm(x1, g2_ref[...]).astype(BF16)

    def ffn(i, h):
        rows = slice(i * sub, (i + 1) * sub)
        for c in range(wup_ref.shape[1] // FF_CHUNK):
            cols = slice(c * FF_CHUNK, (c + 1) * FF_CHUNK)
            up = jnp.dot(h, wup_ref[:, cols], preferred_element_type=F32)
            act = jnp.square(jnp.maximum(up, 0.0)).astype(BF16)
            o_ref[rows, :] += jnp.dot(act, wdn_ref[cols, :], preferred_element_type=F32)

    def tail(i):
        rows = slice(i * sub, (i + 1) * sub)
        o_ref[rows, :] = _rms_norm(o_ref[rows, :], g3_ref[...])

    h_prev = head(0)
    for i in range(1, n_sub):
        h_next = head(i)
        ffn(i - 1, h_prev)
        tail(i - 1)
        h_prev = h_next
    ffn(n_sub - 1, h_prev)
    tail(n_sub - 1)


def _ffn_call(x, at, p, wo, g2, wup, wdn, g3):
    N, D = x.shape
    T = ROW_TILE_FFN
    const = lambda shape: pl.BlockSpec(shape, lambda t: (0,) * len(shape),
                                       pipeline_mode=pl.Buffered(1))
    in_hbm = pl.BlockSpec(memory_space=pl.ANY)
    d_ff = wup.shape[1]
    assert wup.shape == (D, d_ff) and wdn.shape == (d_ff, D) and wo.shape == (D, D)
    return pl.pallas_call(
        _ffn_kernel,
        grid=(N // T,),
        in_specs=[
            pl.BlockSpec((T, D), lambda t: (t, 0)),
            pl.BlockSpec((T // MOBA_BLOCK, ATTN_WIDTH, MOBA_BLOCK), lambda t: (t, 0, 0)),
            pl.BlockSpec((T, POOL_WIDTH), lambda t: (t, 0)),
            in_hbm, const((1, D)),
            in_hbm, in_hbm, const((1, D)),
        ],
        out_specs=pl.BlockSpec((T, D), lambda t: (t, 0)),
        out_shape=jax.ShapeDtypeStruct((N, D), F32),
        scratch_shapes=[
            pltpu.VMEM(wo.shape, BF16), pltpu.VMEM(wup.shape, BF16), pltpu.VMEM(wdn.shape, BF16),
            pltpu.VMEM((2, WEIGHT_STAGE_ELEMS // d_ff, d_ff), F32),
            pltpu.VMEM((2, WEIGHT_STAGE_ELEMS // D, D), F32),
            pltpu.SemaphoreType.DMA((2,)),
        ],
        compiler_params=pltpu.CompilerParams(
            dimension_semantics=("arbitrary",), vmem_limit_bytes=VMEM_LIMIT),
        name="outproj_ffn",
    )(x, at, p, wo, g2, wup, wdn, g3)


def _alibi_constants():
    slopes = (2.0 ** (-8.0 * np.arange(1, N_HEADS + 1) / N_HEADS) * LOG2E).astype(np.float32)
    key = np.arange(MOBA_BLOCK, dtype=np.float32)[:, None]
    qry = np.arange(MOBA_BLOCK, dtype=np.float32)[None, :]
    past = -slopes[:, None, None] * (qry - key)[None]
    own = np.where((key <= qry)[None], past, -np.inf).astype(np.float32)
    return jnp.asarray(slopes), jnp.asarray(np.stack([past, own], axis=1))


def kernel(x, norm_mix, w_in, w_pool, pool_scale, w_out, norm_mlp, w_up, w_down, norm_final):
    B, S, D = x.shape
    assert w_in.shape[0] == 1, "single trunk layer (the final norm is fused into the FFN call)"
    assert S % ROW_TILE_IN == 0 and (B * S) % ROW_TILE_FFN == 0 and S % MOBA_BLOCK == 0
    assert w_in.shape[2] == 3 * ATTN_WIDTH + POOL_WIDTH and D == ATTN_WIDTH + POOL_WIDTH
    qt, k, vt, kmean, p = _inproj_call(
        x, norm_mix[0][None, :], w_in[0], w_pool[0], pool_scale[0][None, :])
    slopes, bias = _alibi_constants()
    at = _attn_call(slopes, bias, qt, k, vt, kmean)
    y = _ffn_call(x.reshape(B * S, D), at.reshape(B * (S // MOBA_BLOCK), ATTN_WIDTH, MOBA_BLOCK),
                  p.reshape(B * S, POOL_WIDTH), w_out[0],
                  norm_mlp[0][None, :], w_up[0], w_down[0], norm_final[None, :])
    return y.reshape(B, S, D)
```

```python
import functools

import jax
import jax.numpy as jnp
import numpy as np
from jax import lax
from jax.experimental import pallas as pl
from jax.experimental.pallas import tpu as pltpu

F32 = jnp.float32
BF16 = jnp.bfloat16

N_HEADS = 8
HEAD_DIM = 64
ATTN_WIDTH = N_HEADS * HEAD_DIM
POOL_WINDOWS = (2, 4, 8, 16)
POOL_GROUP = 128
POOL_WIDTH = POOL_GROUP * len(POOL_WINDOWS)
MOBA_BLOCK = 256
MOBA_TOPK = 3
EPS = 1e-6
LOG2E = 1.4426950408889634
QK_SCALE = HEAD_DIM ** -0.5 * LOG2E
SUM_ROWS = 16
POOL_HALO = 16
LANES = 128
HEADS_PER_GROUP = LANES // HEAD_DIM
GROUPS_PER_STEP = 2
HEADS_PER_STEP = GROUPS_PER_STEP * HEADS_PER_GROUP
ITEMS_PER_TRIP = 2
STAGE_SLOTS = 2
ACC_BUFFERS = 2

ROW_TILE_IN = 2048
ROW_SUBTILE_IN = 256
ROW_TILE_FFN = 1024
ROW_SUBTILE_FFN = 512
FF_CHUNK = 1024
WEIGHT_STAGE_ELEMS = 512 * 1024
VMEM_LIMIT = 56 * 1024 * 1024

NT_DIMS = (((1,), (1,)), ((), ()))
TN_DIMS = (((0,), (0,)), ((), ()))


def _rms_norm(x, g):
    ms = jnp.mean(x * x, axis=-1, keepdims=True)
    return x * lax.rsqrt(ms + EPS) * g


def _inproj_kernel(x_ref, g_ref, wi_hbm, wpool_ref, pscale_ref,
                   qt_ref, k_ref, vt_ref, kmean_ref, p_ref,
                   halo_ref, wi_ref, wqvt_ref, stage_ref, stage_sem):
    t = pl.program_id(1)
    rows = x_ref.shape[1]
    sub = ROW_SUBTILE_IN
    nblk = sub // MOBA_BLOCK
    u_cols = slice(3 * ATTN_WIDTH, 3 * ATTN_WIDTH + POOL_WIDTH)

    @pl.when((pl.program_id(0) == 0) & (t == 0))
    def _():
        chunk = stage_ref.shape[1]
        n_chunks = wi_hbm.shape[0] // chunk

        def copy(n):
            return pltpu.make_async_copy(wi_hbm.at[pl.ds(n * chunk, chunk), :], stage_ref.at[n % 2],
                                         stage_sem.at[n % 2])

        copy(0).start()
        for n in range(n_chunks):
            if n + 1 < n_chunks:
                copy(n + 1).start()
            copy(n).wait()
            wi_ref[n * chunk:(n + 1) * chunk, :] = stage_ref[n % 2].astype(BF16)
        wqvt_ref[:ATTN_WIDTH, :] = wi_ref[:, :ATTN_WIDTH].T
        wqvt_ref[ATTN_WIDTH:, :] = wi_ref[:, 2 * ATTN_WIDTH:3 * ATTN_WIDTH].T

    @pl.when(t == 0)
    def _():
        halo_ref[...] = jnp.zeros_like(halo_ref)

    def project(i):
        r0 = i * sub
        h = _rms_norm(x_ref[0, r0:r0 + sub, :], g_ref[...]).astype(BF16)
        kf = jnp.dot(h, wi_ref[:, ATTN_WIDTH:2 * ATTN_WIDTH], preferred_element_type=F32)
        k_ref[0, r0:r0 + sub, :] = kf.astype(BF16)
        for b in range(nblk):
            blk_row = (t * (rows // sub) + i) * nblk + b
            kmean_ref[0, pl.ds(blk_row, 1), :] = (
                jnp.sum(kf[b * MOBA_BLOCK:(b + 1) * MOBA_BLOCK], axis=0, keepdims=True)
                * (1.0 / MOBA_BLOCK))
        qvt = lax.dot_general(wqvt_ref[...], h, NT_DIMS, preferred_element_type=F32)
        for b in range(nblk):
            cols = slice(b * MOBA_BLOCK, (b + 1) * MOBA_BLOCK)
            qt_ref[0, i * nblk + b] = (qvt[:ATTN_WIDTH, cols] * QK_SCALE).astype(BF16)
            vt_ref[0, i * nblk + b] = qvt[ATTN_WIDTH:, cols].astype(BF16)
        return jnp.dot(h, wi_ref[:, u_cols], preferred_element_type=F32)

    def pool(i, u, halo):
        r0 = i * sub
        ext = jnp.concatenate([halo, u], axis=0)
        pos = t * rows + r0 + lax.broadcasted_iota(jnp.int32, (sub, POOL_GROUP), 0)
        for g, w in enumerate(POOL_WINDOWS):
            cols = slice(g * POOL_GROUP, (g + 1) * POOL_GROUP)
            s = ext[:, cols]
            shift = 1
            while shift < w:
                s = s + pltpu.roll(s, shift, axis=0)
                shift *= 2
            cnt = jnp.minimum(pos + 1, w).astype(F32)
            mixed = s[POOL_HALO:, :] / cnt - u[:, cols]
            y = jnp.dot(mixed.astype(BF16), wpool_ref[g].astype(BF16), preferred_element_type=F32)
            p_ref[0, r0:r0 + sub, cols] = (y * pscale_ref[:, cols]).astype(BF16)
        return u[sub - POOL_HALO:, :]

    n_sub = rows // sub
    halo = halo_ref[...]
    u_prev = project(0)
    for i in range(1, n_sub):
        u_next = project(i)
        halo = pool(i - 1, u_prev, halo)
        u_prev = u_next
    halo_ref[...] = pool(n_sub - 1, u_prev, halo)


def _inproj_call(x, g, wi, wpool, pscale):
    B, S, D = x.shape
    T = ROW_TILE_IN
    nblk = T // MOBA_BLOCK
    nb = S // MOBA_BLOCK
    const = lambda shape: pl.BlockSpec(shape, lambda b, t: (0,) * len(shape),
                                       pipeline_mode=pl.Buffered(1))
    return pl.pallas_call(
        _inproj_kernel,
        grid=(B, S // T),
        in_specs=[
            pl.BlockSpec((1, T, D), lambda b, t: (b, t, 0)),
            const((1, D)),
            pl.BlockSpec(memory_space=pl.ANY),
            const(wpool.shape),
            const((1, POOL_WIDTH)),
        ],
        out_specs=[
            pl.BlockSpec((1, nblk, ATTN_WIDTH, MOBA_BLOCK), lambda b, t: (b, t, 0, 0)),
            pl.BlockSpec((1, T, ATTN_WIDTH), lambda b, t: (b, t, 0)),
            pl.BlockSpec((1, nblk, ATTN_WIDTH, MOBA_BLOCK), lambda b, t: (b, t, 0, 0)),
            pl.BlockSpec((1, nb, ATTN_WIDTH), lambda b, t: (b, 0, 0)),
            pl.BlockSpec((1, T, POOL_WIDTH), lambda b, t: (b, t, 0)),
        ],
        out_shape=[
            jax.ShapeDtypeStruct((B, nb, ATTN_WIDTH, MOBA_BLOCK), BF16),
            jax.ShapeDtypeStruct((B, S, ATTN_WIDTH), BF16),
            jax.ShapeDtypeStruct((B, nb, ATTN_WIDTH, MOBA_BLOCK), BF16),
            jax.ShapeDtypeStruct((B, nb, ATTN_WIDTH), F32),
            jax.ShapeDtypeStruct((B, S, POOL_WIDTH), BF16),
        ],
        scratch_shapes=[
            pltpu.VMEM((POOL_HALO, POOL_WIDTH), F32),
            pltpu.VMEM(wi.shape, BF16),
            pltpu.VMEM((2 * ATTN_WIDTH, D), BF16),
            pltpu.VMEM((2, WEIGHT_STAGE_ELEMS // wi.shape[1], wi.shape[1]), F32),
            pltpu.SemaphoreType.DMA((2,)),
        ],
        compiler_params=pltpu.CompilerParams(
            dimension_semantics=("arbitrary", "arbitrary"), vmem_limit_bytes=VMEM_LIMIT),
        name="inproj_pool",
    )(x, g, wi, wpool, pscale)


_F_QBLK, _F_JA, _F_JB, _F_A_VALID, _F_B_VALID, _F_FIRST, _F_LAST = range(7)


def _attn_schedule(nb):
    items = []
    for i in range(nb):
        steps = (i + 2) // 2
        for t in range(steps):
            jb = i - 2 * t - 1
            items.append((i, i - 2 * t, max(jb, 0), 1, int(jb >= 0), int(t == 0), int(t == steps - 1)))
    n_items = len(items)
    items += [(nb - 1, 0, 0, 0, 0, 0, 0)] * 2
    return n_items, np.asarray(items, np.int32).T.reshape(-1)


def _attn_kernel(n_items, slopes_ref, sched_ref, qt_ref, k_ref, vt_ref, kmean_ref, bias_ref, o_ref,
                 shift_ref, s_ref, mx_ref, p_ref, alpha_ref, m_ref, acc_ref):
    step = pl.program_id(1)
    nb = kmean_ref.shape[1]
    seq = k_ref.shape[1]
    blk = MOBA_BLOCK
    neg_inf = -jnp.inf
    heads = range(HEADS_PER_STEP)
    groups = [tuple(range(g * HEADS_PER_GROUP, (g + 1) * HEADS_PER_GROUP)) for g in range(GROUPS_PER_STEP)]
    stride = n_items + 2

    def group_lanes(hh):
        g = hh // HEADS_PER_GROUP
        return slice(g * LANES, (g + 1) * LANES)

    def head_weights(hh, rows):
        pad = jnp.zeros_like(rows)
        return jnp.concatenate([rows, pad] if hh % HEADS_PER_GROUP == 0 else [pad, rows], axis=0)

    def head_rows(hh):
        return slice(hh * HEAD_DIM, (hh + 1) * HEAD_DIM)

    def field(f, w):
        return sched_ref[f * stride + w]

    kmean = kmean_ref[0].astype(BF16)
    key_blk = lax.broadcasted_iota(jnp.int32, (nb, seq), 0)
    qry_blk = lax.broadcasted_iota(jnp.int32, (nb, seq), 1) // blk
    dist = ((qry_blk - key_blk) * blk).astype(F32)
    key_idx = key_blk.astype(F32)
    for hh in heads:
        qt_all = jnp.concatenate([qt_ref[0, i, head_rows(hh), :] for i in range(nb)], axis=1)
        gate = jnp.dot(kmean[:, group_lanes(hh)], head_weights(hh, qt_all),
                       preferred_element_type=F32)
        gate = jnp.where(key_blk < qry_blk, gate, neg_inf)
        sel = key_blk == qry_blk
        for _ in range(MOBA_TOPK):
            top = jnp.max(gate, axis=0, keepdims=True)
            first = jnp.min(jnp.where(gate == top, key_idx, float(nb)), axis=0, keepdims=True)
            pick = (key_idx == first) & (top > neg_inf)
            sel = sel | pick
            gate = jnp.where(pick, neg_inf, gate)
        slope = slopes_ref[step * HEADS_PER_STEP + hh]
        shift = jnp.where(sel, slope * dist, jnp.inf)
        for i in range(nb):
            shift_ref[hh, i] = shift[:, i * blk:(i + 1) * blk]

    def score_dot(w, hh):
        i, ja, jb = field(_F_QBLK, w), field(_F_JA, w), field(_F_JB, w)
        kk = jnp.concatenate(
            [k_ref[0, pl.ds(pl.multiple_of(ja * blk, blk), blk), group_lanes(hh)],
             k_ref[0, pl.ds(pl.multiple_of(jb * blk, blk), blk), group_lanes(hh)]], axis=0)
        return jnp.dot(kk, head_weights(hh, qt_ref[0, i, head_rows(hh), :]),
                       preferred_element_type=F32)

    def stage_unit(st, w, slot, hh, u):
        table = field(_F_FIRST, w) if u == 0 else 0
        s_u = st[u * blk:(u + 1) * blk] + bias_ref[hh, table]
        s_ref[slot, hh, u] = s_u
        mx_ref[slot, hh, u] = jnp.max(s_u, axis=0, keepdims=True)

    def stage_scores(sts, w, slot, hhs):
        for hh in hhs:
            for u in range(2):
                stage_unit(sts[hh], w, slot, hh, u)

    def softmax_prep(w, slot, hh):
        i, ja, jb = field(_F_QBLK, w), field(_F_JA, w), field(_F_JB, w)
        a_valid, b_valid = field(_F_A_VALID, w) == 1, field(_F_B_VALID, w) == 1
        first = field(_F_FIRST, w) == 1
        sh_a = jnp.where(a_valid, shift_ref[hh, i, pl.ds(ja, 1), :], jnp.inf)
        sh_b = jnp.where(b_valid, shift_ref[hh, i, pl.ds(jb, 1), :], jnp.inf)
        m_old = jnp.where(first, neg_inf, m_ref[hh])
        m_new = jnp.maximum(m_old, jnp.maximum(mx_ref[slot, hh, 0] - sh_a,
                                               mx_ref[slot, hh, 1] - sh_b))
        m_ref[hh] = m_new
        alpha_ref[slot, hh] = jnp.exp2(m_old - m_new)
        return m_new + sh_a, m_new + sh_b

    def softmax_block(slot, hh, u, offset):
        p_ref[slot, hh, u * blk:(u + 1) * blk] = jnp.exp2(s_ref[slot, hh, u] - offset).astype(BF16)

    def softmax(w, slot):
        for hh in heads:
            offsets = softmax_prep(w, slot, hh)
            for u in range(2):
                softmax_block(slot, hh, u, offsets[u])

    ones_rows = jnp.ones((SUM_ROWS, 2 * blk), BF16)

    def pv_dot(w, slot, hh):
        ja, jb = field(_F_JA, w), field(_F_JB, w)
        rows = slice(hh * HEAD_DIM, (hh + 1) * HEAD_DIM)
        vts = jnp.concatenate([vt_ref[0, ja, rows, :], vt_ref[0, jb, rows, :]], axis=1)
        lhs = jnp.concatenate([vts, ones_rows], axis=0)
        return jnp.dot(lhs, p_ref[slot, hh], preferred_element_type=F32)

    def accumulate(pvs, w, slot):
        par = field(_F_QBLK, w) % ACC_BUFFERS
        for hh in heads:
            acc_ref[par, hh] = alpha_ref[slot, hh] * acc_ref[par, hh] + pvs[hh]

    def finalize(w):
        i = field(_F_QBLK, w)
        par = i % ACC_BUFFERS
        for hh in heads:
            o_ref[0, i, head_rows(hh), :] = (
                acc_ref[par, hh, :HEAD_DIM] / acc_ref[par, hh, HEAD_DIM:HEAD_DIM + 1]).astype(o_ref.dtype)

    acc_ref[...] = jnp.zeros_like(acc_ref)
    m_ref[...] = jnp.zeros_like(m_ref)

    stage_scores([score_dot(0, hh) for hh in heads], 0, 0, heads)
    stage_scores([score_dot(1, hh) for hh in heads], 1, 1, heads)
    softmax(0, 0)

    def item(w, j):
        slot, sm, sc = j % STAGE_SLOTS, (j + 1) % STAGE_SLOTS, (j + 2) % STAGE_SLOTS
        sts, pvs = {}, {}
        for g, hhs in enumerate(groups):
            for hh in hhs:
                sts[hh] = score_dot(w + 2, hh)
            offs = {hh: softmax_prep(w + 1, sm, hh) for hh in hhs}
            if g > 0:
                stage_scores(sts, w + 2, sc, groups[g - 1])
            for hh in hhs:
                for u in range(2):
                    softmax_block(sm, hh, u, offs[hh][u])
        for hh in heads:
            pvs[hh] = pv_dot(w, slot, hh)
        stage_scores(sts, w + 2, sc, groups[-1])
        accumulate(pvs, w, slot)

    def body(trip, carry):
        for j in range(ITEMS_PER_TRIP):
            item(ITEMS_PER_TRIP * trip + j, j)
        for j in range(ITEMS_PER_TRIP):
            w = ITEMS_PER_TRIP * trip + j
            pl.when(field(_F_LAST, w) == 1)(functools.partial(finalize, w))
        return carry

    assert n_items % ITEMS_PER_TRIP == 0 and ITEMS_PER_TRIP % STAGE_SLOTS == 0
    lax.fori_loop(0, n_items // ITEMS_PER_TRIP, body, 0)


def _attn_call(slopes, bias, qt, k, vt, kmean):
    B, S, W = k.shape
    nb = S // MOBA_BLOCK
    step_lanes = GROUPS_PER_STEP * LANES
    n_steps = W // step_lanes
    hps = HEADS_PER_STEP
    n_items, sched = _attn_schedule(nb)
    grid_spec = pltpu.PrefetchScalarGridSpec(
        num_scalar_prefetch=2,
        grid=(B, n_steps),
        in_specs=[
            pl.BlockSpec((1, nb, step_lanes, MOBA_BLOCK), lambda b, p, *_: (b, 0, p, 0)),
            pl.BlockSpec((1, S, step_lanes), lambda b, p, *_: (b, 0, p)),
            pl.BlockSpec((1, nb, step_lanes, MOBA_BLOCK), lambda b, p, *_: (b, 0, p, 0)),
            pl.BlockSpec((1, nb, step_lanes), lambda b, p, *_: (b, 0, p)),
            pl.BlockSpec((hps, 2, MOBA_BLOCK, MOBA_BLOCK), lambda b, p, *_: (p, 0, 0, 0)),
        ],
        out_specs=pl.BlockSpec((1, nb, step_lanes, MOBA_BLOCK), lambda b, p, *_: (b, 0, p, 0)),
        scratch_shapes=[
            pltpu.VMEM((hps, nb, nb, MOBA_BLOCK), F32),
            pltpu.VMEM((STAGE_SLOTS, hps, 2, MOBA_BLOCK, MOBA_BLOCK), F32),
            pltpu.VMEM((STAGE_SLOTS, hps, 2, 1, MOBA_BLOCK), F32),
            pltpu.VMEM((STAGE_SLOTS, hps, 2 * MOBA_BLOCK, MOBA_BLOCK), BF16),
            pltpu.VMEM((STAGE_SLOTS, hps, 1, MOBA_BLOCK), F32),
            pltpu.VMEM((hps, 1, MOBA_BLOCK), F32),
            pltpu.VMEM((ACC_BUFFERS, hps, HEAD_DIM + SUM_ROWS, MOBA_BLOCK), F32),
        ],
    )
    return pl.pallas_call(
        functools.partial(_attn_kernel, n_items),
        grid_spec=grid_spec,
        out_shape=jax.ShapeDtypeStruct((B, nb, W, MOBA_BLOCK), BF16),
        compiler_params=pltpu.CompilerParams(
            dimension_semantics=("arbitrary", "arbitrary"), vmem_limit_bytes=VMEM_LIMIT),
        name="moba_attn",
    )(slopes, jnp.asarray(sched), qt, k, vt, kmean, bias)


def _ffn_kernel(x_ref, at_ref, p_ref, wo_hbm, g2_ref, wup_hbm, wdn_hbm, g3_ref, o_ref,
                wo_ref, wup_ref, wdn_ref, wide_stage, narrow_stage, stage_sem):
    sub = ROW_SUBTILE_FFN
    n_sub = x_ref.shape[0] // sub

    @pl.when(pl.program_id(0) == 0)
    def _():
        jobs = []
        for src, dst in ((wo_hbm, wo_ref), (wup_hbm, wup_ref), (wdn_hbm, wdn_ref)):
            stage = wide_stage if src.shape[1] == wide_stage.shape[2] else narrow_stage
            assert src.shape[1] == stage.shape[2] and src.shape[0] % stage.shape[1] == 0
            jobs += [(src, dst, stage, r0) for r0 in range(0, src.shape[0], stage.shape[1])]

        def copy(n):
            src, _, stage, r0 = jobs[n]
            return pltpu.make_async_copy(src.at[pl.ds(r0, stage.shape[1]), :], stage.at[n % 2],
                                         stage_sem.at[n % 2])

        copy(0).start()
        for n, (_, dst, stage, r0) in enumerate(jobs):
            if n + 1 < len(jobs):
                copy(n + 1).start()
            copy(n).wait()
            dst[r0:r0 + stage.shape[1], :] = stage[n % 2].astype(BF16)

    def head(i):
        rows = slice(i * sub, (i + 1) * sub)
        nblk = sub // MOBA_BLOCK
        attn = jnp.concatenate(
            [lax.dot_general(at_ref[i * nblk + b], wo_ref[:ATTN_WIDTH, :], TN_DIMS,
                             preferred_element_type=F32) for b in range(nblk)], axis=0)
        x1 = (x_ref[rows, :] + attn
              + jnp.dot(p_ref[rows, :], wo_ref[ATTN_WIDTH:, :], preferred_element_type=F32))
        o_ref[rows, :] = x1
        return _rms_norm(x1, g2_ref[...]).astype(BF16)

    def ffn(i, h):
        rows = slice(i * sub, (i + 1) * sub)
        for c in range(wup_ref.shape[1] // FF_CHUNK):
            cols = slice(c * FF_CHUNK, (c + 1) * FF_CHUNK)
            up = jnp.dot(h, wup_ref[:, cols], preferred_element_type=F32)
            act = jnp.square(jnp.maximum(up, 0.0)).astype(BF16)
            o_ref[rows, :] += jnp.dot(act, wdn_ref[cols, :], preferred_element_type=F32)

    def tail(i):
        rows = slice(i * sub, (i + 1) * sub)
        o_ref[rows, :] = _rms_norm(o_ref[rows, :], g3_ref[...])

    h_prev = head(0)
    for i in range(1, n_sub):
        h_next = head(i)
        ffn(i - 1, h_prev)
        tail(i - 1)
        h_prev = h_next
    ffn(n_sub - 1, h_prev)
    tail(n_sub - 1)


def _ffn_call(x, at, p, wo, g2, wup, wdn, g3):
    N, D = x.shape
    T = ROW_TILE_FFN
    const = lambda shape: pl.BlockSpec(shape, lambda t: (0,) * len(shape),
                                       pipeline_mode=pl.Buffered(1))
    in_hbm = pl.BlockSpec(memory_space=pl.ANY)
    d_ff = wup.shape[1]
    assert wup.shape == (D, d_ff) and wdn.shape == (d_ff, D) and wo.shape == (D, D)
    return pl.pallas_call(
        _ffn_kernel,
        grid=(N // T,),
        in_specs=[
            pl.BlockSpec((T, D), lambda t: (t, 0)),
            pl.BlockSpec((T // MOBA_BLOCK, ATTN_WIDTH, MOBA_BLOCK), lambda t: (t, 0, 0)),
            pl.BlockSpec((T, POOL_WIDTH), lambda t: (t, 0)),
            in_hbm, const((1, D)),
            in_hbm, in_hbm, const((1, D)),
        ],
        out_specs=pl.BlockSpec((T, D), lambda t: (t, 0)),
        out_shape=jax.ShapeDtypeStruct((N, D), F32),
        scratch_shapes=[
            pltpu.VMEM(wo.shape, BF16), pltpu.VMEM(wup.shape, BF16), pltpu.VMEM(wdn.shape, BF16),
            pltpu.VMEM((2, WEIGHT_STAGE_ELEMS // d_ff, d_ff), F32),
            pltpu.VMEM((2, WEIGHT_STAGE_ELEMS // D, D), F32),
            pltpu.SemaphoreType.DMA((2,)),
        ],
        compiler_params=pltpu.CompilerParams(
            dimension_semantics=("arbitrary",), vmem_limit_bytes=VMEM_LIMIT),
        name="outproj_ffn",
    )(x, at, p, wo, g2, wup, wdn, g3)


def _alibi_constants():
    slopes = (2.0 ** (-8.0 * np.arange(1, N_HEADS + 1) / N_HEADS) * LOG2E).astype(np.float32)
    key = np.arange(MOBA_BLOCK, dtype=np.float32)[:, None]
    qry = np.arange(MOBA_BLOCK, dtype=np.float32)[None, :]
    past = -slopes[:, None, None] * (qry - key)[None]
    own = np.where((key <= qry)[None], past, -np.inf).astype(np.float32)
    return jnp.asarray(slopes), jnp.asarray(np.stack([past, own], axis=1))


def kernel(x, norm_mix, w_in, w_pool, pool_scale, w_out, norm_mlp, w_up, w_down, norm_final):
    B, S, D = x.shape
    assert w_in.shape[0] == 1, "single trunk layer (the final norm is fused into the FFN call)"
    assert S % ROW_TILE_IN == 0 and (B * S) % ROW_TILE_FFN == 0 and S % MOBA_BLOCK == 0
    assert w_in.shape[2] == 3 * ATTN_WIDTH + POOL_WIDTH and D == ATTN_WIDTH + POOL_WIDTH
    qt, k, vt, kmean, p = _inproj_call(
        x, norm_mix[0][None, :], w_in[0], w_pool[0], pool_scale[0][None, :])
    slopes, bias = _alibi_constants()
    at = _attn_call(slopes, bias, qt, k, vt, kmean)
    y = _ffn_call(x.reshape(B * S, D), at.reshape(B * (S // MOBA_BLOCK), ATTN_WIDTH, MOBA_BLOCK),
                  p.reshape(B * S, POOL_WIDTH), w_out[0],
                  norm_mlp[0][None, :], w_up[0], w_down[0], norm_final[None, :])
    return y.reshape(B, S, D)
```

```python
import functools

import jax
import jax.numpy as jnp
import numpy as np
from jax import lax
from jax.experimental import pallas as pl
from jax.experimental.pallas import tpu as pltpu

F32 = jnp.float32
BF16 = jnp.bfloat16

N_HEADS = 8
HEAD_DIM = 64
ATTN_WIDTH = N_HEADS * HEAD_DIM
POOL_WINDOWS = (2, 4, 8, 16)
POOL_GROUP = 128
POOL_WIDTH = POOL_GROUP * len(POOL_WINDOWS)
MOBA_BLOCK = 256
MOBA_TOPK = 3
EPS = 1e-6
LOG2E = 1.4426950408889634
QK_SCALE = HEAD_DIM ** -0.5 * LOG2E
SUM_ROWS = 16
POOL_HALO = 16
LANES = 128
HEADS_PER_GROUP = LANES // HEAD_DIM
GROUPS_PER_STEP = 2
HEADS_PER_STEP = GROUPS_PER_STEP * HEADS_PER_GROUP
ITEMS_PER_TRIP = 2
STAGE_SLOTS = 2
ACC_BUFFERS = 2

ROW_TILE_IN = 2048
ROW_SUBTILE_IN = 256
ROW_TILE_FFN = 1024
ROW_SUBTILE_FFN = 512
FF_CHUNK = 1024
WEIGHT_STAGE_ELEMS = 512 * 1024
VMEM_LIMIT = 56 * 1024 * 1024

NT_DIMS = (((1,), (1,)), ((), ()))
TN_DIMS = (((0,), (0,)), ((), ()))


def _rms_norm(x, g):
    ms = jnp.mean(x * x, axis=-1, keepdims=True)
    return x * lax.rsqrt(ms + EPS) * g


def _inproj_kernel(x_ref, g_ref, wi_hbm, wpool_ref, pscale_ref,
                   qt_ref, k_ref, vt_ref, kmean_ref, p_ref,
                   halo_ref, wi_ref, wqvt_ref, stage_ref, stage_sem):
    t = pl.program_id(1)
    rows = x_ref.shape[1]
    sub = ROW_SUBTILE_IN
    nblk = sub // MOBA_BLOCK
    u_cols = slice(3 * ATTN_WIDTH, 3 * ATTN_WIDTH + POOL_WIDTH)

    @pl.when((pl.program_id(0) == 0) & (t == 0))
    def _():
        chunk = stage_ref.shape[1]
        n_chunks = wi_hbm.shape[0] // chunk

        def copy(n):
            return pltpu.make_async_copy(wi_hbm.at[pl.ds(n * chunk, chunk), :], stage_ref.at[n % 2],
                                         stage_sem.at[n % 2])

        copy(0).start()
        for n in range(n_chunks):
            if n + 1 < n_chunks:
                copy(n + 1).start()
            copy(n).wait()
            wi_ref[n * chunk:(n + 1) * chunk, :] = stage_ref[n % 2].astype(BF16)
        wqvt_ref[:ATTN_WIDTH, :] = wi_ref[:, :ATTN_WIDTH].T
        wqvt_ref[ATTN_WIDTH:, :] = wi_ref[:, 2 * ATTN_WIDTH:3 * ATTN_WIDTH].T

    @pl.when(t == 0)
    def _():
        halo_ref[...] = jnp.zeros_like(halo_ref)

    def project(i):
        r0 = i * sub
        h = _rms_norm(x_ref[0, r0:r0 + sub, :], g_ref[...]).astype(BF16)
        kf = jnp.dot(h, wi_ref[:, ATTN_WIDTH:2 * ATTN_WIDTH], preferred_element_type=F32)
        k_ref[0, r0:r0 + sub, :] = kf.astype(BF16)
        for b in range(nblk):
            blk_row = (t * (rows // sub) + i) * nblk + b
            kmean_ref[0, pl.ds(blk_row, 1), :] = (
                jnp.sum(kf[b * MOBA_BLOCK:(b + 1) * MOBA_BLOCK], axis=0, keepdims=True)
                * (1.0 / MOBA_BLOCK))
        qvt = lax.dot_general(wqvt_ref[...], h, NT_DIMS, preferred_element_type=F32)
        for b in range(nblk):
            cols = slice(b * MOBA_BLOCK, (b + 1) * MOBA_BLOCK)
            qt_ref[0, i * nblk + b] = (qvt[:ATTN_WIDTH, cols] * QK_SCALE).astype(BF16)
            vt_ref[0, i * nblk + b] = qvt[ATTN_WIDTH:, cols].astype(BF16)
        return jnp.dot(h, wi_ref[:, u_cols], preferred_element_type=F32)

    def pool(i, u, halo):
        r0 = i * sub
        ext = jnp.concatenate([halo, u], axis=0)
        pos = t * rows + r0 + lax.broadcasted_iota(jnp.int32, (sub, POOL_GROUP), 0)
        for g, w in enumerate(POOL_WINDOWS):
            cols = slice(g * POOL_GROUP, (g + 1) * POOL_GROUP)
            s = ext[:, cols]
            shift = 1
            while shift < w:
                s = s + pltpu.roll(s, shift, axis=0)
                shift *= 2
            cnt = jnp.minimum(pos + 1, w).astype(F32)
            mixed = s[POOL_HALO:, :] / cnt - u[:, cols]
            y = jnp.dot(mixed.astype(BF16), wpool_ref[g].astype(BF16), preferred_element_type=F32)
            p_ref[0, r0:r0 + sub, cols] = (y * pscale_ref[:, cols]).astype(BF16)
        return u[sub - POOL_HALO:, :]

    n_sub = rows // sub
    halo = halo_ref[...]
    u_prev = project(0)
    for i in range(1, n_sub):
        u_next = project(i)
        halo = pool(i - 1, u_prev, halo)
        u_prev = u_next
    halo_ref[...] = pool(n_sub - 1, u_prev, halo)


def _inproj_call(x, g, wi, wpool, pscale):
    B, S, D = x.shape
    T = ROW_TILE_IN
    nblk = T // MOBA_BLOCK
    nb = S // MOBA_BLOCK
    const = lambda shape: pl.BlockSpec(shape, lambda b, t: (0,) * len(shape),
                                       pipeline_mode=pl.Buffered(1))
    return pl.pallas_call(
        _inproj_kernel,
        grid=(B, S // T),
        in_specs=[
            pl.BlockSpec((1, T, D), lambda b, t: (b, t, 0)),
            const((1, D)),
            pl.BlockSpec(memory_space=pl.ANY),
            const(wpool.shape),
            const((1, POOL_WIDTH)),
        ],
        out_specs=[
            pl.BlockSpec((1, nblk, ATTN_WIDTH, MOBA_BLOCK), lambda b, t: (b, t, 0, 0)),
            pl.BlockSpec((1, T, ATTN_WIDTH), lambda b, t: (b, t, 0)),
            pl.BlockSpec((1, nblk, ATTN_WIDTH, MOBA_BLOCK), lambda b, t: (b, t, 0, 0)),
            pl.BlockSpec((1, nb, ATTN_WIDTH), lambda b, t: (b, 0, 0)),
            pl.BlockSpec((1, T, POOL_WIDTH), lambda b, t: (b, t, 0)),
        ],
        out_shape=[
            jax.ShapeDtypeStruct((B, nb, ATTN_WIDTH, MOBA_BLOCK), BF16),
            jax.ShapeDtypeStruct((B, S, ATTN_WIDTH), BF16),
            jax.ShapeDtypeStruct((B, nb, ATTN_WIDTH, MOBA_BLOCK), BF16),
            jax.ShapeDtypeStruct((B, nb, ATTN_WIDTH), F32),
            jax.ShapeDtypeStruct((B, S, POOL_WIDTH), BF16),
        ],
        scratch_shapes=[
            pltpu.VMEM((POOL_HALO, POOL_WIDTH), F32),
            pltpu.VMEM(wi.shape, BF16),
            pltpu.VMEM((2 * ATTN_WIDTH, D), BF16),
            pltpu.VMEM((2, WEIGHT_STAGE_ELEMS // wi.shape[1], wi.shape[1]), F32),
            pltpu.SemaphoreType.DMA((2,)),
        ],
        compiler_params=pltpu.CompilerParams(
            dimension_semantics=("arbitrary", "arbitrary"), vmem_limit_bytes=VMEM_LIMIT),
        name="inproj_pool",
    )(x, g, wi, wpool, pscale)


_F_QBLK, _F_JA, _F_JB, _F_A_VALID, _F_B_VALID, _F_FIRST, _F_LAST = range(7)


def _attn_schedule(nb):
    items = []
    for i in range(nb):
        steps = (i + 2) // 2
        for t in range(steps):
            jb = i - 2 * t - 1
            items.append((i, i - 2 * t, max(jb, 0), 1, int(jb >= 0), int(t == 0), int(t == steps - 1)))
    n_items = len(items)
    items += [(nb - 1, 0, 0, 0, 0, 0, 0)] * 2
    return n_items, np.asarray(items, np.int32).T.reshape(-1)


def _attn_kernel(n_items, slopes_ref, sched_ref, qt_ref, k_ref, vt_ref, kmean_ref, bias_ref, o_ref,
                 shift_ref, s_ref, mx_ref, p_ref, alpha_ref, m_ref, acc_ref):
    step = pl.program_id(1)
    nb = kmean_ref.shape[1]
    seq = k_ref.shape[1]
    blk = MOBA_BLOCK
    neg_inf = -jnp.inf
    heads = range(HEADS_PER_STEP)
    groups = [tuple(range(g * HEADS_PER_GROUP, (g + 1) * HEADS_PER_GROUP)) for g in range(GROUPS_PER_STEP)]
    stride = n_items + 2

    def group_lanes(hh):
        g = hh // HEADS_PER_GROUP
        return slice(g * LANES, (g + 1) * LANES)

    def head_weights(hh, rows):
        pad = jnp.zeros_like(rows)
        return jnp.concatenate([rows, pad] if hh % HEADS_PER_GROUP == 0 else [pad, rows], axis=0)

    def head_rows(hh):
        return slice(hh * HEAD_DIM, (hh + 1) * HEAD_DIM)

    def field(f, w):
        return sched_ref[f * stride + w]

    kmean = kmean_ref[0].astype(BF16)
    key_blk = lax.broadcasted_iota(jnp.int32, (nb, seq), 0)
    qry_blk = lax.broadcasted_iota(jnp.int32, (nb, seq), 1) // blk
    dist = ((qry_blk - key_blk) * blk).astype(F32)
    key_idx = key_blk.astype(F32)
    for hh in heads:
        qt_all = jnp.concatenate([qt_ref[0, i, head_rows(hh), :] for i in range(nb)], axis=1)
        gate = jnp.dot(kmean[:, group_lanes(hh)], head_weights(hh, qt_all),
                       preferred_element_type=F32)
        gate = jnp.where(key_blk < qry_blk, gate, neg_inf)
        sel = key_blk == qry_blk
        for _ in range(MOBA_TOPK):
            top = jnp.max(gate, axis=0, keepdims=True)
            first = jnp.min(jnp.where(gate == top, key_idx, float(nb)), axis=0, keepdims=True)
            pick = (key_idx == first) & (top > neg_inf)
            sel = sel | pick
            gate = jnp.where(pick, neg_inf, gate)
        slope = slopes_ref[step * HEADS_PER_STEP + hh]
        shift = jnp.where(sel, slope * dist, jnp.inf)
        for i in range(nb):
            shift_ref[hh, i] = shift[:, i * blk:(i + 1) * blk]

    def score_dot(w, hh):
        i, ja, jb = field(_F_QBLK, w), field(_F_JA, w), field(_F_JB, w)
        kk = jnp.concatenate(
            [k_ref[0, pl.ds(pl.multiple_of(ja * blk, blk), blk), group_lanes(hh)],
             k_ref[0, pl.ds(pl.multiple_of(jb * blk, blk), blk), group_lanes(hh)]], axis=0)
        return jnp.dot(kk, head_weights(hh, qt_ref[0, i, head_rows(hh), :]),
                       preferred_element_type=F32)

    def stage_unit(st, w, slot, hh, u):
        table = field(_F_FIRST, w) if u == 0 else 0
        s_u = st[u * blk:(u + 1) * blk] + bias_ref[hh, table]
        s_ref[slot, hh, u] = s_u
        mx_ref[slot, hh, u] = jnp.max(s_u, axis=0, keepdims=True)

    def stage_scores(sts, w, slot, hhs):
        for hh in hhs:
            for u in range(2):
                stage_unit(sts[hh], w, slot, hh, u)

    def softmax_prep(w, slot, hh):
        i, ja, jb = field(_F_QBLK, w), field(_F_JA, w), field(_F_JB, w)
        a_valid, b_valid = field(_F_A_VALID, w) == 1, field(_F_B_VALID, w) == 1
        first = field(_F_FIRST, w) == 1
        sh_a = jnp.where(a_valid, shift_ref[hh, i, pl.ds(ja, 1), :], jnp.inf)
        sh_b = jnp.where(b_valid, shift_ref[hh, i, pl.ds(jb, 1), :], jnp.inf)
        m_old = jnp.where(first, neg_inf, m_ref[hh])
        m_new = jnp.maximum(m_old, jnp.maximum(mx_ref[slot, hh, 0] - sh_a,
                                               mx_ref[slot, hh, 1] - sh_b))
        m_ref[hh] = m_new
        alpha_ref[slot, hh] = jnp.exp2(m_old - m_new)
        return m_new + sh_a, m_new + sh_b

    def softmax_block(slot, hh, u, offset):
        p_ref[slot, hh, u * blk:(u + 1) * blk] = jnp.exp2(s_ref[slot, hh, u] - offset).astype(BF16)

    def softmax(w, slot):
        for hh in heads:
            offsets = softmax_prep(w, slot, hh)
            for u in range(2):
                softmax_block(slot, hh, u, offsets[u])

    ones_rows = jnp.ones((SUM_ROWS, 2 * blk), BF16)

    def pv_dot(w, slot, hh):
        ja, jb = field(_F_JA, w), field(_F_JB, w)
        rows = slice(hh * HEAD_DIM, (hh + 1) * HEAD_DIM)
        vts = jnp.concatenate([vt_ref[0, ja, rows, :], vt_ref[0, jb, rows, :]], axis=1)
        lhs = jnp.concatenate([vts, ones_rows], axis=0)
        return jnp.dot(lhs, p_ref[slot, hh], preferred_element_type=F32)

    def accumulate(pvs, w, slot):
        par = field(_F_QBLK, w) % ACC_BUFFERS
        for hh in heads:
            acc_ref[par, hh] = alpha_ref[slot, hh] * acc_ref[par, hh] + pvs[hh]

    def finalize(w):
        i = field(_F_QBLK, w)
        par = i % ACC_BUFFERS
        for hh in heads:
            o_ref[0, i, head_rows(hh), :] = (
                acc_ref[par, hh, :HEAD_DIM] / acc_ref[par, hh, HEAD_DIM:HEAD_DIM + 1]).astype(o_ref.dtype)

    acc_ref[...] = jnp.zeros_like(acc_ref)
    m_ref[...] = jnp.zeros_like(m_ref)

    stage_scores([score_dot(0, hh) for hh in heads], 0, 0, heads)
    stage_scores([score_dot(1, hh) for hh in heads], 1, 1, heads)
    softmax(0, 0)

    def item(w, j):
        slot, sm, sc = j % STAGE_SLOTS, (j + 1) % STAGE_SLOTS, (j + 2) % STAGE_SLOTS
        sts, pvs = {}, {}
        for g, hhs in enumerate(groups):
            for hh in hhs:
                sts[hh] = score_dot(w + 2, hh)
            offs = {hh: softmax_prep(w + 1, sm, hh) for hh in hhs}
            if g > 0:
                stage_scores(sts, w + 2, sc, groups[g - 1])
            for hh in hhs:
                for u in range(2):
                    softmax_block(sm, hh, u, offs[hh][u])
        for hh in heads:
            pvs[hh] = pv_dot(w, slot, hh)
        stage_scores(sts, w + 2, sc, groups[-1])
        accumulate(pvs, w, slot)

    def body(trip, carry):
        for j in range(ITEMS_PER_TRIP):
            item(ITEMS_PER_TRIP * trip + j, j)
        for j in range(ITEMS_PER_TRIP):
            w = ITEMS_PER_TRIP * trip + j
            pl.when(field(_F_LAST, w) == 1)(functools.partial(finalize, w))
        return carry

    assert n_items % ITEMS_PER_TRIP == 0 and ITEMS_PER_TRIP % STAGE_SLOTS == 0
    lax.fori_loop(0, n_items // ITEMS_PER_TRIP, body, 0)


def _attn_call(slopes, bias, qt, k, vt, kmean):
    B, S, W = k.shape
    nb = S // MOBA_BLOCK
    step_lanes = GROUPS_PER_STEP * LANES
    n_steps = W // step_lanes
    hps = HEADS_PER_STEP
    n_items, sched = _attn_schedule(nb)
    grid_spec = pltpu.PrefetchScalarGridSpec(
        num_scalar_prefetch=2,
        grid=(B, n_steps),
        in_specs=[
            pl.BlockSpec((1, nb, step_lanes, MOBA_BLOCK), lambda b, p, *_: (b, 0, p, 0)),
            pl.BlockSpec((1, S, step_lanes), lambda b, p, *_: (b, 0, p)),
            pl.BlockSpec((1, nb, step_lanes, MOBA_BLOCK), lambda b, p, *_: (b, 0, p, 0)),
            pl.BlockSpec((1, nb, step_lanes), lambda b, p, *_: (b, 0, p)),
            pl.BlockSpec((hps, 2, MOBA_BLOCK, MOBA_BLOCK), lambda b, p, *_: (p, 0, 0, 0)),
        ],
        out_specs=pl.BlockSpec((1, nb, step_lanes, MOBA_BLOCK), lambda b, p, *_: (b, 0, p, 0)),
        scratch_shapes=[
            pltpu.VMEM((hps, nb, nb, MOBA_BLOCK), F32),
            pltpu.VMEM((STAGE_SLOTS, hps, 2, MOBA_BLOCK, MOBA_BLOCK), F32),
            pltpu.VMEM((STAGE_SLOTS, hps, 2, 1, MOBA_BLOCK), F32),
            pltpu.VMEM((STAGE_SLOTS, hps, 2 * MOBA_BLOCK, MOBA_BLOCK), BF16),
            pltpu.VMEM((STAGE_SLOTS, hps, 1, MOBA_BLOCK), F32),
            pltpu.VMEM((hps, 1, MOBA_BLOCK), F32),
            pltpu.VMEM((ACC_BUFFERS, hps, HEAD_DIM + SUM_ROWS, MOBA_BLOCK), F32),
        ],
    )
    return pl.pallas_call(
        functools.partial(_attn_kernel, n_items),
        grid_spec=grid_spec,
        out_shape=jax.ShapeDtypeStruct((B, nb, W, MOBA_BLOCK), BF16),
        compiler_params=pltpu.CompilerParams(
            dimension_semantics=("arbitrary", "arbitrary"), vmem_limit_bytes=VMEM_LIMIT),
        name="moba_attn",
    )(slopes, jnp.asarray(sched), qt, k, vt, kmean, bias)


def _ffn_kernel(x_ref, at_ref, p_ref, wo_hbm, g2_ref, wup_hbm, wdn_hbm, g3_ref, o_ref,
                wo_ref, wup_ref, wdn_ref, wide_stage, narrow_stage, stage_sem):
    sub = ROW_SUBTILE_FFN
    n_sub = x_ref.shape[0] // sub

    @pl.when(pl.program_id(0) == 0)
    def _():
        jobs = []
        for src, dst in ((wo_hbm, wo_ref), (wup_hbm, wup_ref), (wdn_hbm, wdn_ref)):
            stage = wide_stage if src.shape[1] == wide_stage.shape[2] else narrow_stage
            assert src.shape[1] == stage.shape[2] and src.shape[0] % stage.shape[1] == 0
            jobs += [(src, dst, stage, r0) for r0 in range(0, src.shape[0], stage.shape[1])]

        def copy(n):
            src, _, stage, r0 = jobs[n]
            return pltpu.make_async_copy(src.at[pl.ds(r0, stage.shape[1]), :], stage.at[n % 2],
                                         stage_sem.at[n % 2])

        copy(0).start()
        for n, (_, dst, stage, r0) in enumerate(jobs):
            if n + 1 < len(jobs):
                copy(n + 1).start()
            copy(n).wait()
            dst[r0:r0 + stage.shape[1], :] = stage[n % 2].astype(BF16)

    def head(i):
        rows = slice(i * sub, (i + 1) * sub)
        nblk = sub // MOBA_BLOCK
        attn = jnp.concatenate(
            [lax.dot_general(at_ref[i * nblk + b], wo_ref[:ATTN_WIDTH, :], TN_DIMS,
                             preferred_element_type=F32) for b in range(nblk)], axis=0)
        x1 = (x_ref[rows, :] + attn
              + jnp.dot(p_ref[rows, :], wo_ref[ATTN_WIDTH:, :], preferred_element_type=F32))
        o_ref[rows, :] = x1
        return _rms_norm(x1, g2_ref[...]).astype(BF16)

    def ffn(i, h):
        rows = slice(i * sub, (i + 1) * sub)
        for c in range(wup_ref.shape[1] // FF_CHUNK):
            cols = slice(c * FF_CHUNK, (c + 1) * FF_CHUNK)
            up = jnp.dot(h, wup_ref[:, cols], preferred_element_type=F32)
            act = jnp.square(jnp.maximum(up, 0.0)).astype(BF16)
            o_ref[rows, :] += jnp.dot(act, wdn_ref[cols, :], preferred_element_type=F32)

    def tail(i):
        rows = slice(i * sub, (i + 1) * sub)
        o_ref[rows, :] = _rms_norm(o_ref[rows, :], g3_ref[...])

    h_prev = head(0)
    for i in range(1, n_sub):
        h_next = head(i)
        ffn(i - 1, h_prev)
        tail(i - 1)
        h_prev = h_next
    ffn(n_sub - 1, h_prev)
    tail(n_sub - 1)


def _ffn_call(x, at, p, wo, g2, wup, wdn, g3):
    N, D = x.shape
    T = ROW_TILE_FFN
    const = lambda shape: pl.BlockSpec(shape, lambda t: (0,) * len(shape),
                                       pipeline_mode=pl.Buffered(1))
    in_hbm = pl.BlockSpec(memory_space=pl.ANY)
    d_ff = wup.shape[1]
    assert wup.shape == (D, d_ff) and wdn.shape == (d_ff, D) and wo.shape == (D, D)
    return pl.pallas_call(
        _ffn_kernel,
        grid=(N // T,),
        in_specs=[
            pl.BlockSpec((T, D), lambda t: (t, 0)),
            pl.BlockSpec((T // MOBA_BLOCK, ATTN_WIDTH, MOBA_BLOCK), lambda t: (t, 0, 0)),
            pl.BlockSpec((T, POOL_WIDTH), lambda t: (t, 0)),
            in_hbm, const((1, D)),
            in_hbm, in_hbm, const((1, D)),
        ],
        out_specs=pl.BlockSpec((T, D), lambda t: (t, 0)),
        out_shape=jax.ShapeDtypeStruct((N, D), F32),
        scratch_shapes=[
            pltpu.VMEM(wo.shape, BF16), pltpu.VMEM(wup.shape, BF16), pltpu.VMEM(wdn.shape, BF16),
            pltpu.VMEM((2, WEIGHT_STAGE_ELEMS // d_ff, d_ff), F32),
            pltpu.VMEM((2, WEIGHT_STAGE_ELEMS // D, D), F32),
            pltpu.SemaphoreType.DMA((2,)),
        ],
        compiler_params=pltpu.CompilerParams(
            dimension_semantics=("arbitrary",), vmem_limit_bytes=VMEM_LIMIT),
        name="outproj_ffn",
    )(x, at, p, wo, g2, wup, wdn, g3)


def _alibi_constants():
    slopes = (2.0 ** (-8.0 * np.arange(1, N_HEADS + 1) / N_HEADS) * LOG2E).astype(np.float32)
    key = np.arange(MOBA_BLOCK, dtype=np.float32)[:, None]
    qry = np.arange(MOBA_BLOCK, dtype=np.float32)[None, :]
    past = -slopes[:, None, None] * (qry - key)[None]
    own = np.where((key <= qry)[None], past, -np.inf).astype(np.float32)
    return jnp.asarray(slopes), jnp.asarray(np.stack([past, own], axis=1))


def kernel(x, norm_mix, w_in, w_pool, pool_scale, w_out, norm_mlp, w_up, w_down, norm_final):
    B, S, D = x.shape
    assert w_in.shape[0] == 1, "single trunk layer (the final norm is fused into the FFN call)"
    assert S % ROW_TILE_IN == 0 and (B * S) % ROW_TILE_FFN == 0 and S % MOBA_BLOCK == 0
    assert ROW_SUBTILE_IN % MOBA_BLOCK == 0 and ROW_TILE_IN % ROW_SUBTILE_IN == 0
    assert ROW_SUBTILE_FFN % MOBA_BLOCK == 0 and ROW_TILE_FFN % ROW_SUBTILE_FFN == 0
    assert w_in.shape[2] == 3 * ATTN_WIDTH + POOL_WIDTH and D == ATTN_WIDTH + POOL_WIDTH
    qt, k, vt, kmean, p = _inproj_call(
        x, norm_mix[0][None, :], w_in[0], w_pool[0], pool_scale[0][None, :])
    slopes, bias = _alibi_constants()
    at = _attn_call(slopes, bias, qt, k, vt, kmean)
    y = _ffn_call(x.reshape(B * S, D), at.reshape(B * (S // MOBA_BLOCK), ATTN_WIDTH, MOBA_BLOCK),
                  p.reshape(B * S, POOL_WIDTH), w_out[0],
                  norm_mlp[0][None, :], w_up[0], w_down[0], norm_final[None, :])
    return y.reshape(B, S, D)
```

```python
import functools

import jax
import jax.numpy as jnp
import numpy as np
from jax import lax
from jax.experimental import pallas as pl
from jax.experimental.pallas import tpu as pltpu

F32 = jnp.float32
BF16 = jnp.bfloat16

N_HEADS = 8
HEAD_DIM = 64
ATTN_WIDTH = N_HEADS * HEAD_DIM
POOL_WINDOWS = (2, 4, 8, 16)
POOL_GROUP = 128
POOL_WIDTH = POOL_GROUP * len(POOL_WINDOWS)
MOBA_BLOCK = 256
MOBA_TOPK = 3
EPS = 1e-6
LOG2E = 1.4426950408889634
QK_SCALE = HEAD_DIM ** -0.5 * LOG2E
SUM_ROWS = 16
POOL_HALO = 16
LANES = 128
HEADS_PER_GROUP = LANES // HEAD_DIM
GROUPS_PER_STEP = 2
HEADS_PER_STEP = GROUPS_PER_STEP * HEADS_PER_GROUP
GATE_CHUNK = 4
ITEMS_PER_TRIP = 2
STAGE_SLOTS = 2
ACC_BUFFERS = 2

ROW_TILE_IN = 2048
ROW_SUBTILE_IN = 256
ROW_TILE_FFN = 1024
ROW_SUBTILE_FFN = 512
FF_CHUNK = 1024
WEIGHT_STAGE_ELEMS = 512 * 1024
VMEM_LIMIT = 56 * 1024 * 1024

NT_DIMS = (((1,), (1,)), ((), ()))
TN_DIMS = (((0,), (0,)), ((), ()))


def _rms_norm(x, g):
    ms = jnp.mean(x * x, axis=-1, keepdims=True)
    return x * lax.rsqrt(ms + EPS) * g


def _inproj_kernel(x_ref, g_ref, wi_hbm, wpool_ref, pscale_ref,
                   qt_ref, k_ref, vt_ref, kmean_ref, p_ref,
                   halo_ref, wi_ref, wqvt_ref, stage_ref, stage_sem):
    t = pl.program_id(1)
    rows = x_ref.shape[1]
    sub = ROW_SUBTILE_IN
    nblk = sub // MOBA_BLOCK
    u_cols = slice(3 * ATTN_WIDTH, 3 * ATTN_WIDTH + POOL_WIDTH)

    @pl.when((pl.program_id(0) == 0) & (t == 0))
    def _():
        chunk = stage_ref.shape[1]
        n_chunks = wi_hbm.shape[0] // chunk

        def copy(n):
            return pltpu.make_async_copy(wi_hbm.at[pl.ds(n * chunk, chunk), :], stage_ref.at[n % 2],
                                         stage_sem.at[n % 2])

        copy(0).start()
        for n in range(n_chunks):
            if n + 1 < n_chunks:
                copy(n + 1).start()
            copy(n).wait()
            wi_ref[n * chunk:(n + 1) * chunk, :] = stage_ref[n % 2].astype(BF16)
        wqvt_ref[:ATTN_WIDTH, :] = wi_ref[:, :ATTN_WIDTH].T
        wqvt_ref[ATTN_WIDTH:, :] = wi_ref[:, 2 * ATTN_WIDTH:3 * ATTN_WIDTH].T

    @pl.when(t == 0)
    def _():
        halo_ref[...] = jnp.zeros_like(halo_ref)

    def project(i):
        r0 = i * sub
        h = _rms_norm(x_ref[0, r0:r0 + sub, :], g_ref[...]).astype(BF16)
        kf = jnp.dot(h, wi_ref[:, ATTN_WIDTH:2 * ATTN_WIDTH], preferred_element_type=F32)
        k_ref[0, r0:r0 + sub, :] = kf.astype(BF16)
        for b in range(nblk):
            blk_row = (t * (rows // sub) + i) * nblk + b
            kmean_ref[0, pl.ds(blk_row, 1), :] = (
                jnp.sum(kf[b * MOBA_BLOCK:(b + 1) * MOBA_BLOCK], axis=0, keepdims=True)
                * (1.0 / MOBA_BLOCK))
        qvt = lax.dot_general(wqvt_ref[...], h, NT_DIMS, preferred_element_type=F32)
        for b in range(nblk):
            cols = slice(b * MOBA_BLOCK, (b + 1) * MOBA_BLOCK)
            qt_ref[0, i * nblk + b] = (qvt[:ATTN_WIDTH, cols] * QK_SCALE).astype(BF16)
            vt_ref[0, i * nblk + b] = qvt[ATTN_WIDTH:, cols].astype(BF16)
        return jnp.dot(h, wi_ref[:, u_cols], preferred_element_type=F32)

    def pool(i, u, halo):
        r0 = i * sub
        ext = jnp.concatenate([halo, u], axis=0)
        pos = t * rows + r0 + lax.broadcasted_iota(jnp.int32, (sub, POOL_GROUP), 0)
        for g, w in enumerate(POOL_WINDOWS):
            cols = slice(g * POOL_GROUP, (g + 1) * POOL_GROUP)
            s = ext[:, cols]
            shift = 1
            while shift < w:
                s = s + pltpu.roll(s, shift, axis=0)
                shift *= 2
            cnt = jnp.minimum(pos + 1, w).astype(F32)
            mixed = s[POOL_HALO:, :] / cnt - u[:, cols]
            y = jnp.dot(mixed.astype(BF16), wpool_ref[g].astype(BF16), preferred_element_type=F32)
            p_ref[0, r0:r0 + sub, cols] = (y * pscale_ref[:, cols]).astype(BF16)
        return u[sub - POOL_HALO:, :]

    n_sub = rows // sub
    halo = halo_ref[...]
    u_prev = project(0)
    for i in range(1, n_sub):
        u_next = project(i)
        halo = pool(i - 1, u_prev, halo)
        u_prev = u_next
    halo_ref[...] = pool(n_sub - 1, u_prev, halo)


def _inproj_call(x, g, wi, wpool, pscale):
    B, S, D = x.shape
    T = ROW_TILE_IN
    nblk = T // MOBA_BLOCK
    nb = S // MOBA_BLOCK
    const = lambda shape: pl.BlockSpec(shape, lambda b, t: (0,) * len(shape),
                                       pipeline_mode=pl.Buffered(1))
    return pl.pallas_call(
        _inproj_kernel,
        grid=(B, S // T),
        in_specs=[
            pl.BlockSpec((1, T, D), lambda b, t: (b, t, 0)),
            const((1, D)),
            pl.BlockSpec(memory_space=pl.ANY),
            const(wpool.shape),
            const((1, POOL_WIDTH)),
        ],
        out_specs=[
            pl.BlockSpec((1, nblk, ATTN_WIDTH, MOBA_BLOCK), lambda b, t: (b, t, 0, 0)),
            pl.BlockSpec((1, T, ATTN_WIDTH), lambda b, t: (b, t, 0)),
            pl.BlockSpec((1, nblk, ATTN_WIDTH, MOBA_BLOCK), lambda b, t: (b, t, 0, 0)),
            pl.BlockSpec((1, nb, ATTN_WIDTH), lambda b, t: (b, 0, 0)),
            pl.BlockSpec((1, T, POOL_WIDTH), lambda b, t: (b, t, 0)),
        ],
        out_shape=[
            jax.ShapeDtypeStruct((B, nb, ATTN_WIDTH, MOBA_BLOCK), BF16),
            jax.ShapeDtypeStruct((B, S, ATTN_WIDTH), BF16),
            jax.ShapeDtypeStruct((B, nb, ATTN_WIDTH, MOBA_BLOCK), BF16),
            jax.ShapeDtypeStruct((B, nb, ATTN_WIDTH), F32),
            jax.ShapeDtypeStruct((B, S, POOL_WIDTH), BF16),
        ],
        scratch_shapes=[
            pltpu.VMEM((POOL_HALO, POOL_WIDTH), F32),
            pltpu.VMEM(wi.shape, BF16),
            pltpu.VMEM((2 * ATTN_WIDTH, D), BF16),
            pltpu.VMEM((2, WEIGHT_STAGE_ELEMS // wi.shape[1], wi.shape[1]), F32),
            pltpu.SemaphoreType.DMA((2,)),
        ],
        compiler_params=pltpu.CompilerParams(
            dimension_semantics=("arbitrary", "arbitrary"), vmem_limit_bytes=VMEM_LIMIT),
        name="inproj_pool",
    )(x, g, wi, wpool, pscale)


_F_QBLK, _F_JA, _F_JB, _F_A_VALID, _F_B_VALID, _F_FIRST, _F_LAST = range(7)


def _attn_schedule(nb):
    items = []
    for i in range(nb):
        steps = (i + 2) // 2
        for t in range(steps):
            jb = i - 2 * t - 1
            items.append((i, i - 2 * t, max(jb, 0), 1, int(jb >= 0), int(t == 0), int(t == steps - 1)))
    n_items = len(items)
    items += [(nb - 1, 0, 0, 0, 0, 0, 0)] * 2
    return n_items, np.asarray(items, np.int32).T.reshape(-1)


def _attn_kernel(n_items, slopes_ref, sched_ref, qt_ref, k_ref, vt_ref, kmean_ref, bias_ref, o_ref,
                 shift_ref, s_ref, mx_ref, p_ref, alpha_ref, m_ref, acc_ref):
    step = pl.program_id(1)
    nb = kmean_ref.shape[1]
    seq = k_ref.shape[1]
    blk = MOBA_BLOCK
    neg_inf = -jnp.inf
    heads = range(HEADS_PER_STEP)
    groups = [tuple(range(g * HEADS_PER_GROUP, (g + 1) * HEADS_PER_GROUP)) for g in range(GROUPS_PER_STEP)]
    stride = n_items + 2

    def group_lanes(hh):
        g = hh // HEADS_PER_GROUP
        return slice(g * LANES, (g + 1) * LANES)

    def head_weights(hh, rows):
        pad = jnp.zeros_like(rows)
        return jnp.concatenate([rows, pad] if hh % HEADS_PER_GROUP == 0 else [pad, rows], axis=0)

    def head_rows(hh):
        return slice(hh * HEAD_DIM, (hh + 1) * HEAD_DIM)

    def field(f, w):
        return sched_ref[f * stride + w]

    kmean = kmean_ref[0].astype(BF16)
    width = GATE_CHUNK * blk
    key_blk = lax.broadcasted_iota(jnp.int32, (nb, width), 0)
    key_idx = key_blk.astype(F32)
    lane_blk = lax.broadcasted_iota(jnp.int32, (nb, width), 1) // blk
    for c in range(nb // GATE_CHUNK):
        qry_blk = lane_blk + c * GATE_CHUNK
        dist = ((qry_blk - key_blk) * blk).astype(F32)
        all_past_selected = (c + 1) * GATE_CHUNK - 1 <= MOBA_TOPK
        for hh in heads:
            if all_past_selected:
                sel = key_blk <= qry_blk
            else:
                qt_c = jnp.concatenate([qt_ref[0, c * GATE_CHUNK + i, head_rows(hh), :]
                                        for i in range(GATE_CHUNK)], axis=1)
                gate = jnp.dot(kmean[:, group_lanes(hh)], head_weights(hh, qt_c),
                               preferred_element_type=F32)
                gate = jnp.where(key_blk < qry_blk, gate, neg_inf)
                sel = key_blk == qry_blk
                for _ in range(MOBA_TOPK):
                    top = jnp.max(gate, axis=0, keepdims=True)
                    first = jnp.min(jnp.where(gate == top, key_idx, float(nb)), axis=0, keepdims=True)
                    pick = (key_idx == first) & (top > neg_inf)
                    sel = sel | pick
                    gate = jnp.where(pick, neg_inf, gate)
            slope = slopes_ref[step * HEADS_PER_STEP + hh]
            shift = jnp.where(sel, slope * dist, jnp.inf)
            for i in range(GATE_CHUNK):
                shift_ref[hh, c * GATE_CHUNK + i] = shift[:, i * blk:(i + 1) * blk]

    def score_dot(w, hh):
        i, ja, jb = field(_F_QBLK, w), field(_F_JA, w), field(_F_JB, w)
        kk = jnp.concatenate(
            [k_ref[0, pl.ds(pl.multiple_of(ja * blk, blk), blk), group_lanes(hh)],
             k_ref[0, pl.ds(pl.multiple_of(jb * blk, blk), blk), group_lanes(hh)]], axis=0)
        return jnp.dot(kk, head_weights(hh, qt_ref[0, i, head_rows(hh), :]),
                       preferred_element_type=F32)

    def stage_unit(st, w, slot, hh, u):
        table = field(_F_FIRST, w) if u == 0 else 0
        s_u = st[u * blk:(u + 1) * blk] + bias_ref[hh, table]
        s_ref[slot, hh, u] = s_u
        mx_ref[slot, hh, u] = jnp.max(s_u, axis=0, keepdims=True)

    def stage_scores(sts, w, slot, hhs):
        for hh in hhs:
            for u in range(2):
                stage_unit(sts[hh], w, slot, hh, u)

    def softmax_prep(w, slot, hh):
        i, ja, jb = field(_F_QBLK, w), field(_F_JA, w), field(_F_JB, w)
        a_valid, b_valid = field(_F_A_VALID, w) == 1, field(_F_B_VALID, w) == 1
        first = field(_F_FIRST, w) == 1
        sh_a = jnp.where(a_valid, shift_ref[hh, i, pl.ds(ja, 1), :], jnp.inf)
        sh_b = jnp.where(b_valid, shift_ref[hh, i, pl.ds(jb, 1), :], jnp.inf)
        m_old = jnp.where(first, neg_inf, m_ref[hh])
        m_new = jnp.maximum(m_old, jnp.maximum(mx_ref[slot, hh, 0] - sh_a,
                                               mx_ref[slot, hh, 1] - sh_b))
        m_ref[hh] = m_new
        alpha_ref[slot, hh] = jnp.exp2(m_old - m_new)
        return m_new + sh_a, m_new + sh_b

    def softmax_block(slot, hh, u, offset):
        p_ref[slot, hh, u * blk:(u + 1) * blk] = jnp.exp2(s_ref[slot, hh, u] - offset).astype(BF16)

    def softmax(w, slot):
        for hh in heads:
            offsets = softmax_prep(w, slot, hh)
            for u in range(2):
                softmax_block(slot, hh, u, offsets[u])

    ones_rows = jnp.ones((SUM_ROWS, 2 * blk), BF16)

    def pv_dot(w, slot, hh):
        ja, jb = field(_F_JA, w), field(_F_JB, w)
        rows = slice(hh * HEAD_DIM, (hh + 1) * HEAD_DIM)
        vts = jnp.concatenate([vt_ref[0, ja, rows, :], vt_ref[0, jb, rows, :]], axis=1)
        lhs = jnp.concatenate([vts, ones_rows], axis=0)
        return jnp.dot(lhs, p_ref[slot, hh], preferred_element_type=F32)

    def accumulate(pvs, w, slot):
        par = field(_F_QBLK, w) % ACC_BUFFERS
        for hh in heads:
            acc_ref[par, hh] = alpha_ref[slot, hh] * acc_ref[par, hh] + pvs[hh]

    def finalize(w):
        i = field(_F_QBLK, w)
        par = i % ACC_BUFFERS
        for hh in heads:
            o_ref[0, i, head_rows(hh), :] = (
                acc_ref[par, hh, :HEAD_DIM] / acc_ref[par, hh, HEAD_DIM:HEAD_DIM + 1]).astype(o_ref.dtype)

    acc_ref[...] = jnp.zeros_like(acc_ref)
    m_ref[...] = jnp.zeros_like(m_ref)

    stage_scores([score_dot(0, hh) for hh in heads], 0, 0, heads)
    stage_scores([score_dot(1, hh) for hh in heads], 1, 1, heads)
    softmax(0, 0)

    def item(w, j):
        slot, sm, sc = j % STAGE_SLOTS, (j + 1) % STAGE_SLOTS, (j + 2) % STAGE_SLOTS
        sts, pvs = {}, {}
        for g, hhs in enumerate(groups):
            for hh in hhs:
                sts[hh] = score_dot(w + 2, hh)
            offs = {hh: softmax_prep(w + 1, sm, hh) for hh in hhs}
            if g > 0:
                stage_scores(sts, w + 2, sc, groups[g - 1])
            for hh in hhs:
                for u in range(2):
                    softmax_block(sm, hh, u, offs[hh][u])
        for hh in heads:
            pvs[hh] = pv_dot(w, slot, hh)
        stage_scores(sts, w + 2, sc, groups[-1])
        accumulate(pvs, w, slot)

    def body(trip, carry):
        for j in range(ITEMS_PER_TRIP):
            item(ITEMS_PER_TRIP * trip + j, j)
        for j in range(ITEMS_PER_TRIP):
            w = ITEMS_PER_TRIP * trip + j
            pl.when(field(_F_LAST, w) == 1)(functools.partial(finalize, w))
        return carry

    assert n_items % ITEMS_PER_TRIP == 0 and ITEMS_PER_TRIP % STAGE_SLOTS == 0
    lax.fori_loop(0, n_items // ITEMS_PER_TRIP, body, 0)


def _attn_call(slopes, bias, qt, k, vt, kmean):
    B, S, W = k.shape
    nb = S // MOBA_BLOCK
    step_lanes = GROUPS_PER_STEP * LANES
    n_steps = W // step_lanes
    hps = HEADS_PER_STEP
    n_items, sched = _attn_schedule(nb)
    grid_spec = pltpu.PrefetchScalarGridSpec(
        num_scalar_prefetch=2,
        grid=(B, n_steps),
        in_specs=[
            pl.BlockSpec((1, nb, step_lanes, MOBA_BLOCK), lambda b, p, *_: (b, 0, p, 0)),
            pl.BlockSpec((1, S, step_lanes), lambda b, p, *_: (b, 0, p)),
            pl.BlockSpec((1, nb, step_lanes, MOBA_BLOCK), lambda b, p, *_: (b, 0, p, 0)),
            pl.BlockSpec((1, nb, step_lanes), lambda b, p, *_: (b, 0, p)),
            pl.BlockSpec((hps, 2, MOBA_BLOCK, MOBA_BLOCK), lambda b, p, *_: (p, 0, 0, 0)),
        ],
        out_specs=pl.BlockSpec((1, nb, step_lanes, MOBA_BLOCK), lambda b, p, *_: (b, 0, p, 0)),
        scratch_shapes=[
            pltpu.VMEM((hps, nb, nb, MOBA_BLOCK), F32),
            pltpu.VMEM((STAGE_SLOTS, hps, 2, MOBA_BLOCK, MOBA_BLOCK), F32),
            pltpu.VMEM((STAGE_SLOTS, hps, 2, 1, MOBA_BLOCK), F32),
            pltpu.VMEM((STAGE_SLOTS, hps, 2 * MOBA_BLOCK, MOBA_BLOCK), BF16),
            pltpu.VMEM((STAGE_SLOTS, hps, 1, MOBA_BLOCK), F32),
            pltpu.VMEM((hps, 1, MOBA_BLOCK), F32),
            pltpu.VMEM((ACC_BUFFERS, hps, HEAD_DIM + SUM_ROWS, MOBA_BLOCK), F32),
        ],
    )
    return pl.pallas_call(
        functools.partial(_attn_kernel, n_items),
        grid_spec=grid_spec,
        out_shape=jax.ShapeDtypeStruct((B, nb, W, MOBA_BLOCK), BF16),
        compiler_params=pltpu.CompilerParams(
            dimension_semantics=("arbitrary", "arbitrary"), vmem_limit_bytes=VMEM_LIMIT),
        name="moba_attn",
    )(slopes, jnp.asarray(sched), qt, k, vt, kmean, bias)


def _ffn_kernel(x_ref, at_ref, p_ref, wo_hbm, g2_ref, wup_hbm, wdn_hbm, g3_ref, o_ref,
                wo_ref, wup_ref, wdn_ref, wide_stage, narrow_stage, stage_sem):
    sub = ROW_SUBTILE_FFN
    n_sub = x_ref.shape[0] // sub

    @pl.when(pl.program_id(0) == 0)
    def _():
        jobs = []
        for src, dst in ((wo_hbm, wo_ref), (wup_hbm, wup_ref), (wdn_hbm, wdn_ref)):
            stage = wide_stage if src.shape[1] == wide_stage.shape[2] else narrow_stage
            assert src.shape[1] == stage.shape[2] and src.shape[0] % stage.shape[1] == 0
            jobs += [(src, dst, stage, r0) for r0 in range(0, src.shape[0], stage.shape[1])]

        def copy(n):
            src, _, stage, r0 = jobs[n]
            return pltpu.make_async_copy(src.at[pl.ds(r0, stage.shape[1]), :], stage.at[n % 2],
                                         stage_sem.at[n % 2])

        copy(0).start()
        for n, (_, dst, stage, r0) in enumerate(jobs):
            if n + 1 < len(jobs):
                copy(n + 1).start()
            copy(n).wait()
            dst[r0:r0 + stage.shape[1], :] = stage[n % 2].astype(BF16)

    def head(i):
        rows = slice(i * sub, (i + 1) * sub)
        nblk = sub // MOBA_BLOCK
        attn = jnp.concatenate(
            [lax.dot_general(at_ref[i * nblk + b], wo_ref[:ATTN_WIDTH, :], TN_DIMS,
                             preferred_element_type=F32) for b in range(nblk)], axis=0)
        x1 = (x_ref[rows, :] + attn
              + jnp.dot(p_ref[rows, :], wo_ref[ATTN_WIDTH:, :], preferred_element_type=F32))
        o_ref[rows, :] = x1
        return _rms_norm(x1, g2_ref[...]).astype(BF16)

    def ffn(i, h):
        rows = slice(i * sub, (i + 1) * sub)
        for c in range(wup_ref.shape[1] // FF_CHUNK):
            cols = slice(c * FF_CHUNK, (c + 1) * FF_CHUNK)
            up = jnp.dot(h, wup_ref[:, cols], preferred_element_type=F32)
            act = jnp.square(jnp.maximum(up, 0.0)).astype(BF16)
            o_ref[rows, :] += jnp.dot(act, wdn_ref[cols, :], preferred_element_type=F32)

    def tail(i):
        rows = slice(i * sub, (i + 1) * sub)
        o_ref[rows, :] = _rms_norm(o_ref[rows, :], g3_ref[...])

    h_prev = head(0)
    for i in range(1, n_sub):
        h_next = head(i)
        ffn(i - 1, h_prev)
        tail(i - 1)
        h_prev = h_next
    ffn(n_sub - 1, h_prev)
    tail(n_sub - 1)


def _ffn_call(x, at, p, wo, g2, wup, wdn, g3):
    N, D = x.shape
    T = ROW_TILE_FFN
    const = lambda shape: pl.BlockSpec(shape, lambda t: (0,) * len(shape),
                                       pipeline_mode=pl.Buffered(1))
    in_hbm = pl.BlockSpec(memory_space=pl.ANY)
    d_ff = wup.shape[1]
    assert wup.shape == (D, d_ff) and wdn.shape == (d_ff, D) and wo.shape == (D, D)
    return pl.pallas_call(
        _ffn_kernel,
        grid=(N // T,),
        in_specs=[
            pl.BlockSpec((T, D), lambda t: (t, 0)),
            pl.BlockSpec((T // MOBA_BLOCK, ATTN_WIDTH, MOBA_BLOCK), lambda t: (t, 0, 0)),
            pl.BlockSpec((T, POOL_WIDTH), lambda t: (t, 0)),
            in_hbm, const((1, D)),
            in_hbm, in_hbm, const((1, D)),
        ],
        out_specs=pl.BlockSpec((T, D), lambda t: (t, 0)),
        out_shape=jax.ShapeDtypeStruct((N, D), F32),
        scratch_shapes=[
            pltpu.VMEM(wo.shape, BF16), pltpu.VMEM(wup.shape, BF16), pltpu.VMEM(wdn.shape, BF16),
            pltpu.VMEM((2, WEIGHT_STAGE_ELEMS // d_ff, d_ff), F32),
            pltpu.VMEM((2, WEIGHT_STAGE_ELEMS // D, D), F32),
            pltpu.SemaphoreType.DMA((2,)),
        ],
        compiler_params=pltpu.CompilerParams(
            dimension_semantics=("arbitrary",), vmem_limit_bytes=VMEM_LIMIT),
        name="outproj_ffn",
    )(x, at, p, wo, g2, wup, wdn, g3)


def _alibi_constants():
    slopes = (2.0 ** (-8.0 * np.arange(1, N_HEADS + 1) / N_HEADS) * LOG2E).astype(np.float32)
    key = np.arange(MOBA_BLOCK, dtype=np.float32)[:, None]
    qry = np.arange(MOBA_BLOCK, dtype=np.float32)[None, :]
    past = -slopes[:, None, None] * (qry - key)[None]
    own = np.where((key <= qry)[None], past, -np.inf).astype(np.float32)
    return jnp.asarray(slopes), jnp.asarray(np.stack([past, own], axis=1))


def kernel(x, norm_mix, w_in, w_pool, pool_scale, w_out, norm_mlp, w_up, w_down, norm_final):
    B, S, D = x.shape
    assert w_in.shape[0] == 1, "single trunk layer (the final norm is fused into the FFN call)"
    assert S % ROW_TILE_IN == 0 and (B * S) % ROW_TILE_FFN == 0 and S % MOBA_BLOCK == 0
    assert ROW_SUBTILE_IN % MOBA_BLOCK == 0 and ROW_TILE_IN % ROW_SUBTILE_IN == 0
    assert ROW_SUBTILE_FFN % MOBA_BLOCK == 0 and ROW_TILE_FFN % ROW_SUBTILE_FFN == 0
    assert w_in.shape[2] == 3 * ATTN_WIDTH + POOL_WIDTH and D == ATTN_WIDTH + POOL_WIDTH
    qt, k, vt, kmean, p = _inproj_call(
        x, norm_mix[0][None, :], w_in[0], w_pool[0], pool_scale[0][None, :])
    slopes, bias = _alibi_constants()
    at = _attn_call(slopes, bias, qt, k, vt, kmean)
    y = _ffn_call(x.reshape(B * S, D), at.reshape(B * (S // MOBA_BLOCK), ATTN_WIDTH, MOBA_BLOCK),
                  p.reshape(B * S, POOL_WIDTH), w_out[0],
                  norm_mlp[0][None, :], w_up[0], w_down[0], norm_final[None, :])
    return y.reshape(B, S, D)
```

```python
import functools

import jax
import jax.numpy as jnp
import numpy as np
from jax import lax
from jax.experimental import pallas as pl
from jax.experimental.pallas import tpu as pltpu

F32 = jnp.float32
BF16 = jnp.bfloat16

N_HEADS = 8
HEAD_DIM = 64
ATTN_WIDTH = N_HEADS * HEAD_DIM
POOL_WINDOWS = (2, 4, 8, 16)
POOL_GROUP = 128
POOL_WIDTH = POOL_GROUP * len(POOL_WINDOWS)
MOBA_BLOCK = 256
MOBA_TOPK = 3
EPS = 1e-6
LOG2E = 1.4426950408889634
QK_SCALE = HEAD_DIM ** -0.5 * LOG2E
SUM_ROWS = 16
POOL_HALO = 16
LANES = 128
HEADS_PER_GROUP = LANES // HEAD_DIM
GROUPS_PER_STEP = 2
HEADS_PER_STEP = GROUPS_PER_STEP * HEADS_PER_GROUP
GATE_CHUNK = 4
ITEMS_PER_TRIP = 2
STAGE_SLOTS = 2
ACC_BUFFERS = 2

ROW_TILE_IN = 2048
ROW_SUBTILE_IN = 256
ROW_TILE_FFN = 1024
ROW_SUBTILE_FFN = 512
FF_CHUNK = 1024
WEIGHT_STAGE_ELEMS = 512 * 1024
VMEM_LIMIT = 56 * 1024 * 1024

NT_DIMS = (((1,), (1,)), ((), ()))
TN_DIMS = (((0,), (0,)), ((), ()))


def _rms_norm(x, g):
    ms = jnp.mean(x * x, axis=-1, keepdims=True)
    return x * lax.rsqrt(ms + EPS) * g


def _inproj_kernel(x_ref, g_ref, wi_hbm, wpool_ref, pscale_ref,
                   qt_ref, k_ref, vt_ref, kmean_ref, p_ref,
                   halo_ref, wi_ref, wqvt_ref, stage_ref, stage_sem):
    t = pl.program_id(1)
    rows = x_ref.shape[1]
    sub = ROW_SUBTILE_IN
    nblk = sub // MOBA_BLOCK
    u_cols = slice(3 * ATTN_WIDTH, 3 * ATTN_WIDTH + POOL_WIDTH)

    @pl.when((pl.program_id(0) == 0) & (t == 0))
    def _():
        chunk = stage_ref.shape[1]
        n_chunks = wi_hbm.shape[0] // chunk

        def copy(n):
            return pltpu.make_async_copy(wi_hbm.at[pl.ds(n * chunk, chunk), :], stage_ref.at[n % 2],
                                         stage_sem.at[n % 2])

        copy(0).start()
        for n in range(n_chunks):
            if n + 1 < n_chunks:
                copy(n + 1).start()
            copy(n).wait()
            wi_ref[n * chunk:(n + 1) * chunk, :] = stage_ref[n % 2].astype(BF16)
        wqvt_ref[:ATTN_WIDTH, :] = wi_ref[:, :ATTN_WIDTH].T
        wqvt_ref[ATTN_WIDTH:, :] = wi_ref[:, 2 * ATTN_WIDTH:3 * ATTN_WIDTH].T

    @pl.when(t == 0)
    def _():
        halo_ref[...] = jnp.zeros_like(halo_ref)

    def project(i):
        r0 = i * sub
        h = _rms_norm(x_ref[0, r0:r0 + sub, :], g_ref[...]).astype(BF16)
        kf = jnp.dot(h, wi_ref[:, ATTN_WIDTH:2 * ATTN_WIDTH], preferred_element_type=F32)
        k_ref[0, r0:r0 + sub, :] = kf.astype(BF16)
        for b in range(nblk):
            blk_row = (t * (rows // sub) + i) * nblk + b
            kmean_ref[0, pl.ds(blk_row, 1), :] = (
                jnp.sum(kf[b * MOBA_BLOCK:(b + 1) * MOBA_BLOCK], axis=0, keepdims=True)
                * (1.0 / MOBA_BLOCK))
        qvt = lax.dot_general(wqvt_ref[...], h, NT_DIMS, preferred_element_type=F32)
        for b in range(nblk):
            cols = slice(b * MOBA_BLOCK, (b + 1) * MOBA_BLOCK)
            qt_ref[0, i * nblk + b] = (qvt[:ATTN_WIDTH, cols] * QK_SCALE).astype(BF16)
            vt_ref[0, i * nblk + b] = qvt[ATTN_WIDTH:, cols].astype(BF16)
        return jnp.dot(h, wi_ref[:, u_cols], preferred_element_type=F32)

    def pool(i, u, halo):
        r0 = i * sub
        ext = jnp.concatenate([halo, u], axis=0)
        pos = t * rows + r0 + lax.broadcasted_iota(jnp.int32, (sub, POOL_GROUP), 0)
        for g, w in enumerate(POOL_WINDOWS):
            cols = slice(g * POOL_GROUP, (g + 1) * POOL_GROUP)
            s = ext[:, cols]
            shift = 1
            while shift < w:
                s = s + pltpu.roll(s, shift, axis=0)
                shift *= 2
            cnt = jnp.minimum(pos + 1, w).astype(F32)
            mixed = s[POOL_HALO:, :] / cnt - u[:, cols]
            y = jnp.dot(mixed.astype(BF16), wpool_ref[g].astype(BF16), preferred_element_type=F32)
            p_ref[0, r0:r0 + sub, cols] = (y * pscale_ref[:, cols]).astype(BF16)
        return u[sub - POOL_HALO:, :]

    n_sub = rows // sub
    halo = halo_ref[...]
    u_prev = project(0)
    for i in range(1, n_sub):
        u_next = project(i)
        halo = pool(i - 1, u_prev, halo)
        u_prev = u_next
    halo_ref[...] = pool(n_sub - 1, u_prev, halo)


def _inproj_call(x, g, wi, wpool, pscale):
    B, S, D = x.shape
    T = ROW_TILE_IN
    nblk = T // MOBA_BLOCK
    nb = S // MOBA_BLOCK
    const = lambda shape: pl.BlockSpec(shape, lambda b, t: (0,) * len(shape),
                                       pipeline_mode=pl.Buffered(1))
    return pl.pallas_call(
        _inproj_kernel,
        grid=(B, S // T),
        in_specs=[
            pl.BlockSpec((1, T, D), lambda b, t: (b, t, 0)),
            const((1, D)),
            pl.BlockSpec(memory_space=pl.ANY),
            const(wpool.shape),
            const((1, POOL_WIDTH)),
        ],
        out_specs=[
            pl.BlockSpec((1, nblk, ATTN_WIDTH, MOBA_BLOCK), lambda b, t: (b, t, 0, 0)),
            pl.BlockSpec((1, T, ATTN_WIDTH), lambda b, t: (b, t, 0)),
            pl.BlockSpec((1, nblk, ATTN_WIDTH, MOBA_BLOCK), lambda b, t: (b, t, 0, 0)),
            pl.BlockSpec((1, nb, ATTN_WIDTH), lambda b, t: (b, 0, 0)),
            pl.BlockSpec((1, T, POOL_WIDTH), lambda b, t: (b, t, 0)),
        ],
        out_shape=[
            jax.ShapeDtypeStruct((B, nb, ATTN_WIDTH, MOBA_BLOCK), BF16),
            jax.ShapeDtypeStruct((B, S, ATTN_WIDTH), BF16),
            jax.ShapeDtypeStruct((B, nb, ATTN_WIDTH, MOBA_BLOCK), BF16),
            jax.ShapeDtypeStruct((B, nb, ATTN_WIDTH), F32),
            jax.ShapeDtypeStruct((B, S, POOL_WIDTH), BF16),
        ],
        scratch_shapes=[
            pltpu.VMEM((POOL_HALO, POOL_WIDTH), F32),
            pltpu.VMEM(wi.shape, BF16),
            pltpu.VMEM((2 * ATTN_WIDTH, D), BF16),
            pltpu.VMEM((2, WEIGHT_STAGE_ELEMS // wi.shape[1], wi.shape[1]), F32),
            pltpu.SemaphoreType.DMA((2,)),
        ],
        compiler_params=pltpu.CompilerParams(
            dimension_semantics=("arbitrary", "arbitrary"), vmem_limit_bytes=VMEM_LIMIT),
        name="inproj_pool",
    )(x, g, wi, wpool, pscale)


_F_QBLK, _F_JA, _F_JB, _F_A_VALID, _F_B_VALID, _F_FIRST, _F_LAST = range(7)


def _attn_schedule(nb):
    items = []
    for i in range(nb):
        steps = (i + 2) // 2
        for t in range(steps):
            jb = i - 2 * t - 1
            items.append((i, i - 2 * t, max(jb, 0), 1, int(jb >= 0), int(t == 0), int(t == steps - 1)))
    n_items = len(items)
    items += [(nb - 1, 0, 0, 0, 0, 0, 0)] * 2
    return n_items, np.asarray(items, np.int32).T.reshape(-1)


def _attn_kernel(n_items, slopes_ref, sched_ref, qt_ref, k_ref, vt_ref, kmean_ref, bias_ref, o_ref,
                 shift_ref, s_ref, mx_ref, p_ref, alpha_ref, m_ref, acc_ref):
    step = pl.program_id(1)
    nb = kmean_ref.shape[1]
    seq = k_ref.shape[1]
    blk = MOBA_BLOCK
    neg_inf = -jnp.inf
    heads = range(HEADS_PER_STEP)
    groups = [tuple(range(g * HEADS_PER_GROUP, (g + 1) * HEADS_PER_GROUP)) for g in range(GROUPS_PER_STEP)]
    stride = n_items + 2

    def group_lanes(hh):
        g = hh // HEADS_PER_GROUP
        return slice(g * LANES, (g + 1) * LANES)

    def head_weights(hh, rows):
        pad = jnp.zeros_like(rows)
        return jnp.concatenate([rows, pad] if hh % HEADS_PER_GROUP == 0 else [pad, rows], axis=0)

    def head_rows(hh):
        return slice(hh * HEAD_DIM, (hh + 1) * HEAD_DIM)

    def field(f, w):
        return sched_ref[f * stride + w]

    kmean = kmean_ref[0].astype(BF16)
    width = GATE_CHUNK * blk
    key_blk = lax.broadcasted_iota(jnp.int32, (nb, width), 0)
    key_idx = key_blk.astype(F32)
    lane_blk = lax.broadcasted_iota(jnp.int32, (nb, width), 1) // blk
    for c in range(nb // GATE_CHUNK):
        qry_blk = lane_blk + c * GATE_CHUNK
        dist = ((qry_blk - key_blk) * blk).astype(F32)
        all_past_selected = (c + 1) * GATE_CHUNK - 1 <= MOBA_TOPK
        for hh in heads:
            if all_past_selected:
                sel = key_blk <= qry_blk
            else:
                qt_c = jnp.concatenate([qt_ref[0, c * GATE_CHUNK + i, head_rows(hh), :]
                                        for i in range(GATE_CHUNK)], axis=1)
                gate = jnp.dot(kmean[:, group_lanes(hh)], head_weights(hh, qt_c),
                               preferred_element_type=F32)
                gate = jnp.where(key_blk < qry_blk, gate, neg_inf)
                sel = key_blk == qry_blk
                for _ in range(MOBA_TOPK):
                    top = jnp.max(gate, axis=0, keepdims=True)
                    first = jnp.min(jnp.where(gate == top, key_idx, float(nb)), axis=0, keepdims=True)
                    pick = (key_idx == first) & (top > neg_inf)
                    sel = sel | pick
                    gate = jnp.where(pick, neg_inf, gate)
            slope = slopes_ref[step * HEADS_PER_STEP + hh]
            shift = jnp.where(sel, slope * dist, jnp.inf)
            for i in range(GATE_CHUNK):
                shift_ref[hh, c * GATE_CHUNK + i] = shift[:, i * blk:(i + 1) * blk]

    def score_dot(w, hh):
        i, ja, jb = field(_F_QBLK, w), field(_F_JA, w), field(_F_JB, w)
        kk = jnp.concatenate(
            [k_ref[0, pl.ds(pl.multiple_of(ja * blk, blk), blk), group_lanes(hh)],
             k_ref[0, pl.ds(pl.multiple_of(jb * blk, blk), blk), group_lanes(hh)]], axis=0)
        return jnp.dot(kk, head_weights(hh, qt_ref[0, i, head_rows(hh), :]),
                       preferred_element_type=F32)

    def stage_unit(st, w, slot, hh, u):
        table = field(_F_FIRST, w) if u == 0 else 0
        s_u = st[u * blk:(u + 1) * blk] + bias_ref[hh, table]
        s_ref[slot, hh, u] = s_u
        mx_ref[slot, hh, u] = jnp.max(s_u, axis=0, keepdims=True)

    def stage_scores(sts, w, slot, hhs):
        for hh in hhs:
            for u in range(2):
                stage_unit(sts[hh], w, slot, hh, u)

    def softmax_prep(w, slot, hh):
        i, ja, jb = field(_F_QBLK, w), field(_F_JA, w), field(_F_JB, w)
        a_valid, b_valid = field(_F_A_VALID, w) == 1, field(_F_B_VALID, w) == 1
        first = field(_F_FIRST, w) == 1
        sh_a = jnp.where(a_valid, shift_ref[hh, i, pl.ds(ja, 1), :], jnp.inf)
        sh_b = jnp.where(b_valid, shift_ref[hh, i, pl.ds(jb, 1), :], jnp.inf)
        m_old = jnp.where(first, neg_inf, m_ref[hh])
        m_new = jnp.maximum(m_old, jnp.maximum(mx_ref[slot, hh, 0] - sh_a,
                                               mx_ref[slot, hh, 1] - sh_b))
        m_ref[hh] = m_new
        alpha_ref[slot, hh] = jnp.exp2(m_old - m_new)
        return m_new + sh_a, m_new + sh_b

    def softmax_block(slot, hh, u, offset):
        p_ref[slot, hh, u * blk:(u + 1) * blk] = jnp.exp2(s_ref[slot, hh, u] - offset).astype(BF16)

    def softmax(w, slot):
        for hh in heads:
            offsets = softmax_prep(w, slot, hh)
            for u in range(2):
                softmax_block(slot, hh, u, offsets[u])

    ones_rows = jnp.ones((SUM_ROWS, 2 * blk), BF16)

    def pv_dot(w, slot, hh):
        ja, jb = field(_F_JA, w), field(_F_JB, w)
        rows = slice(hh * HEAD_DIM, (hh + 1) * HEAD_DIM)
        vts = jnp.concatenate([vt_ref[0, ja, rows, :], vt_ref[0, jb, rows, :]], axis=1)
        lhs = jnp.concatenate([vts, ones_rows], axis=0)
        return jnp.dot(lhs, p_ref[slot, hh], preferred_element_type=F32)

    def accumulate(pvs, w, slot):
        par = field(_F_QBLK, w) % ACC_BUFFERS
        for hh in heads:
            acc_ref[par, hh] = alpha_ref[slot, hh] * acc_ref[par, hh] + pvs[hh]

    def finalize(w):
        i = field(_F_QBLK, w)
        par = i % ACC_BUFFERS
        for hh in heads:
            inv_l = 1.0 / acc_ref[par, hh, HEAD_DIM:HEAD_DIM + 1]
            o_ref[0, i, head_rows(hh), :] = (acc_ref[par, hh, :HEAD_DIM] * inv_l).astype(o_ref.dtype)

    acc_ref[...] = jnp.zeros_like(acc_ref)
    m_ref[...] = jnp.zeros_like(m_ref)

    stage_scores([score_dot(0, hh) for hh in heads], 0, 0, heads)
    stage_scores([score_dot(1, hh) for hh in heads], 1, 1, heads)
    softmax(0, 0)

    def item(w, j):
        slot, sm, sc = j % STAGE_SLOTS, (j + 1) % STAGE_SLOTS, (j + 2) % STAGE_SLOTS
        sts, pvs = {}, {}
        for g, hhs in enumerate(groups):
            for hh in hhs:
                sts[hh] = score_dot(w + 2, hh)
            offs = {hh: softmax_prep(w + 1, sm, hh) for hh in hhs}
            if g > 0:
                stage_scores(sts, w + 2, sc, groups[g - 1])
            for hh in hhs:
                for u in range(2):
                    softmax_block(sm, hh, u, offs[hh][u])
        for hh in heads:
            pvs[hh] = pv_dot(w, slot, hh)
        stage_scores(sts, w + 2, sc, groups[-1])
        accumulate(pvs, w, slot)

    def body(trip, carry):
        for j in range(ITEMS_PER_TRIP):
            item(ITEMS_PER_TRIP * trip + j, j)
        for j in range(ITEMS_PER_TRIP):
            w = ITEMS_PER_TRIP * trip + j
            pl.when(field(_F_LAST, w) == 1)(functools.partial(finalize, w))
        return carry

    assert n_items % ITEMS_PER_TRIP == 0 and ITEMS_PER_TRIP % STAGE_SLOTS == 0
    lax.fori_loop(0, n_items // ITEMS_PER_TRIP, body, 0)


def _attn_call(slopes, bias, qt, k, vt, kmean):
    B, S, W = k.shape
    nb = S // MOBA_BLOCK
    step_lanes = GROUPS_PER_STEP * LANES
    n_steps = W // step_lanes
    hps = HEADS_PER_STEP
    n_items, sched = _attn_schedule(nb)
    grid_spec = pltpu.PrefetchScalarGridSpec(
        num_scalar_prefetch=2,
        grid=(B, n_steps),
        in_specs=[
            pl.BlockSpec((1, nb, step_lanes, MOBA_BLOCK), lambda b, p, *_: (b, 0, p, 0)),
            pl.BlockSpec((1, S, step_lanes), lambda b, p, *_: (b, 0, p)),
            pl.BlockSpec((1, nb, step_lanes, MOBA_BLOCK), lambda b, p, *_: (b, 0, p, 0)),
            pl.BlockSpec((1, nb, step_lanes), lambda b, p, *_: (b, 0, p)),
            pl.BlockSpec((hps, 2, MOBA_BLOCK, MOBA_BLOCK), lambda b, p, *_: (p, 0, 0, 0)),
        ],
        out_specs=pl.BlockSpec((1, nb, step_lanes, MOBA_BLOCK), lambda b, p, *_: (b, 0, p, 0)),
        scratch_shapes=[
            pltpu.VMEM((hps, nb, nb, MOBA_BLOCK), F32),
            pltpu.VMEM((STAGE_SLOTS, hps, 2, MOBA_BLOCK, MOBA_BLOCK), F32),
            pltpu.VMEM((STAGE_SLOTS, hps, 2, 1, MOBA_BLOCK), F32),
            pltpu.VMEM((STAGE_SLOTS, hps, 2 * MOBA_BLOCK, MOBA_BLOCK), BF16),
            pltpu.VMEM((STAGE_SLOTS, hps, 1, MOBA_BLOCK), F32),
            pltpu.VMEM((hps, 1, MOBA_BLOCK), F32),
            pltpu.VMEM((ACC_BUFFERS, hps, HEAD_DIM + SUM_ROWS, MOBA_BLOCK), F32),
        ],
    )
    return pl.pallas_call(
        functools.partial(_attn_kernel, n_items),
        grid_spec=grid_spec,
        out_shape=jax.ShapeDtypeStruct((B, nb, W, MOBA_BLOCK), BF16),
        compiler_params=pltpu.CompilerParams(
            dimension_semantics=("arbitrary", "arbitrary"), vmem_limit_bytes=VMEM_LIMIT),
        name="moba_attn",
    )(slopes, jnp.asarray(sched), qt, k, vt, kmean, bias)


def _ffn_kernel(x_ref, at_ref, p_ref, wo_hbm, g2_ref, wup_hbm, wdn_hbm, g3_ref, o_ref,
                wo_ref, wup_ref, wdn_ref, wide_stage, narrow_stage, stage_sem):
    sub = ROW_SUBTILE_FFN
    n_sub = x_ref.shape[0] // sub

    @pl.when(pl.program_id(0) == 0)
    def _():
        jobs = []
        for src, dst in ((wo_hbm, wo_ref), (wup_hbm, wup_ref), (wdn_hbm, wdn_ref)):
            stage = wide_stage if src.shape[1] == wide_stage.shape[2] else narrow_stage
            assert src.shape[1] == stage.shape[2] and src.shape[0] % stage.shape[1] == 0
            jobs += [(src, dst, stage, r0) for r0 in range(0, src.shape[0], stage.shape[1])]

        def copy(n):
            src, _, stage, r0 = jobs[n]
            return pltpu.make_async_copy(src.at[pl.ds(r0, stage.shape[1]), :], stage.at[n % 2],
                                         stage_sem.at[n % 2])

        copy(0).start()
        for n, (_, dst, stage, r0) in enumerate(jobs):
            if n + 1 < len(jobs):
                copy(n + 1).start()
            copy(n).wait()
            dst[r0:r0 + stage.shape[1], :] = stage[n % 2].astype(BF16)

    def head(i):
        rows = slice(i * sub, (i + 1) * sub)
        nblk = sub // MOBA_BLOCK
        attn = jnp.concatenate(
            [lax.dot_general(at_ref[i * nblk + b], wo_ref[:ATTN_WIDTH, :], TN_DIMS,
                             preferred_element_type=F32) for b in range(nblk)], axis=0)
        x1 = (x_ref[rows, :] + attn
              + jnp.dot(p_ref[rows, :], wo_ref[ATTN_WIDTH:, :], preferred_element_type=F32))
        o_ref[rows, :] = x1
        return _rms_norm(x1, g2_ref[...]).astype(BF16)

    def ffn(i, h):
        rows = slice(i * sub, (i + 1) * sub)
        for c in range(wup_ref.shape[1] // FF_CHUNK):
            cols = slice(c * FF_CHUNK, (c + 1) * FF_CHUNK)
            up = jnp.dot(h, wup_ref[:, cols], preferred_element_type=F32)
            act = jnp.square(jnp.maximum(up, 0.0)).astype(BF16)
            o_ref[rows, :] += jnp.dot(act, wdn_ref[cols, :], preferred_element_type=F32)

    def tail(i):
        rows = slice(i * sub, (i + 1) * sub)
        o_ref[rows, :] = _rms_norm(o_ref[rows, :], g3_ref[...])

    h_prev = head(0)
    for i in range(1, n_sub):
        h_next = head(i)
        ffn(i - 1, h_prev)
        tail(i - 1)
        h_prev = h_next
    ffn(n_sub - 1, h_prev)
    tail(n_sub - 1)


def _ffn_call(x, at, p, wo, g2, wup, wdn, g3):
    N, D = x.shape
    T = ROW_TILE_FFN
    const = lambda shape: pl.BlockSpec(shape, lambda t: (0,) * len(shape),
                                       pipeline_mode=pl.Buffered(1))
    in_hbm = pl.BlockSpec(memory_space=pl.ANY)
    d_ff = wup.shape[1]
    assert wup.shape == (D, d_ff) and wdn.shape == (d_ff, D) and wo.shape == (D, D)
    return pl.pallas_call(
        _ffn_kernel,
        grid=(N // T,),
        in_specs=[
            pl.BlockSpec((T, D), lambda t: (t, 0)),
            pl.BlockSpec((T // MOBA_BLOCK, ATTN_WIDTH, MOBA_BLOCK), lambda t: (t, 0, 0)),
            pl.BlockSpec((T, POOL_WIDTH), lambda t: (t, 0)),
            in_hbm, const((1, D)),
            in_hbm, in_hbm, const((1, D)),
        ],
        out_specs=pl.BlockSpec((T, D), lambda t: (t, 0)),
        out_shape=jax.ShapeDtypeStruct((N, D), F32),
        scratch_shapes=[
            pltpu.VMEM(wo.shape, BF16), pltpu.VMEM(wup.shape, BF16), pltpu.VMEM(wdn.shape, BF16),
            pltpu.VMEM((2, WEIGHT_STAGE_ELEMS // d_ff, d_ff), F32),
            pltpu.VMEM((2, WEIGHT_STAGE_ELEMS // D, D), F32),
            pltpu.SemaphoreType.DMA((2,)),
        ],
        compiler_params=pltpu.CompilerParams(
            dimension_semantics=("arbitrary",), vmem_limit_bytes=VMEM_LIMIT),
        name="outproj_ffn",
    )(x, at, p, wo, g2, wup, wdn, g3)


def _alibi_constants():
    slopes = (2.0 ** (-8.0 * np.arange(1, N_HEADS + 1) / N_HEADS) * LOG2E).astype(np.float32)
    key = np.arange(MOBA_BLOCK, dtype=np.float32)[:, None]
    qry = np.arange(MOBA_BLOCK, dtype=np.float32)[None, :]
    past = -slopes[:, None, None] * (qry - key)[None]
    own = np.where((key <= qry)[None], past, -np.inf).astype(np.float32)
    return jnp.asarray(slopes), jnp.asarray(np.stack([past, own], axis=1))


def kernel(x, norm_mix, w_in, w_pool, pool_scale, w_out, norm_mlp, w_up, w_down, norm_final):
    B, S, D = x.shape
    assert w_in.shape[0] == 1, "single trunk layer (the final norm is fused into the FFN call)"
    assert S % ROW_TILE_IN == 0 and (B * S) % ROW_TILE_FFN == 0 and S % MOBA_BLOCK == 0
    assert ROW_SUBTILE_IN % MOBA_BLOCK == 0 and ROW_TILE_IN % ROW_SUBTILE_IN == 0
    assert ROW_SUBTILE_FFN % MOBA_BLOCK == 0 and ROW_TILE_FFN % ROW_SUBTILE_FFN == 0
    assert w_in.shape[2] == 3 * ATTN_WIDTH + POOL_WIDTH and D == ATTN_WIDTH + POOL_WIDTH
    qt, k, vt, kmean, p = _inproj_call(
        x, norm_mix[0][None, :], w_in[0], w_pool[0], pool_scale[0][None, :])
    slopes, bias = _alibi_constants()
    at = _attn_call(slopes, bias, qt, k, vt, kmean)
    y = _ffn_call(x.reshape(B * S, D), at.reshape(B * (S // MOBA_BLOCK), ATTN_WIDTH, MOBA_BLOCK),
                  p.reshape(B * S, POOL_WIDTH), w_out[0],
                  norm_mlp[0][None, :], w_up[0], w_down[0], norm_final[None, :])
    return y.reshape(B, S, D)
```
